```python
import math
import jax, jax.numpy as jnp
from jax import lax
import numpy as np

D_MODEL = 1024
BATCH = 8
SEQ = 2048
DEPTH = 2
DEC_BATCH = 128
DEC_SEQ = 4
PAST_LEN = 8192
PAGE_SIZE = 128

N_HEADS = 8
QK_NOPE_DIM = 128
QK_ROPE_DIM = 64
V_HEAD_DIM = 128
Q_LORA_RANK = 384
KV_LORA_RANK = 256
ROPE_THETA = 10000.0
Q_BLOCK = 128
D_SSM = D_MODEL
SSM_GROUP = 16
N_GROUPS = D_SSM // SSM_GROUP
SSM_STATE = 64
DT_MIN = 1e-3
DT_MAX = 1e-1
D_FF = 2816
N_MOD = 9
EPS = 1e-6
IN_SPLITS = [Q_LORA_RANK,
             Q_LORA_RANK + KV_LORA_RANK,
             Q_LORA_RANK + KV_LORA_RANK + QK_ROPE_DIM,
             Q_LORA_RANK + KV_LORA_RANK + QK_ROPE_DIM + D_SSM,
             Q_LORA_RANK + KV_LORA_RANK + QK_ROPE_DIM + D_SSM + D_MODEL]
IN_WIDTH = Q_LORA_RANK + KV_LORA_RANK + QK_ROPE_DIM + D_SSM + 2 * D_MODEL

kernel_name = "mla_s5_gated_macaron_decode_step"


def rms_norm(x, g):
    xf = x.astype(jnp.float32)
    y = xf * lax.rsqrt(jnp.mean(xf * xf, axis=-1, keepdims=True) + EPS)
    return (y * g.astype(jnp.float32)).astype(x.dtype)


def rope(x, pos):
    half = x.shape[-1] // 2
    inv = ROPE_THETA ** (-jnp.arange(half, dtype=jnp.float32) / half)
    ang = pos.astype(jnp.float32)[:, None] * inv[None, :]
    shp = (ang.shape[0],) + (1,) * (x.ndim - 3) + (half,)
    cos = jnp.cos(ang).reshape(shp)
    sin = jnp.sin(ang).reshape(shp)
    xf = x.astype(jnp.float32)
    x1, x2 = xf[..., :half], xf[..., half:]
    return jnp.concatenate([x1 * cos - x2 * sin, x1 * sin + x2 * cos], axis=-1).astype(x.dtype)


def swiglu(h, w_up, w_down):
    a, b = jnp.split(h @ w_up, 2, axis=-1)
    return (jax.nn.silu(a) * b) @ w_down


def attend_prompt(q_lat, q_rope, ckv, kr):
    bsz, t, h, lat = q_lat.shape
    nb = t // Q_BLOCK
    ql = q_lat.reshape(bsz, nb, Q_BLOCK, h, lat).transpose(1, 0, 2, 3, 4)
    qr = q_rope.reshape(bsz, nb, Q_BLOCK, h, QK_ROPE_DIM).transpose(1, 0, 2, 3, 4)
    k_pos = jnp.arange(t)

    def one_block(args):
        ql_b, qr_b, i = args
        s = (jnp.einsum('bqhl,bsl->bhqs', ql_b, ckv)
             + jnp.einsum('bqhr,bsr->bhqs', qr_b, kr)).astype(jnp.float32)
        q_pos = i * Q_BLOCK + jnp.arange(Q_BLOCK)
        mask = k_pos[None, :] <= q_pos[:, None]
        p = jax.nn.softmax(jnp.where(mask, s, -jnp.inf), axis=-1).astype(ckv.dtype)
        return jnp.einsum('bhqs,bsl->bqhl', p, ckv)

    o = lax.map(one_block, (ql, qr, jnp.arange(nb)))
    return o.transpose(1, 0, 2, 3, 4).reshape(bsz, t, h, lat)


def attend_sample(q_lat, q_rope, ckv, kr, ckv_past, kr_past):
    t = q_lat.shape[1]
    past = ckv_past.shape[1]
    s_past = (jnp.einsum('bthl,bsl->bhts', q_lat, ckv_past)
              + jnp.einsum('bthr,bsr->bhts', q_rope, kr_past)).astype(jnp.float32)
    s_new = (jnp.einsum('bthl,bsl->bhts', q_lat, ckv)
             + jnp.einsum('bthr,bsr->bhts', q_rope, kr)).astype(jnp.float32)
    causal = jnp.tril(jnp.ones((t, t), dtype=bool))
    s_new = jnp.where(causal, s_new, -jnp.inf)
    p = jax.nn.softmax(jnp.concatenate([s_past, s_new], axis=-1), axis=-1).astype(ckv.dtype)
    return (jnp.einsum('bhts,bsl->bthl', p[..., :past], ckv_past)
            + jnp.einsum('bhts,bsl->bthl', p[..., past:], ckv))


def _diag_combine(e1, e2):
    a1, b1 = e1
    a2, b2 = e2
    return a1 * a2, a2 * b1 + b2


def s5_branch(u, a_re, a_im, log_dt, b_re, b_im, c_re, c_im, d_skip, w_glu, h0_re, h0_im):
    bsz, t, _ = u.shape
    f32 = jnp.float32
    uf = u.astype(f32).reshape(bsz, t, N_GROUPS, SSM_GROUP)
    lam = lax.complex(a_re.astype(f32), a_im.astype(f32))
    lam_dt = lam * jnp.exp(log_dt.astype(f32))[:, None]
    lam_bar = jnp.exp(lam_dt)
    b_bar = ((lam_bar - 1.0) / lam)[..., None] * lax.complex(b_re.astype(f32), b_im.astype(f32))
    bu = lax.complex(jnp.einsum('gpc,btgc->tbgp', b_bar.real, uf),
                     jnp.einsum('gpc,btgc->tbgp', b_bar.imag, uf))
    a = jnp.broadcast_to(lam_bar, (t, 1) + lam_bar.shape)
    _, hs = lax.associative_scan(_diag_combine, (a, bu), axis=0)
    if h0_re is not None:
        steps = jnp.arange(1, t + 1, dtype=f32)[:, None, None, None]
        h0 = lax.complex(h0_re.astype(f32), h0_im.astype(f32))
        hs = hs + jnp.exp(lam_dt[None, None] * steps) * h0[None]
    y = (jnp.einsum('gcp,tbgp->btgc', c_re.astype(f32), hs.real)
         - jnp.einsum('gcp,tbgp->btgc', c_im.astype(f32), hs.imag))
    y = (y + d_skip.astype(f32).reshape(N_GROUPS, SSM_GROUP) * uf).reshape(bsz, t, D_SSM)
    z = jax.nn.gelu(y).astype(u.dtype)
    ga, gb = jnp.split(z @ w_glu, 2, axis=-1)
    h_last = hs[-1]
    return ga * jax.nn.sigmoid(gb), h_last.real, h_last.imag


def layer(x, c, pos, past_ckv, past_kr, h0_re, h0_im,
          ada_w, ada_b, norm_ffn1, ffn1_up, ffn1_down, norm_mix, w_in,
          q_norm, w_uq, kv_norm, w_uk, w_uv,
          ssm_a_re, ssm_a_im, ssm_log_dt, ssm_b_re, ssm_b_im, ssm_c_re, ssm_c_im, ssm_d, w_glu,
          w_out, norm_ffn2, ffn2_up, ffn2_down):
    bsz, t, _ = x.shape
    mod = (jax.nn.silu(c) @ ada_w + ada_b)[:, None, :]
    sh1, sc1, g1, shm, scm, gm, sh2, sc2, g2 = jnp.split(mod, N_MOD, axis=-1)
    h = rms_norm(x, norm_ffn1) * (1.0 + sc1) + sh1
    x = x + 0.5 * g1 * swiglu(h, ffn1_up, ffn1_down)
    h = rms_norm(x, norm_mix) * (1.0 + scm) + shm
    cq, ckv, kr, u, gate_a, gate_b = jnp.split(h @ w_in, IN_SPLITS, axis=-1)
    q = (rms_norm(cq, q_norm) @ w_uq).reshape(bsz, t, N_HEADS, QK_NOPE_DIM + QK_ROPE_DIM)
    scale = (QK_NOPE_DIM + QK_ROPE_DIM) ** -0.5
    q_lat = jnp.einsum('bthn,lhn->bthl', q[..., :QK_NOPE_DIM], w_uk) * scale
    q_rope = rope(q[..., QK_NOPE_DIM:], pos) * scale
    ckv = rms_norm(ckv, kv_norm)
    kr = rope(kr, pos)
    if past_ckv is None:
        o_lat = attend_prompt(q_lat, q_rope, ckv, kr)
    else:
        o_lat = attend_sample(q_lat, q_rope, ckv, kr, past_ckv, past_kr)
    y_attn = jnp.einsum('bthl,lhv->bthv', o_lat, w_uv).reshape(bsz, t, N_HEADS * V_HEAD_DIM)
    y_ssm, h_re, h_im = s5_branch(u, ssm_a_re, ssm_a_im, ssm_log_dt, ssm_b_re, ssm_b_im,
                                  ssm_c_re, ssm_c_im, ssm_d, w_glu, h0_re, h0_im)
    mixed = jax.nn.sigmoid(gate_a) * y_attn + jax.nn.sigmoid(gate_b) * y_ssm
    x = x + gm * (mixed @ w_out)
    h = rms_norm(x, norm_ffn2) * (1.0 + sc2) + sh2
    x = x + 0.5 * g2 * swiglu(h, ffn2_up, ffn2_down)
    return x, ckv, kr, h_re, h_im


def setup_inputs(seed: int = 0) -> dict:
    key = jax.random.key(seed)
    ks = iter(jax.random.split(key, 48))
    f32 = jnp.float32

    def nrm(shape, scale):
        return jax.random.normal(next(ks), shape, f32) * scale

    n_pages = PAST_LEN // PAGE_SIZE
    n_used = DEC_BATCH * n_pages
    n_phys = (5 * n_used + 3) // 4
    page_table = jax.random.permutation(next(ks), n_phys)[:n_used].reshape(DEC_BATCH, n_pages).astype(jnp.int32)
    a_im0 = jnp.pi * jnp.arange(SSM_STATE, dtype=f32)
    return {
        "x_prompt": nrm((BATCH, SEQ, D_MODEL), 1.0),
        "x_sample": nrm((DEC_BATCH, DEC_SEQ, D_MODEL), 1.0),
        "c_prompt": nrm((BATCH, D_MODEL), 1.0),
        "c_sample": nrm((DEC_BATCH, D_MODEL), 1.0),
        "cache_ckv": nrm((DEPTH, n_phys, PAGE_SIZE, KV_LORA_RANK), 1.0),
        "cache_kr": nrm((DEPTH, n_phys, PAGE_SIZE, QK_ROPE_DIM), 1.0),
        "state_ssm_re": nrm((DEPTH, DEC_BATCH, N_GROUPS, SSM_STATE), 0.3),
        "state_ssm_im": nrm((DEPTH, DEC_BATCH, N_GROUPS, SSM_STATE), 0.3),
        "page_table": page_table,
        "ada_w": nrm((DEPTH, D_MODEL, N_MOD * D_MODEL), 0.5 * D_MODEL ** -0.5),
        "ada_b": nrm((DEPTH, N_MOD * D_MODEL), 0.01),
        "norm_ffn1": 1.0 + nrm((DEPTH, D_MODEL), 0.02),
        "ffn1_up": nrm((DEPTH, D_MODEL, 2 * D_FF), D_MODEL ** -0.5),
        "ffn1_down": nrm((DEPTH, D_FF, D_MODEL), D_FF ** -0.5),
        "norm_mix": 1.0 + nrm((DEPTH, D_MODEL), 0.02),
        "w_in": nrm((DEPTH, D_MODEL, IN_WIDTH), D_MODEL ** -0.5),
        "q_norm": 1.0 + nrm((DEPTH, Q_LORA_RANK), 0.02),
        "w_uq": nrm((DEPTH, Q_LORA_RANK, N_HEADS * (QK_NOPE_DIM + QK_ROPE_DIM)), Q_LORA_RANK ** -0.5),
        "kv_norm": 1.0 + nrm((DEPTH, KV_LORA_RANK), 0.02),
        "w_uk": nrm((DEPTH, KV_LORA_RANK, N_HEADS, QK_NOPE_DIM), KV_LORA_RANK ** -0.5),
        "w_uv": nrm((DEPTH, KV_LORA_RANK, N_HEADS, V_HEAD_DIM), KV_LORA_RANK ** -0.5),
        "ssm_a_re": -0.5 + nrm((DEPTH, N_GROUPS, SSM_STATE), 0.01),
        "ssm_a_im": a_im0 + nrm((DEPTH, N_GROUPS, SSM_STATE), 0.01),
        "ssm_log_dt": jax.random.uniform(next(ks), (DEPTH, N_GROUPS), f32, math.log(DT_MIN), math.log(DT_MAX)),
        "ssm_b_re": nrm((DEPTH, N_GROUPS, SSM_STATE, SSM_GROUP), (2.0 * SSM_GROUP) ** -0.5),
        "ssm_b_im": nrm((DEPTH, N_GROUPS, SSM_STATE, SSM_GROUP), (2.0 * SSM_GROUP) ** -0.5),
        "ssm_c_re": nrm((DEPTH, N_GROUPS, SSM_GROUP, SSM_STATE), (2.0 * SSM_STATE) ** -0.5),
        "ssm_c_im": nrm((DEPTH, N_GROUPS, SSM_GROUP, SSM_STATE), (2.0 * SSM_STATE) ** -0.5),
        "ssm_d": nrm((DEPTH, D_SSM), 1.0),
        "w_glu": nrm((DEPTH, D_SSM, 2 * D_MODEL), D_SSM ** -0.5),
        "w_out": nrm((DEPTH, D_MODEL, D_MODEL), D_MODEL ** -0.5),
        "norm_ffn2": 1.0 + nrm((DEPTH, D_MODEL), 0.02),
        "ffn2_up": nrm((DEPTH, D_MODEL, 2 * D_FF), D_MODEL ** -0.5),
        "ffn2_down": nrm((DEPTH, D_FF, D_MODEL), D_FF ** -0.5),
        "final_norm": 1.0 + nrm((D_MODEL,), 0.02),
    }


def reference(x_prompt, x_sample, c_prompt, c_sample, cache_ckv, cache_kr, state_ssm_re, state_ssm_im,
              page_table, ada_w, ada_b, norm_ffn1, ffn1_up, ffn1_down, norm_mix, w_in, q_norm, w_uq,
              kv_norm, w_uk, w_uv, ssm_a_re, ssm_a_im, ssm_log_dt, ssm_b_re, ssm_b_im, ssm_c_re, ssm_c_im,
              ssm_d, w_glu, w_out, norm_ffn2, ffn2_up, ffn2_down, final_norm):
    dec_batch, n_pages = page_table.shape
    past_len = n_pages * cache_ckv.shape[2]
    pos_p = jnp.arange(x_prompt.shape[1])
    pos_s = past_len + jnp.arange(x_sample.shape[1])
    xp, xs = x_prompt, x_sample
    ckv_p, kr_p, hre_p, him_p = [], [], [], []
    ckv_s, kr_s, hre_s, him_s = [], [], [], []
    for l in range(DEPTH):
        lp = (ada_w[l], ada_b[l], norm_ffn1[l], ffn1_up[l], ffn1_down[l], norm_mix[l], w_in[l],
              q_norm[l], w_uq[l], kv_norm[l], w_uk[l], w_uv[l],
              ssm_a_re[l], ssm_a_im[l], ssm_log_dt[l], ssm_b_re[l], ssm_b_im[l], ssm_c_re[l], ssm_c_im[l],
              ssm_d[l], w_glu[l], w_out[l], norm_ffn2[l], ffn2_up[l], ffn2_down[l])
        xp, a1, a2, a3, a4 = layer(xp, c_prompt, pos_p, None, None, None, None, *lp)
        ckv_p.append(a1); kr_p.append(a2); hre_p.append(a3); him_p.append(a4)
        past_ckv = cache_ckv[l, page_table].reshape(dec_batch, past_len, KV_LORA_RANK)
        past_kr = cache_kr[l, page_table].reshape(dec_batch, past_len, QK_ROPE_DIM)
        xs, b1, b2, b3, b4 = layer(xs, c_sample, pos_s, past_ckv, past_kr,
                                   state_ssm_re[l], state_ssm_im[l], *lp)
        ckv_s.append(b1); kr_s.append(b2); hre_s.append(b3); him_s.append(b4)
    y_prompt = rms_norm(xp, final_norm)
    y_sample = rms_norm(xs, final_norm)
    return (y_prompt, y_sample,
            jnp.stack(ckv_p), jnp.stack(kr_p), jnp.stack(hre_p), jnp.stack(him_p),
            jnp.stack(ckv_s), jnp.stack(kr_s), jnp.stack(hre_s), jnp.stack(him_s))
```

```python
import functools
import math

import jax
import jax.numpy as jnp
from jax import lax
from jax.experimental import pallas as pl
from jax.experimental.pallas import tpu as pltpu

F32 = jnp.float32
BF16 = jnp.bfloat16

NORM_EPS = 1e-6
ROPE_BASE = 10000.0
SSM_GROUP_WIDTH = 16
N_MODULATIONS = 9
LANES = 128
VMEM_LIMIT_BYTES = 56 * 1024 * 1024


def _cparams(*sem):
    return pltpu.CompilerParams(dimension_semantics=sem, vmem_limit_bytes=VMEM_LIMIT_BYTES)


def _sigmoid(x):
    return 1.0 / (1.0 + jnp.exp(-x))


def _rms(x, w):
    return x * lax.rsqrt(jnp.mean(x * x, axis=-1, keepdims=True) + NORM_EPS) * w


def _resident(shape):
    zeros = (0,) * len(shape)
    return pl.BlockSpec(shape, lambda *_: zeros, pipeline_mode=pl.Buffered(1))


def _adaln_kernel(c_ref, w_ref, b_ref, o_ref):
    c = c_ref[...]
    a = (c * _sigmoid(c)).astype(BF16)
    o_ref[...] = jnp.dot(a, w_ref[...].astype(BF16), preferred_element_type=F32) + b_ref[...]


def _adaln(c_all, ada_w, ada_b, tn=1536):
    depth, d, n = ada_w.shape
    rows = c_all.shape[0]
    return pl.pallas_call(
        _adaln_kernel,
        out_shape=jax.ShapeDtypeStruct((depth, rows, n), F32),
        grid=(depth, n // tn),
        in_specs=[pl.BlockSpec((rows, d), lambda l, j: (0, 0)),
                  pl.BlockSpec((None, d, tn), lambda l, j: (l, 0, j)),
                  pl.BlockSpec((None, 1, tn), lambda l, j: (l, 0, j))],
        out_specs=pl.BlockSpec((None, rows, tn), lambda l, j: (l, 0, j)),
        compiler_params=_cparams("arbitrary", "arbitrary"),
        name="adaln",
    )(c_all, ada_w, ada_b.reshape(depth, 1, n))


class _Mod:
    def __init__(self, arr, per_token, rows_per_batch):
        self.arr = arr
        self.per_token = per_token
        self.rows_per_batch = rows_per_batch

    def spec(self, k, tm, d):
        if self.per_token:
            return pl.BlockSpec((None, tm, d), lambda i: (k, i, 0))
        tiles_per_batch = self.rows_per_batch // tm
        return pl.BlockSpec((None, None, 1, d), lambda i: (i // tiles_per_batch, k, 0, 0))


def _ffn_kernel(x_ref, sh_ref, sc_ref, g_ref, nw_ref, wup_ref, wdn_ref, *rest, n_chunks, tf, final):
    if final:
        fn_ref, o_ref, h_scr, acc_scr = rest
    else:
        o_ref, h_scr, acc_scr = rest
    x = x_ref[...]
    h_scr[...] = (_rms(x, nw_ref[...]) * (1.0 + sc_ref[...]) + sh_ref[...]).astype(BF16)
    acc_scr[...] = jnp.zeros_like(acc_scr)

    def chunk(c, carry):
        ab = jnp.dot(h_scr[...], wup_ref[c], preferred_element_type=F32)
        a, b = ab[:, :tf], ab[:, tf:]
        act = (a * _sigmoid(a) * b).astype(BF16)
        acc_scr[...] += jnp.dot(act, wdn_ref[c], preferred_element_type=F32)
        return carry

    lax.fori_loop(0, n_chunks, chunk, 0)
    y = x + (0.5 * g_ref[...]) * acc_scr[...]
    if final:
        y = _rms(y, fn_ref[...])
    o_ref[...] = y


def _ffn(x, mod, k0, norm_w, wup_c, wdn_c, final_w=None, tm=512):
    m, d = x.shape
    tm = min(tm, m)
    n_chunks, _, tf2 = wup_c.shape
    tf = tf2 // 2
    final = final_w is not None
    in_specs = [pl.BlockSpec((tm, d), lambda i: (i, 0)),
                mod.spec(k0, tm, d), mod.spec(k0 + 1, tm, d), mod.spec(k0 + 2, tm, d),
                _resident((1, d)), _resident(wup_c.shape), _resident(wdn_c.shape)]
    args = [x, mod.arr, mod.arr, mod.arr, norm_w.reshape(1, d), wup_c, wdn_c]
    if final:
        in_specs.append(_resident((1, d)))
        args.append(final_w.reshape(1, d))
    return pl.pallas_call(
        functools.partial(_ffn_kernel, n_chunks=n_chunks, tf=tf, final=final),
        out_shape=jax.ShapeDtypeStruct((m, d), F32),
        grid=(m // tm,),
        in_specs=in_specs,
        out_specs=pl.BlockSpec((tm, d), lambda i: (i, 0)),
        scratch_shapes=[pltpu.VMEM((tm, d), BF16), pltpu.VMEM((tm, d), F32)],
        compiler_params=_cparams("arbitrary"),
        name="ffn",
    )(*args)


def _chunk_ffn_weights(w_up, w_down, tf=256):
    d, f2 = w_up.shape
    f = f2 // 2
    n_chunks = f // tf
    a = w_up[:, :f].reshape(d, n_chunks, tf).transpose(1, 0, 2)
    b = w_up[:, f:].reshape(d, n_chunks, tf).transpose(1, 0, 2)
    wup_c = jnp.concatenate([a, b], axis=-1).astype(BF16)
    wdn_c = w_down.reshape(n_chunks, tf, d).astype(BF16)
    return wup_c, wdn_c


def _mixer_in_kernel(x_ref, sh_ref, sc_ref, nw_ref, cs_ref, wa_ref, wu_ref, qn_ref, kvn_ref, wuq_ref, wuk_ref,
                     q_ref, kc_ref, ckv_ref, kr_ref, u_ref, sga_ref, sgb_ref,
                     *, n_heads, q_rank, kv_rank, rope_dim, nope_dim, d_ssm, scale):
    x = x_ref[...]
    h = (_rms(x, nw_ref[...]) * (1.0 + sc_ref[...]) + sh_ref[...]).astype(BF16)
    cos = cs_ref[:, :LANES]
    sin = cs_ref[:, LANES:]

    t = jnp.dot(h, wa_ref[...], preferred_element_type=F32)
    cq = t[:, :q_rank]
    kr = t[:, q_rank:q_rank + LANES] * cos + t[:, q_rank + LANES:q_rank + 2 * LANES] * sin
    ckv = _rms(t[:, q_rank + 2 * LANES:], kvn_ref[...])
    kr_ref[...] = kr[:, :rope_dim]
    ckv_ref[...] = ckv
    kc_ref[:, :kv_rank] = ckv.astype(BF16)
    kc_ref[:, kv_rank:] = kr.astype(BF16)

    ug = jnp.dot(h, wu_ref[...], preferred_element_type=F32)
    u_ref[...] = ug[:, :d_ssm]
    d_model = sga_ref.shape[-1]
    sga_ref[...] = _sigmoid(ug[:, d_ssm:d_ssm + d_model])
    sgb_ref[...] = _sigmoid(ug[:, d_ssm + d_model:])

    cqn = _rms(cq, qn_ref[...]).astype(BF16)
    q = jnp.dot(cqn, wuq_ref[...], preferred_element_type=F32)
    base_a = n_heads * nope_dim
    base_b = base_a + n_heads * LANES
    for hd in range(n_heads):
        nope = q[:, hd * nope_dim:(hd + 1) * nope_dim].astype(BF16)
        q_lat = jnp.dot(nope, wuk_ref[hd], preferred_element_type=F32) * scale
        q_rot = (q[:, base_a + hd * LANES:base_a + (hd + 1) * LANES] * cos
                 + q[:, base_b + hd * LANES:base_b + (hd + 1) * LANES] * sin) * scale
        q_ref[hd, :, :kv_rank] = q_lat.astype(BF16)
        q_ref[hd, :, kv_rank:] = q_rot.astype(BF16)


def _mixer_in(x, mod, norm_w, cs_table, w, dims, n_batch, tm=256):
    m, d = x.shape
    tm = min(tm, m // n_batch)
    rows = m // n_batch
    tiles = rows // tm
    cs_tiles = cs_table.shape[0] // tm
    nh, q_rank, kv_rank, rope_dim, nope_dim, d_ssm = (dims[k] for k in
                                                       ("n_heads", "q_rank", "kv_rank", "rope_dim", "nope_dim", "d_ssm"))
    kw = kv_rank + LANES
    kern = functools.partial(_mixer_in_kernel, n_heads=nh, q_rank=q_rank, kv_rank=kv_rank, rope_dim=rope_dim,
                             nope_dim=nope_dim, d_ssm=d_ssm, scale=(nope_dim + rope_dim) ** -0.5)
    row = lambda width: pl.BlockSpec((tm, width), lambda i: (i, 0))
    out_shape = [jax.ShapeDtypeStruct((n_batch, nh, rows, kw), BF16),
                 jax.ShapeDtypeStruct((m, kw), BF16),
                 jax.ShapeDtypeStruct((m, kv_rank), F32),
                 jax.ShapeDtypeStruct((m, rope_dim), F32),
                 jax.ShapeDtypeStruct((m, d_ssm), F32),
                 jax.ShapeDtypeStruct((m, d), F32),
                 jax.ShapeDtypeStruct((m, d), F32)]
    out_specs = [pl.BlockSpec((None, nh, tm, kw), lambda i: (i // tiles, 0, i % tiles, 0)),
                 row(kw), row(kv_rank), row(rope_dim), row(d_ssm), row(d), row(d)]
    return pl.pallas_call(
        kern,
        out_shape=out_shape,
        grid=(m // tm,),
        in_specs=[row(d), mod.spec(3, tm, d), mod.spec(4, tm, d), _resident((1, d)),
                  pl.BlockSpec((tm, 2 * LANES), lambda i: (i % cs_tiles, 0)),
                  _resident(w["wa"].shape), _resident(w["wu"].shape), _resident((1, q_rank)),
                  _resident((1, kv_rank)), _resident(w["wuq"].shape), _resident(w["wuk"].shape)],
        out_specs=out_specs,
        compiler_params=_cparams("arbitrary"),
        name="mixer_in",
    )(x, mod.arr, mod.arr, norm_w.reshape(1, d), cs_table, w["wa"], w["wu"], w["qn"], w["kvn"], w["wuq"], w["wuk"])


def _rope_tables(pos, rope_dim):
    half = rope_dim // 2
    inv = ROPE_BASE ** (-jnp.arange(half, dtype=F32) / half)
    ang = pos.astype(F32)[:, None] * inv[None, :]
    cos, sin = jnp.cos(ang), jnp.sin(ang)
    pad = jnp.zeros((pos.shape[0], LANES - rope_dim), F32)
    return jnp.concatenate([cos, cos, pad, -sin, sin, pad], axis=-1)


def _swap_halves(w, rope_dim):
    half = rope_dim // 2
    return jnp.concatenate([w[..., half:], w[..., :half]], axis=-1)


def _pad_lanes(w):
    return jnp.pad(w, [(0, 0)] * (w.ndim - 1) + [(0, LANES - w.shape[-1])])


def _mixer_in_weights(w_in, q_norm, w_uq, kv_norm, w_uk, dims):
    nh, q_rank, kv_rank, rope_dim, nope_dim, d_ssm = (dims[k] for k in
                                                       ("n_heads", "q_rank", "kv_rank", "rope_dim", "nope_dim", "d_ssm"))
    o1, o2 = q_rank + kv_rank, q_rank + kv_rank + rope_dim
    w_cq, w_ckv, w_kr, w_rest = w_in[:, :q_rank], w_in[:, q_rank:o1], w_in[:, o1:o2], w_in[:, o2:]
    wa = jnp.concatenate([w_cq, _pad_lanes(w_kr), _pad_lanes(_swap_halves(w_kr, rope_dim)), w_ckv], axis=-1)
    wq = w_uq.reshape(q_rank, nh, nope_dim + rope_dim)
    wq_nope = wq[..., :nope_dim].reshape(q_rank, nh * nope_dim)
    wq_rope = wq[..., nope_dim:]
    wq_a = _pad_lanes(wq_rope).reshape(q_rank, nh * LANES)
    wq_b = _pad_lanes(_swap_halves(wq_rope, rope_dim)).reshape(q_rank, nh * LANES)
    return {"wa": wa.astype(BF16), "wu": w_rest.astype(BF16),
            "qn": q_norm.reshape(1, q_rank), "kvn": kv_norm.reshape(1, kv_rank),
            "wuq": jnp.concatenate([wq_nope, wq_a, wq_b], axis=-1).astype(BF16),
            "wuk": w_uk.transpose(1, 2, 0).astype(BF16)}


def _softmax_step(s, v, m_scr, l_scr, acc_scr):
    m_prev = m_scr[...]
    m_new = jnp.maximum(m_prev, jnp.max(s, axis=1, keepdims=True))
    alpha = jnp.exp(m_prev - m_new)
    p = jnp.exp(s - m_new)
    l_scr[...] = alpha * l_scr[...] + jnp.sum(p, axis=1, keepdims=True)
    acc_scr[...] = alpha * acc_scr[...] + jnp.dot(p.astype(BF16), v, preferred_element_type=F32)
    m_scr[...] = m_new


_NT = (((1,), (1,)), ((), ()))


def _attn_prompt_kernel(q_ref, k_ref, o_ref, m_scr, l_scr, acc_scr, *, tq, n_heads, kv_rank):
    qi = pl.program_id(1)
    rows = n_heads * tq
    q = q_ref[...].reshape(rows, q_ref.shape[-1])
    m_scr[...] = jnp.full_like(m_scr, -jnp.inf)
    l_scr[...] = jnp.zeros_like(l_scr)
    acc_scr[...] = jnp.zeros_like(acc_scr)

    def step(j, masked):
        k = k_ref[pl.ds(pl.multiple_of(j * tq, tq), tq), :]
        s = lax.dot_general(q, k, _NT, preferred_element_type=F32)
        if masked:
            q_pos = jnp.bitwise_and(lax.broadcasted_iota(jnp.int32, (rows, tq), 0), tq - 1)
            k_pos = lax.broadcasted_iota(jnp.int32, (rows, tq), 1)
            s = jnp.where(k_pos <= q_pos, s, -jnp.inf)
        _softmax_step(s, k[:, :kv_rank], m_scr, l_scr, acc_scr)

    def body(j, carry):
        step(j, False)
        return carry

    lax.fori_loop(0, qi, body, 0)
    step(qi, True)
    o = acc_scr[...] / l_scr[...]
    o_ref[...] = o.astype(BF16).reshape(n_heads, tq, kv_rank)


def _attn_prompt(q, kc, kv_rank, tq=256):
    b, nh, t, kw = q.shape
    tq = min(tq, t)
    assert tq & (tq - 1) == 0 and t % tq == 0
    return pl.pallas_call(
        functools.partial(_attn_prompt_kernel, tq=tq, n_heads=nh, kv_rank=kv_rank),
        out_shape=jax.ShapeDtypeStruct((b, nh, t, kv_rank), BF16),
        grid=(b, t // tq),
        in_specs=[pl.BlockSpec((None, nh, tq, kw), lambda bi, qi: (bi, 0, qi, 0)),
                  pl.BlockSpec((None, t, kw), lambda bi, qi: (bi, 0, 0))],
        out_specs=pl.BlockSpec((None, nh, tq, kv_rank), lambda bi, qi: (bi, 0, qi, 0)),
        scratch_shapes=[pltpu.VMEM((nh * tq, 1), F32), pltpu.VMEM((nh * tq, 1), F32),
                        pltpu.VMEM((nh * tq, kv_rank), F32)],
        compiler_params=_cparams("arbitrary", "arbitrary"),
        name="attn_prompt",
    )(q, kc)


def _attn_sample_kernel(pt_ref, q_ref, knew_ref, *rest, pages_per_step, page, kv_rank, rope_dim, new_len, n_steps):
    ckv_refs = rest[:pages_per_step]
    kr_refs = rest[pages_per_step:2 * pages_per_step]
    o_ref, kst, m_scr, l_scr, acc_scr = rest[2 * pages_per_step:]
    b, j = pl.program_id(0), pl.program_id(1)

    @pl.when((b == 0) & (j == 0))
    def _():
        kst[:, kv_rank + rope_dim:] = jnp.zeros((kst.shape[0], kst.shape[1] - kv_rank - rope_dim), BF16)

    @pl.when(j == 0)
    def _():
        m_scr[...] = jnp.full_like(m_scr, -jnp.inf)
        l_scr[...] = jnp.zeros_like(l_scr)
        acc_scr[...] = jnp.zeros_like(acc_scr)

    for p in range(pages_per_step):
        kst[p * page:(p + 1) * page, :kv_rank] = ckv_refs[p][...].astype(BF16)
        kst[p * page:(p + 1) * page, kv_rank:kv_rank + rope_dim] = kr_refs[p][...].astype(BF16)
    q = q_ref[...]
    k = kst[...]
    s = lax.dot_general(q, k, _NT, preferred_element_type=F32)
    _softmax_step(s, k[:, :kv_rank], m_scr, l_scr, acc_scr)

    @pl.when(j == n_steps - 1)
    def _():
        rows = q.shape[0]
        kn = knew_ref[...]
        npad = kn.shape[0]
        sn = lax.dot_general(q, kn, _NT, preferred_element_type=F32)
        q_pos = jnp.bitwise_and(lax.broadcasted_iota(jnp.int32, (rows, npad), 0), new_len - 1)
        k_pos = lax.broadcasted_iota(jnp.int32, (rows, npad), 1)
        sn = jnp.where(k_pos <= q_pos, sn, -jnp.inf)
        _softmax_step(sn, kn[:, :kv_rank], m_scr, l_scr, acc_scr)
        o_ref[...] = (acc_scr[...] / l_scr[...]).astype(BF16)


def _attn_sample(q, k_new, cache_ckv, cache_kr, page_table, layer, kv_rank, rope_dim, new_len, pages_per_step=8):
    bs, rows, kw = q.shape
    n_pages = page_table.shape[1]
    page = cache_ckv.shape[2]
    pages_per_step = min(pages_per_step, n_pages)
    assert n_pages % pages_per_step == 0 and new_len & (new_len - 1) == 0
    n_steps = n_pages // pages_per_step
    new_pad = k_new.shape[1]

    def page_spec(width, p):
        return pl.BlockSpec((None, None, page, width),
                            lambda b, j, pt: (layer, pt[b, j * pages_per_step + p], 0, 0))

    in_specs = ([pl.BlockSpec((None, rows, kw), lambda b, j, pt: (b, 0, 0)),
                 pl.BlockSpec((None, new_pad, kw), lambda b, j, pt: (b, 0, 0))]
                + [page_spec(kv_rank, p) for p in range(pages_per_step)]
                + [page_spec(rope_dim, p) for p in range(pages_per_step)])
    kern = functools.partial(_attn_sample_kernel, pages_per_step=pages_per_step, page=page, kv_rank=kv_rank,
                             rope_dim=rope_dim, new_len=new_len, n_steps=n_steps)
    return pl.pallas_call(
        kern,
        out_shape=jax.ShapeDtypeStruct((bs, rows, kv_rank), BF16),
        grid_spec=pltpu.PrefetchScalarGridSpec(
            num_scalar_prefetch=1,
            grid=(bs, n_steps),
            in_specs=in_specs,
            out_specs=pl.BlockSpec((None, rows, kv_rank), lambda b, j, pt: (b, 0, 0)),
            scratch_shapes=[pltpu.VMEM((pages_per_step * page, kw), BF16),
                            pltpu.VMEM((rows, 1), F32), pltpu.VMEM((rows, 1), F32),
                            pltpu.VMEM((rows, kv_rank), F32)]),
        compiler_params=_cparams("arbitrary", "arbitrary"),
        name="attn_sample",
    )(page_table, q, k_new, *([cache_ckv] * pages_per_step), *([cache_kr] * pages_per_step))


def _ssm_kernel(u_ref, kmat_ref, bre_ref, bim_ref, cre_ref, cim_ref, lre_ref, lim_ref, dsk_ref, h0re_ref, h0im_ref,
                z_ref, hre_ref, him_ref, sre_scr, sim_scr, *, n_chunks, batch, groups):
    for g in range(groups):
        ub = u_ref[g].astype(BF16)
        sre_scr[g] = jnp.dot(ub, bre_ref[g], preferred_element_type=F32)
        sim_scr[g] = jnp.dot(ub, bim_ref[g], preferred_element_type=F32)
    lre, lim = lre_ref[...], lim_ref[...]

    def scan(n, carry):
        h_re, h_im = carry
        rows = pl.ds(pl.multiple_of(n * batch, batch), batch)
        s_re, s_im = sre_scr[:, rows, :], sim_scr[:, rows, :]
        sre_scr[:, rows, :] = h_re
        sim_scr[:, rows, :] = h_im
        return lre * h_re - lim * h_im + s_re, lre * h_im + lim * h_re + s_im

    h_re, h_im = lax.fori_loop(0, n_chunks, scan, (h0re_ref[...], h0im_ref[...]))
    hre_ref[...] = h_re
    him_ref[...] = h_im
    for g in range(groups):
        u = u_ref[g]
        y = (jnp.dot(u.astype(BF16), kmat_ref[g], preferred_element_type=F32)
             + jnp.dot(sre_scr[g].astype(BF16), cre_ref[g], preferred_element_type=F32)
             + jnp.dot(sim_scr[g].astype(BF16), cim_ref[g], preferred_element_type=F32)
             + dsk_ref[g] * u)
        gelu = 0.5 * y * (1.0 + jnp.tanh(math.sqrt(2.0 / math.pi) * (y + 0.044715 * (y * y * y))))
        z_ref[g] = gelu.astype(BF16)


def _ssm(u4, p, h0_re, h0_im, n_chunks, batch, groups=4):
    g_all, rows, width = u4.shape
    n_state = p["bre"].shape[-1]
    blk = lambda a, b: pl.BlockSpec((groups, a, b), lambda i: (i, 0, 0))
    return pl.pallas_call(
        functools.partial(_ssm_kernel, n_chunks=n_chunks, batch=batch, groups=groups),
        out_shape=[jax.ShapeDtypeStruct((g_all, rows, width), BF16),
                   jax.ShapeDtypeStruct((g_all, batch, n_state), F32),
                   jax.ShapeDtypeStruct((g_all, batch, n_state), F32)],
        grid=(g_all // groups,),
        in_specs=[blk(rows, width), blk(width, width), blk(width, n_state), blk(width, n_state),
                  blk(n_state, width), blk(n_state, width), blk(1, n_state), blk(1, n_state), blk(1, width),
                  blk(batch, n_state), blk(batch, n_state)],
        out_specs=[blk(rows, width), blk(batch, n_state), blk(batch, n_state)],
        scratch_shapes=[pltpu.VMEM((groups, rows, n_state), F32), pltpu.VMEM((groups, rows, n_state), F32)],
        compiler_params=_cparams("arbitrary"),
        name="ssm",
    )(u4, p["kmat"], p["bre"], p["bim"], p["cre"], p["cim"], p["lre"], p["lim"], p["dsk"], h0_re, h0_im)


def _ssm_params(a_re, a_im, log_dt, b_re, b_im, c_re, c_im, d_skip, chunk):
    hp = lax.Precision.HIGHEST
    g, n_state = a_re.shape
    cw = SSM_GROUP_WIDTH
    lam = lax.complex(a_re, a_im)
    lam_dt = lam * jnp.exp(log_dt)[:, None]
    lam_bar = jnp.exp(lam_dt)
    b_bar = ((lam_bar - 1.0) / lam)[..., None] * lax.complex(b_re, b_im)
    cc = lax.complex(c_re, c_im)
    steps = jnp.arange(chunk + 1, dtype=F32)
    pw = jnp.exp(lam_dt[None] * steps[:, None, None])
    lag = jnp.einsum("gcp,kgp,gpd->gkdc", cc, pw[:chunk], b_bar, precision=hp).real
    t_idx = jnp.arange(chunk)
    diff = t_idx[None, :] - t_idx[:, None]
    kmat = jnp.where((diff >= 0)[None, :, None, :, None],
                     lag[:, jnp.clip(diff, 0, chunk - 1)].transpose(0, 1, 3, 2, 4), 0.0)
    kmat = kmat.reshape(g, chunk * cw, chunk * cw)
    bend = jnp.einsum("sgp,gpd->gsdp", pw[chunk - 1 - t_idx], b_bar, precision=hp).reshape(g, chunk * cw, n_state)
    cpow = jnp.einsum("gcp,tgp->gptc", cc, pw[1:], precision=hp).reshape(g, n_state, chunk * cw)
    lam_l = pw[chunk]
    dsk = jnp.tile(d_skip.reshape(g, 1, cw), (1, chunk, 1)).reshape(g, 1, chunk * cw)
    return {"kmat": kmat.astype(BF16), "bre": bend.real.astype(BF16), "bim": bend.imag.astype(BF16),
            "cre": cpow.real.astype(BF16), "cim": (-cpow.imag).astype(BF16),
            "lre": lam_l.real.reshape(g, 1, n_state), "lim": lam_l.imag.reshape(g, 1, n_state), "dsk": dsk}


def _to_chunks(u, n_batch, chunk):
    m, d = u.shape
    g = d // SSM_GROUP_WIDTH
    n = m // n_batch // chunk
    u5 = u.reshape(n_batch, n, chunk, g, SSM_GROUP_WIDTH)
    return u5.transpose(3, 1, 0, 2, 4).reshape(g, n * n_batch, chunk * SSM_GROUP_WIDTH)


def _from_chunks(z4, n_batch, chunk):
    g, rows, _ = z4.shape
    n = rows // n_batch
    z5 = z4.reshape(g, n, n_batch, chunk, SSM_GROUP_WIDTH)
    return z5.transpose(2, 1, 3, 0, 4).reshape(n_batch * n * chunk, g * SSM_GROUP_WIDTH)


def _mixer_out_kernel(x_ref, gm_ref, z_ref, o_ref, sga_ref, sgb_ref, wglu_ref, wuv_ref, wout_ref, y_ref, mixed_scr,
                      *, n_heads, v_dim):
    d = x_ref.shape[-1]
    glu = jnp.dot(z_ref[...], wglu_ref[...], preferred_element_type=F32)
    y_ssm = glu[:, :d] * _sigmoid(glu[:, d:])
    mixed_scr[...] = (sgb_ref[...] * y_ssm).astype(mixed_scr.dtype)
    for hd in range(n_heads):
        cols = slice(hd * v_dim, (hd + 1) * v_dim)
        y_attn = jnp.dot(o_ref[hd], wuv_ref[hd], preferred_element_type=F32)
        mixed_scr[:, cols] = mixed_scr[:, cols] + sga_ref[:, cols] * y_attn
    proj = jnp.dot(mixed_scr[...].astype(BF16), wout_ref[...], preferred_element_type=F32)
    y_ref[...] = x_ref[...] + gm_ref[...] * proj


def _mixer_out(x, mod, z, o_lat, sga, sgb, w, n_batch, tm=512):
    m, d = x.shape
    rows = m // n_batch
    tm = min(tm, rows)
    tiles = rows // tm
    _, nh, _, kv_rank = o_lat.shape
    v_dim = w["wuv"].shape[-1]
    row = lambda width: pl.BlockSpec((tm, width), lambda i: (i, 0))
    return pl.pallas_call(
        functools.partial(_mixer_out_kernel, n_heads=nh, v_dim=v_dim),
        out_shape=jax.ShapeDtypeStruct((m, d), F32),
        grid=(m // tm,),
        in_specs=[row(d), mod.spec(5, tm, d), row(d),
                  pl.BlockSpec((None, nh, tm, kv_rank), lambda i: (i // tiles, 0, i % tiles, 0)),
                  row(d), row(d), _resident(w["wglu"].shape), _resident(w["wuv"].shape), _resident(w["wout"].shape)],
        out_specs=row(d),
        scratch_shapes=[pltpu.VMEM((tm, d), F32)],
        compiler_params=_cparams("arbitrary"),
        name="mixer_out",
    )(x, mod.arr, z, o_lat, sga, sgb, w["wglu"], w["wuv"], w["wout"])


def _layer(x, mod, n_batch, cs_table, lw, dims, attend, ssm_batch, ssm_chunk, h0_re, h0_im, final_w):
    m, d = x.shape
    x = _ffn(x, mod, 0, lw["norm_ffn1"], *lw["ffn1"])
    q, kc, ckv, kr, u, sga, sgb = _mixer_in(x, mod, lw["norm_mix"], cs_table, lw["mix_in"], dims, n_batch)
    o_lat = attend(q, kc)
    n_chunks = m // ssm_batch // ssm_chunk
    z4, h_re, h_im = _ssm(_to_chunks(u, ssm_batch, ssm_chunk), lw["ssm"][ssm_chunk], h0_re, h0_im, n_chunks, ssm_batch)
    z = _from_chunks(z4, ssm_batch, ssm_chunk)
    x = _mixer_out(x, mod, z, o_lat, sga, sgb, lw["mix_out"], n_batch)
    x = _ffn(x, mod, 6, lw["norm_ffn2"], *lw["ffn2"], final_w=final_w)
    return x, ckv, kr, h_re.transpose(1, 0, 2), h_im.transpose(1, 0, 2)


def kernel(x_prompt, x_sample, c_prompt, c_sample, cache_ckv, cache_kr, state_ssm_re, state_ssm_im, page_table, ada_w, ada_b, norm_ffn1, ffn1_up, ffn1_down, norm_mix, w_in, q_norm, w_uq, kv_norm, w_uk, w_uv, ssm_a_re, ssm_a_im, ssm_log_dt, ssm_b_re, ssm_b_im, ssm_c_re, ssm_c_im, ssm_d, w_glu, w_out, norm_ffn2, ffn2_up, ffn2_down, final_norm):
    bp, tp, d = x_prompt.shape
    bs, ts, _ = x_sample.shape
    depth = ada_w.shape[0]
    q_rank, kv_rank = q_norm.shape[-1], kv_norm.shape[-1]
    rope_dim = cache_kr.shape[-1]
    nh, nope_dim = w_uk.shape[2], w_uk.shape[3]
    n_groups, n_state = ssm_a_re.shape[1], ssm_a_re.shape[2]
    dims = {"n_heads": nh, "q_rank": q_rank, "kv_rank": kv_rank, "rope_dim": rope_dim, "nope_dim": nope_dim,
            "d_ssm": ssm_d.shape[-1]}
    past_len = page_table.shape[1] * cache_ckv.shape[2]
    prompt_chunk = 16 if tp % 16 == 0 else tp
    sample_chunk = ts

    mod_all = _adaln(jnp.concatenate([c_prompt, c_sample], axis=0), ada_w, ada_b)
    cs_p = _rope_tables(jnp.arange(tp), rope_dim)
    cs_s = jnp.tile(_rope_tables(past_len + jnp.arange(ts), rope_dim), (bs, 1))

    xp = x_prompt.reshape(bp * tp, d)
    xs = x_sample.reshape(bs * ts, d)
    zeros_state = jnp.zeros((n_groups, bp, n_state), F32)
    outs = {k: [] for k in ("ckv_p", "kr_p", "hre_p", "him_p", "ckv_s", "kr_s", "hre_s", "him_s")}
    for l in range(depth):
        ssm_args = (ssm_a_re[l], ssm_a_im[l], ssm_log_dt[l], ssm_b_re[l], ssm_b_im[l], ssm_c_re[l], ssm_c_im[l], ssm_d[l])
        lw = {"norm_ffn1": norm_ffn1[l], "norm_mix": norm_mix[l], "norm_ffn2": norm_ffn2[l],
              "ffn1": _chunk_ffn_weights(ffn1_up[l], ffn1_down[l]),
              "ffn2": _chunk_ffn_weights(ffn2_up[l], ffn2_down[l]),
              "mix_in": _mixer_in_weights(w_in[l], q_norm[l], w_uq[l], kv_norm[l], w_uk[l], dims),
              "mix_out": {"wglu": w_glu[l].astype(BF16), "wuv": w_uv[l].transpose(1, 0, 2).astype(BF16),
                          "wout": w_out[l].astype(BF16)},
              "ssm": {c: _ssm_params(*ssm_args, c) for c in {prompt_chunk, sample_chunk}}}
        last = final_norm if l == depth - 1 else None

        mod_p = _Mod(mod_all[l, :bp].reshape(bp, N_MODULATIONS, 1, d), False, tp)
        xp, ckv, kr, h_re, h_im = _layer(
            xp, mod_p, bp, cs_p, lw, dims,
            lambda q, kc: _attn_prompt(q, kc.reshape(bp, tp, kc.shape[-1]), kv_rank), bp, prompt_chunk, zeros_state, zeros_state, last)
        outs["ckv_p"].append(ckv.reshape(bp, tp, kv_rank)); outs["kr_p"].append(kr.reshape(bp, tp, rope_dim))
        outs["hre_p"].append(h_re); outs["him_p"].append(h_im)

        mod_rows = jnp.repeat(mod_all[l, bp:].reshape(bs, N_MODULATIONS, d), ts, axis=0)
        mod_s = _Mod(mod_rows.transpose(1, 0, 2), True, ts)

        def attend_sample(q, kc, l=l):
            kw = q.shape[-1]
            qb = q.reshape(nh, bs, ts, kw).transpose(1, 0, 2, 3).reshape(bs, nh * ts, kw)
            k_new = jnp.pad(kc.reshape(bs, ts, kw), ((0, 0), (0, 16 - ts), (0, 0)))
            o = _attn_sample(qb, k_new, cache_ckv, cache_kr, page_table, l, kv_rank, rope_dim, ts)
            return o.reshape(bs, nh, ts, kv_rank).transpose(1, 0, 2, 3).reshape(1, nh, bs * ts, kv_rank)

        xs, ckv, kr, h_re, h_im = _layer(
            xs, mod_s, 1, cs_s, lw, dims,
            attend_sample, bs, sample_chunk,
            state_ssm_re[l].transpose(1, 0, 2), state_ssm_im[l].transpose(1, 0, 2), last)
        outs["ckv_s"].append(ckv.reshape(bs, ts, kv_rank)); outs["kr_s"].append(kr.reshape(bs, ts, rope_dim))
        outs["hre_s"].append(h_re); outs["him_s"].append(h_im)

    st = lambda k: jnp.stack(outs[k])
    return (xp.reshape(bp, tp, d), xs.reshape(bs, ts, d),
            st("ckv_p"), st("kr_p"), st("hre_p"), st("him_p"),
            st("ckv_s"), st("kr_s"), st("hre_s"), st("him_s"))
```

```python
import functools
import math

import jax
import jax.numpy as jnp
from jax import lax
from jax.experimental import pallas as pl
from jax.experimental.pallas import tpu as pltpu

F32 = jnp.float32
BF16 = jnp.bfloat16

NORM_EPS = 1e-6
ROPE_BASE = 10000.0
SSM_GROUP_WIDTH = 16
N_MODULATIONS = 9
LANES = 128
VMEM_LIMIT_BYTES = 56 * 1024 * 1024


def _cparams(*sem):
    return pltpu.CompilerParams(dimension_semantics=sem, vmem_limit_bytes=VMEM_LIMIT_BYTES)


def _sigmoid(x):
    return 1.0 / (1.0 + jnp.exp(-x))


def _rms(x, w):
    return x * lax.rsqrt(jnp.mean(x * x, axis=-1, keepdims=True) + NORM_EPS) * w


def _resident(shape):
    zeros = (0,) * len(shape)
    return pl.BlockSpec(shape, lambda *_: zeros, pipeline_mode=pl.Buffered(1))


def _adaln_kernel(c_ref, w_ref, b_ref, o_ref):
    c = c_ref[...]
    a = (c * _sigmoid(c)).astype(BF16)
    o_ref[...] = jnp.dot(a, w_ref[...].astype(BF16), preferred_element_type=F32) + b_ref[...]


def _adaln(c_all, ada_w, ada_b, tn=1536):
    depth, d, n = ada_w.shape
    rows = c_all.shape[0]
    return pl.pallas_call(
        _adaln_kernel,
        out_shape=jax.ShapeDtypeStruct((depth, rows, n), F32),
        grid=(depth, n // tn),
        in_specs=[pl.BlockSpec((rows, d), lambda l, j: (0, 0)),
                  pl.BlockSpec((None, d, tn), lambda l, j: (l, 0, j)),
                  pl.BlockSpec((None, 1, tn), lambda l, j: (l, 0, j))],
        out_specs=pl.BlockSpec((None, rows, tn), lambda l, j: (l, 0, j)),
        compiler_params=_cparams("arbitrary", "arbitrary"),
        name="adaln",
    )(c_all, ada_w, ada_b.reshape(depth, 1, n))


class _Mod:
    def __init__(self, arr, per_token, rows_per_batch):
        self.arr = arr
        self.per_token = per_token
        self.rows_per_batch = rows_per_batch

    def spec(self, k, tm, d):
        if self.per_token:
            return pl.BlockSpec((None, tm, d), lambda i: (k, i, 0))
        tiles_per_batch = self.rows_per_batch // tm
        return pl.BlockSpec((None, None, 1, d), lambda i: (i // tiles_per_batch, k, 0, 0))


def _ffn_kernel(x_ref, sh_ref, sc_ref, g_ref, nw_ref, wup_ref, wdn_ref, *rest, n_chunks, tf, final):
    if final:
        fn_ref, o_ref, h_scr, acc_scr = rest
    else:
        o_ref, h_scr, acc_scr = rest
    x = x_ref[...]
    h_scr[...] = (_rms(x, nw_ref[...]) * (1.0 + sc_ref[...]) + sh_ref[...]).astype(BF16)
    acc_scr[...] = jnp.zeros_like(acc_scr)

    def chunk(c, carry):
        ab = jnp.dot(h_scr[...], wup_ref[c], preferred_element_type=F32)
        a, b = ab[:, :tf], ab[:, tf:]
        act = (a * _sigmoid(a) * b).astype(BF16)
        acc_scr[...] += jnp.dot(act, wdn_ref[c], preferred_element_type=F32)
        return carry

    lax.fori_loop(0, n_chunks, chunk, 0)
    y = x + (0.5 * g_ref[...]) * acc_scr[...]
    if final:
        y = _rms(y, fn_ref[...])
    o_ref[...] = y


def _ffn(x, mod, k0, norm_w, wup_c, wdn_c, final_w=None, tm=512):
    m, d = x.shape
    tm = min(tm, m)
    n_chunks, _, tf2 = wup_c.shape
    tf = tf2 // 2
    final = final_w is not None
    in_specs = [pl.BlockSpec((tm, d), lambda i: (i, 0)),
                mod.spec(k0, tm, d), mod.spec(k0 + 1, tm, d), mod.spec(k0 + 2, tm, d),
                _resident((1, d)), _resident(wup_c.shape), _resident(wdn_c.shape)]
    args = [x, mod.arr, mod.arr, mod.arr, norm_w.reshape(1, d), wup_c, wdn_c]
    if final:
        in_specs.append(_resident((1, d)))
        args.append(final_w.reshape(1, d))
    return pl.pallas_call(
        functools.partial(_ffn_kernel, n_chunks=n_chunks, tf=tf, final=final),
        out_shape=jax.ShapeDtypeStruct((m, d), F32),
        grid=(m // tm,),
        in_specs=in_specs,
        out_specs=pl.BlockSpec((tm, d), lambda i: (i, 0)),
        scratch_shapes=[pltpu.VMEM((tm, d), BF16), pltpu.VMEM((tm, d), F32)],
        compiler_params=_cparams("arbitrary"),
        name="ffn",
    )(*args)


def _chunk_ffn_weights(w_up, w_down, tf=256):
    d, f2 = w_up.shape
    f = f2 // 2
    n_chunks = f // tf
    a = w_up[:, :f].reshape(d, n_chunks, tf).transpose(1, 0, 2)
    b = w_up[:, f:].reshape(d, n_chunks, tf).transpose(1, 0, 2)
    wup_c = jnp.concatenate([a, b], axis=-1).astype(BF16)
    wdn_c = w_down.reshape(n_chunks, tf, d).astype(BF16)
    return wup_c, wdn_c


def _mixer_in_kernel(x_ref, sh_ref, sc_ref, nw_ref, cs_ref, wa_ref, wu_ref, qn_ref, kvn_ref, wuq_ref, wuk_ref,
                     q_ref, kc_ref, ckv_ref, kr_ref, u_ref, sga_ref, sgb_ref,
                     *, n_heads, q_rank, kv_rank, rope_dim, nope_dim, d_ssm, scale):
    x = x_ref[...]
    h = (_rms(x, nw_ref[...]) * (1.0 + sc_ref[...]) + sh_ref[...]).astype(BF16)
    cos = cs_ref[:, :LANES]
    sin = cs_ref[:, LANES:]

    t = jnp.dot(h, wa_ref[...], preferred_element_type=F32)
    cq = t[:, :q_rank]
    kr = t[:, q_rank:q_rank + LANES] * cos + t[:, q_rank + LANES:q_rank + 2 * LANES] * sin
    ckv = _rms(t[:, q_rank + 2 * LANES:], kvn_ref[...])
    kr_ref[...] = kr[:, :rope_dim]
    ckv_ref[...] = ckv
    kc_ref[:, :kv_rank] = ckv.astype(BF16)
    kc_ref[:, kv_rank:] = kr.astype(BF16)

    ug = jnp.dot(h, wu_ref[...], preferred_element_type=F32)
    u_ref[...] = ug[:, :d_ssm]
    d_model = sga_ref.shape[-1]
    sga_ref[...] = _sigmoid(ug[:, d_ssm:d_ssm + d_model])
    sgb_ref[...] = _sigmoid(ug[:, d_ssm + d_model:])

    cqn = _rms(cq, qn_ref[...]).astype(BF16)
    q = jnp.dot(cqn, wuq_ref[...], preferred_element_type=F32)
    base_a = n_heads * nope_dim
    base_b = base_a + n_heads * LANES
    for hd in range(n_heads):
        nope = q[:, hd * nope_dim:(hd + 1) * nope_dim].astype(BF16)
        q_lat = jnp.dot(nope, wuk_ref[hd], preferred_element_type=F32) * scale
        q_rot = (q[:, base_a + hd * LANES:base_a + (hd + 1) * LANES] * cos
                 + q[:, base_b + hd * LANES:base_b + (hd + 1) * LANES] * sin) * scale
        q_ref[hd, :, :kv_rank] = q_lat.astype(BF16)
        q_ref[hd, :, kv_rank:] = q_rot.astype(BF16)


def _mixer_in(x, mod, norm_w, cs_table, w, dims, n_batch, tm=256):
    m, d = x.shape
    tm = min(tm, m // n_batch)
    rows = m // n_batch
    tiles = rows // tm
    cs_tiles = cs_table.shape[0] // tm
    nh, q_rank, kv_rank, rope_dim, nope_dim, d_ssm = (dims[k] for k in
                                                       ("n_heads", "q_rank", "kv_rank", "rope_dim", "nope_dim", "d_ssm"))
    kw = kv_rank + LANES
    kern = functools.partial(_mixer_in_kernel, n_heads=nh, q_rank=q_rank, kv_rank=kv_rank, rope_dim=rope_dim,
                             nope_dim=nope_dim, d_ssm=d_ssm, scale=(nope_dim + rope_dim) ** -0.5)
    row = lambda width: pl.BlockSpec((tm, width), lambda i: (i, 0))
    out_shape = [jax.ShapeDtypeStruct((n_batch, nh, rows, kw), BF16),
                 jax.ShapeDtypeStruct((m, kw), BF16),
                 jax.ShapeDtypeStruct((m, kv_rank), F32),
                 jax.ShapeDtypeStruct((m, rope_dim), F32),
                 jax.ShapeDtypeStruct((m, d_ssm), F32),
                 jax.ShapeDtypeStruct((m, d), F32),
                 jax.ShapeDtypeStruct((m, d), F32)]
    out_specs = [pl.BlockSpec((None, nh, tm, kw), lambda i: (i // tiles, 0, i % tiles, 0)),
                 row(kw), row(kv_rank), row(rope_dim), row(d_ssm), row(d), row(d)]
    return pl.pallas_call(
        kern,
        out_shape=out_shape,
        grid=(m // tm,),
        in_specs=[row(d), mod.spec(3, tm, d), mod.spec(4, tm, d), _resident((1, d)),
                  pl.BlockSpec((tm, 2 * LANES), lambda i: (i % cs_tiles, 0)),
                  _resident(w["wa"].shape), _resident(w["wu"].shape), _resident((1, q_rank)),
                  _resident((1, kv_rank)), _resident(w["wuq"].shape), _resident(w["wuk"].shape)],
        out_specs=out_specs,
        compiler_params=_cparams("arbitrary"),
        name="mixer_in",
    )(x, mod.arr, mod.arr, norm_w.reshape(1, d), cs_table, w["wa"], w["wu"], w["qn"], w["kvn"], w["wuq"], w["wuk"])


def _rope_tables(pos, rope_dim):
    half = rope_dim // 2
    inv = ROPE_BASE ** (-jnp.arange(half, dtype=F32) / half)
    ang = pos.astype(F32)[:, None] * inv[None, :]
    cos, sin = jnp.cos(ang), jnp.sin(ang)
    pad = jnp.zeros((pos.shape[0], LANES - rope_dim), F32)
    return jnp.concatenate([cos, cos, pad, -sin, sin, pad], axis=-1)


def _swap_halves(w, rope_dim):
    half = rope_dim // 2
    return jnp.concatenate([w[..., half:], w[..., :half]], axis=-1)


def _pad_lanes(w):
    return jnp.pad(w, [(0, 0)] * (w.ndim - 1) + [(0, LANES - w.shape[-1])])


def _mixer_in_weights(w_in, q_norm, w_uq, kv_norm, w_uk, dims):
    nh, q_rank, kv_rank, rope_dim, nope_dim, d_ssm = (dims[k] for k in
                                                       ("n_heads", "q_rank", "kv_rank", "rope_dim", "nope_dim", "d_ssm"))
    o1, o2 = q_rank + kv_rank, q_rank + kv_rank + rope_dim
    w_cq, w_ckv, w_kr, w_rest = w_in[:, :q_rank], w_in[:, q_rank:o1], w_in[:, o1:o2], w_in[:, o2:]
    wa = jnp.concatenate([w_cq, _pad_lanes(w_kr), _pad_lanes(_swap_halves(w_kr, rope_dim)), w_ckv], axis=-1)
    wq = w_uq.reshape(q_rank, nh, nope_dim + rope_dim)
    wq_nope = wq[..., :nope_dim].reshape(q_rank, nh * nope_dim)
    wq_rope = wq[..., nope_dim:]
    wq_a = _pad_lanes(wq_rope).reshape(q_rank, nh * LANES)
    wq_b = _pad_lanes(_swap_halves(wq_rope, rope_dim)).reshape(q_rank, nh * LANES)
    return {"wa": wa.astype(BF16), "wu": w_rest.astype(BF16),
            "qn": q_norm.reshape(1, q_rank), "kvn": kv_norm.reshape(1, kv_rank),
            "wuq": jnp.concatenate([wq_nope, wq_a, wq_b], axis=-1).astype(BF16),
            "wuk": w_uk.transpose(1, 2, 0).astype(BF16)}


def _softmax_step(s, v, m_scr, l_scr, acc_scr):
    m_prev = m_scr[...]
    m_new = jnp.maximum(m_prev, jnp.max(s, axis=1, keepdims=True))
    alpha = jnp.exp(m_prev - m_new)
    p = jnp.exp(s - m_new)
    l_scr[...] = alpha * l_scr[...] + jnp.sum(p, axis=1, keepdims=True)
    acc_scr[...] = alpha * acc_scr[...] + jnp.dot(p.astype(BF16), v, preferred_element_type=F32)
    m_scr[...] = m_new


_NT = (((1,), (1,)), ((), ()))


def _lane_tile(x, width):
    return x if width == LANES else jnp.concatenate([x] * (width // LANES), axis=1)


def _attn_prompt_kernel(q_ref, k_ref, o_ref, m_scr, l_scr, acc_scr, *, tq, n_heads, kv_rank):
    qi = pl.program_id(1)
    m_scr[...] = jnp.full_like(m_scr, -jnp.inf)
    l_scr[...] = jnp.zeros_like(l_scr)
    acc_scr[...] = jnp.zeros_like(acc_scr)

    def step(j, masked):
        k = k_ref[pl.ds(pl.multiple_of(j * tq, tq), tq), :]
        v = k[:, :kv_rank]
        if masked:
            keep = (lax.broadcasted_iota(jnp.int32, (tq, tq), 1) <= lax.broadcasted_iota(jnp.int32, (tq, tq), 0))
        for hd in range(n_heads):
            s = lax.dot_general(q_ref[hd], k, _NT, preferred_element_type=F32)
            if masked:
                s = jnp.where(keep, s, -jnp.inf)
            m_prev = m_scr[hd]
            m_new = jnp.maximum(m_prev, jnp.max(s, axis=1, keepdims=True))
            alpha = jnp.exp(m_prev - m_new)
            p = jnp.exp(s - _lane_tile(m_new, tq))
            l_scr[hd] = alpha * l_scr[hd] + jnp.sum(p, axis=1, keepdims=True)
            acc_scr[hd] = (_lane_tile(alpha, kv_rank) * acc_scr[hd]
                           + jnp.dot(p.astype(BF16), v, preferred_element_type=F32))
            m_scr[hd] = m_new

    def body(j, carry):
        step(j, False)
        return carry

    lax.fori_loop(0, qi, body, 0)
    step(qi, True)
    for hd in range(n_heads):
        o_ref[hd] = (acc_scr[hd] / _lane_tile(l_scr[hd], kv_rank)).astype(BF16)


def _attn_prompt(q, kc, kv_rank, tq=256):
    b, nh, t, kw = q.shape
    tq = min(tq, t)
    assert t % tq == 0 and tq % LANES == 0 and kv_rank % LANES == 0
    return pl.pallas_call(
        functools.partial(_attn_prompt_kernel, tq=tq, n_heads=nh, kv_rank=kv_rank),
        out_shape=jax.ShapeDtypeStruct((b, nh, t, kv_rank), BF16),
        grid=(b, t // tq),
        in_specs=[pl.BlockSpec((None, nh, tq, kw), lambda bi, qi: (bi, 0, qi, 0)),
                  pl.BlockSpec((None, t, kw), lambda bi, qi: (bi, 0, 0))],
        out_specs=pl.BlockSpec((None, nh, tq, kv_rank), lambda bi, qi: (bi, 0, qi, 0)),
        scratch_shapes=[pltpu.VMEM((nh, tq, LANES), F32), pltpu.VMEM((nh, tq, LANES), F32),
                        pltpu.VMEM((nh, tq, kv_rank), F32)],
        compiler_params=_cparams("arbitrary", "arbitrary"),
        name="attn_prompt",
    )(q, kc)


def _attn_sample_kernel(pt_ref, q_ref, knew_ref, ckv_hbm, krt_hbm, o_ref, kbuf, krbuf, sem,
                        *, layer, n_pages, page, kv_rank, rope_dim, new_len, n_req):
    b = pl.program_id(0)
    slot = lax.rem(b, 2)

    def page_copies(req, slot_, p):
        pid = pt_ref[req, p]
        off = pl.multiple_of(p * page, page)
        return (pltpu.make_async_copy(ckv_hbm.at[layer, pid], kbuf.at[slot_, pl.ds(off, page), :], sem.at[0, slot_]),
                pltpu.make_async_copy(krt_hbm.at[layer, pid], krbuf.at[slot_, :, pl.ds(off, page)], sem.at[1, slot_]))

    def start_all(req, slot_):
        def body(p, carry):
            for cp in page_copies(req, slot_, p):
                cp.start()
            return carry
        lax.fori_loop(0, n_pages, body, 0)

    def wait_all(req, slot_):
        def body(p, carry):
            for cp in page_copies(req, slot_, p):
                cp.wait()
            return carry
        lax.fori_loop(0, n_pages, body, 0)

    @pl.when(b == 0)
    def _():
        start_all(0, 0)

    @pl.when(b + 1 < n_req)
    def _():
        start_all(b + 1, 1 - slot)

    wait_all(b, slot)

    q = q_ref[...]
    rows = q.shape[0]
    kb = kbuf[slot].astype(BF16)
    s = (lax.dot_general(q[:, :kv_rank], kb, _NT, preferred_element_type=F32)
         + jnp.dot(q[:, kv_rank:kv_rank + rope_dim], krbuf[slot].astype(BF16), preferred_element_type=F32))
    kn = knew_ref[...]
    npad = kn.shape[0]
    sn = lax.dot_general(q, kn, _NT, preferred_element_type=F32)
    q_pos = jnp.bitwise_and(lax.broadcasted_iota(jnp.int32, (rows, npad), 0), new_len - 1)
    k_pos = lax.broadcasted_iota(jnp.int32, (rows, npad), 1)
    sn = jnp.where(k_pos <= q_pos, sn, -jnp.inf)
    m = jnp.maximum(jnp.max(s, axis=1, keepdims=True), jnp.max(sn, axis=1, keepdims=True))
    p = jnp.exp(s - m)
    pn = jnp.exp(sn - m)
    denom = jnp.sum(p, axis=1, keepdims=True) + jnp.sum(pn, axis=1, keepdims=True)
    o = (jnp.dot(p.astype(BF16), kb, preferred_element_type=F32)
         + jnp.dot(pn.astype(BF16), kn[:, :kv_rank], preferred_element_type=F32))
    o_ref[...] = (o / denom).astype(BF16)


def _attn_sample(q, k_new, cache_ckv, cache_krt, page_table, layer, kv_rank, rope_dim, new_len):
    bs, rows, kw = q.shape
    n_pages = page_table.shape[1]
    page = cache_ckv.shape[2]
    past = n_pages * page
    new_pad = k_new.shape[1]
    assert new_len & (new_len - 1) == 0 and page % LANES == 0
    kern = functools.partial(_attn_sample_kernel, layer=layer, n_pages=n_pages, page=page, kv_rank=kv_rank,
                             rope_dim=rope_dim, new_len=new_len, n_req=bs)
    return pl.pallas_call(
        kern,
        out_shape=jax.ShapeDtypeStruct((bs, rows, kv_rank), BF16),
        grid_spec=pltpu.PrefetchScalarGridSpec(
            num_scalar_prefetch=1,
            grid=(bs,),
            in_specs=[pl.BlockSpec((None, rows, kw), lambda b, pt: (b, 0, 0)),
                      pl.BlockSpec((None, new_pad, kw), lambda b, pt: (b, 0, 0)),
                      pl.BlockSpec(memory_space=pl.ANY),
                      pl.BlockSpec(memory_space=pl.ANY)],
            out_specs=pl.BlockSpec((None, rows, kv_rank), lambda b, pt: (b, 0, 0)),
            scratch_shapes=[pltpu.VMEM((2, past, kv_rank), F32),
                            pltpu.VMEM((2, rope_dim, past), F32),
                            pltpu.SemaphoreType.DMA((2, 2))]),
        compiler_params=_cparams("arbitrary"),
        name="attn_sample",
    )(page_table, q, k_new, cache_ckv, cache_krt)


def _gelu_tanh(y):
    return 0.5 * y * (1.0 + jnp.tanh(math.sqrt(2.0 / math.pi) * (y + 0.044715 * (y * y * y))))


def _ssm_kernel(u_ref, kmat_ref, bend_ref, cpow_ref, lre_ref, lim_ref, dsk_ref, h0_ref, z_ref, hn_ref,
                h_scr, s_scr, y_scr, *, chunk, n_seq, n_chunks):
    nb, tt, lanes = u_ref.shape
    rpb = tt // chunk
    half = lre_ref.shape[-1]

    @pl.when(pl.program_id(1) == 0)
    def _():
        h_scr[...] = h0_ref[...]

    u_rows = jnp.concatenate(
        [jnp.concatenate([u_ref[b, pl.ds(s, rpb, stride=chunk), :] for s in range(chunk)], axis=1)
         for b in range(nb)], axis=0)
    ub = u_rows.astype(BF16)
    own = jnp.dot(ub, bend_ref[...], preferred_element_type=F32)
    nt = half // lanes
    tile = lambda x, j: x[:, j * lanes:(j + 1) * lanes]
    for j in range(2 * nt):
        s_scr[j] = tile(own, j)
    lre = [tile(lre_ref[...], j) for j in range(nt)]
    lim = [tile(lim_ref[...], j) for j in range(nt)]
    h = h_scr[...]
    h_re = [tile(h, j) for j in range(nt)]
    h_im = [tile(h, nt + j) for j in range(nt)]
    for n in range(n_chunks):
        rows = pl.ds(n, n_seq, stride=n_chunks) if n_chunks > 1 else slice(None)
        for j in range(nt):
            s_re, s_im = s_scr[j, rows, :], s_scr[nt + j, rows, :]
            s_scr[j, rows, :] = h_re[j]
            s_scr[nt + j, rows, :] = h_im[j]
            h_re[j], h_im[j] = (lre[j] * h_re[j] - lim[j] * h_im[j] + s_re,
                                lre[j] * h_im[j] + lim[j] * h_re[j] + s_im)
    h_scr[...] = jnp.concatenate(h_re + h_im, axis=1)
    hn_ref[...] = h_scr[...]
    entering = jnp.concatenate([s_scr[j] for j in range(2 * nt)], axis=1)
    y = (jnp.dot(ub, kmat_ref[...], preferred_element_type=F32)
         + jnp.dot(entering.astype(BF16), cpow_ref[...], preferred_element_type=F32))
    for b in range(nb):
        for t in range(chunk):
            y_scr[b, pl.ds(t, rpb, stride=chunk), :] = y[b * rpb:(b + 1) * rpb, t * lanes:(t + 1) * lanes]
    z_ref[...] = _gelu_tanh(y_scr[...] + dsk_ref[...] * u_ref[...]).astype(BF16)


def _ssm(u3, p, h0, chunk, n_seq, tt=512):
    nb, t, d = u3.shape
    n_sg = d // LANES
    tt = min(tt, t)
    rows = nb * tt // chunk
    n_chunks = rows // n_seq
    width = chunk * LANES
    n_state = h0.shape[-1]
    per_sg = lambda a, b: pl.BlockSpec((None, a, b), lambda sg, i: (sg, 0, 0))
    tok = pl.BlockSpec((nb, tt, LANES), lambda sg, i: (0, i, sg))
    return pl.pallas_call(
        functools.partial(_ssm_kernel, chunk=chunk, n_seq=n_seq, n_chunks=n_chunks),
        out_shape=[jax.ShapeDtypeStruct((nb, t, d), BF16), jax.ShapeDtypeStruct(h0.shape, F32)],
        grid=(n_sg, t // tt),
        in_specs=[tok, per_sg(width, width), per_sg(width, n_state), per_sg(n_state, width),
                  per_sg(1, n_state // 2), per_sg(1, n_state // 2), per_sg(1, LANES), per_sg(n_seq, n_state)],
        out_specs=[tok, per_sg(n_seq, n_state)],
        scratch_shapes=[pltpu.VMEM((n_seq, n_state), F32), pltpu.VMEM((n_state // LANES, rows, LANES), F32),
                        pltpu.VMEM((nb, tt, LANES), F32)],
        compiler_params=_cparams("arbitrary", "arbitrary"),
        name="ssm",
    )(u3, p["kmat"], p["bend"], p["cpow"], p["lre"], p["lim"], p["dsk"], h0)


def _ssm_params(a_re, a_im, log_dt, b_re, b_im, c_re, c_im, d_skip, chunk):
    hp = lax.Precision.HIGHEST
    g, n_state = a_re.shape
    cw = SSM_GROUP_WIDTH
    gl = LANES // cw
    n_sg = g // gl
    lam = lax.complex(a_re, a_im)
    lam_dt = lam * jnp.exp(log_dt)[:, None]
    lam_bar = jnp.exp(lam_dt)
    b_bar = ((lam_bar - 1.0) / lam)[..., None] * lax.complex(b_re, b_im)
    cc = lax.complex(c_re, c_im)
    steps = jnp.arange(chunk + 1, dtype=F32)
    pw = jnp.exp(lam_dt[None] * steps[:, None, None])
    eye = jnp.eye(gl, dtype=F32)
    lag = jnp.einsum("gcp,kgp,gpd->gkdc", cc, pw[:chunk], b_bar, precision=hp).real
    t_idx = jnp.arange(chunk)
    diff = t_idx[None, :] - t_idx[:, None]
    kg = jnp.where((diff >= 0)[None, :, None, :, None],
                   lag[:, jnp.clip(diff, 0, chunk - 1)].transpose(0, 1, 3, 2, 4), 0.0)
    kmat = jnp.einsum("Sgadtc,hg->Sahdtgc", kg.reshape(n_sg, gl, chunk, cw, chunk, cw), eye)
    kmat = kmat.reshape(n_sg, chunk * LANES, chunk * LANES)
    bend = jnp.einsum("sgp,gpd->gsdp", pw[chunk - 1 - t_idx], b_bar, precision=hp)
    bend = bend.reshape(n_sg, gl, chunk, cw, n_state)
    spread_b = lambda x: jnp.einsum("Sgadp,hg->Sahdgp", x, eye).reshape(n_sg, chunk * LANES, gl * n_state)
    bend = jnp.concatenate([spread_b(bend.real), spread_b(bend.imag)], axis=-1)
    cpow = jnp.einsum("gcp,tgp->gptc", cc, pw[1:], precision=hp).reshape(n_sg, gl, n_state, chunk, cw)
    spread_c = lambda x: jnp.einsum("Sgptc,hg->Shptgc", x, eye).reshape(n_sg, gl * n_state, chunk * LANES)
    cpow = jnp.concatenate([spread_c(cpow.real), spread_c(-cpow.imag)], axis=1)
    lam_l = pw[chunk].reshape(n_sg, 1, gl * n_state)
    return {"kmat": kmat.astype(BF16), "bend": bend.astype(BF16), "cpow": cpow.astype(BF16),
            "lre": lam_l.real, "lim": lam_l.imag, "dsk": d_skip.reshape(n_sg, 1, LANES)}


def _pack_state(h_re, h_im):
    b, g, p = h_re.shape
    gl = LANES // SSM_GROUP_WIDTH
    f = lambda h: h.reshape(b, g // gl, gl * p).transpose(1, 0, 2)
    return jnp.concatenate([f(h_re), f(h_im)], axis=-1)


def _unpack_state(h, n_state):
    n_sg, b, w = h.shape
    f = lambda x: x.transpose(1, 0, 2).reshape(b, -1, n_state)
    return f(h[..., :w // 2]), f(h[..., w // 2:])


def _mixer_out_kernel(x_ref, gm_ref, z_ref, o_ref, sga_ref, sgb_ref, wglu_ref, wuv_ref, wout_ref, y_ref, mixed_scr,
                      *, n_heads, v_dim):
    d = x_ref.shape[-1]
    glu = jnp.dot(z_ref[...], wglu_ref[...], preferred_element_type=F32)
    y_ssm = glu[:, :d] * _sigmoid(glu[:, d:])
    mixed_scr[...] = (sgb_ref[...] * y_ssm).astype(mixed_scr.dtype)
    for hd in range(n_heads):
        cols = slice(hd * v_dim, (hd + 1) * v_dim)
        y_attn = jnp.dot(o_ref[hd], wuv_ref[hd], preferred_element_type=F32)
        mixed_scr[:, cols] = mixed_scr[:, cols] + sga_ref[:, cols] * y_attn
    proj = jnp.dot(mixed_scr[...].astype(BF16), wout_ref[...], preferred_element_type=F32)
    y_ref[...] = x_ref[...] + gm_ref[...] * proj


def _mixer_out(x, mod, z, o_lat, sga, sgb, w, n_batch, tm=512):
    m, d = x.shape
    rows = m // n_batch
    tm = min(tm, rows)
    tiles = rows // tm
    _, nh, _, kv_rank = o_lat.shape
    v_dim = w["wuv"].shape[-1]
    row = lambda width: pl.BlockSpec((tm, width), lambda i: (i, 0))
    return pl.pallas_call(
        functools.partial(_mixer_out_kernel, n_heads=nh, v_dim=v_dim),
        out_shape=jax.ShapeDtypeStruct((m, d), F32),
        grid=(m // tm,),
        in_specs=[row(d), mod.spec(5, tm, d), row(d),
                  pl.BlockSpec((None, nh, tm, kv_rank), lambda i: (i // tiles, 0, i % tiles, 0)),
                  row(d), row(d), _resident(w["wglu"].shape), _resident(w["wuv"].shape), _resident(w["wout"].shape)],
        out_specs=row(d),
        scratch_shapes=[pltpu.VMEM((tm, d), F32)],
        compiler_params=_cparams("arbitrary"),
        name="mixer_out",
    )(x, mod.arr, z, o_lat, sga, sgb, w["wglu"], w["wuv"], w["wout"])


def _layer(x, mod, n_batch, cs_table, lw, dims, attend, ssm_seqs, ssm_chunk, h0, final_w):
    m, d = x.shape
    x = _ffn(x, mod, 0, lw["norm_ffn1"], *lw["ffn1"])
    q, kc, ckv, kr, u, sga, sgb = _mixer_in(x, mod, lw["norm_mix"], cs_table, lw["mix_in"], dims, n_batch)
    o_lat = attend(q, kc)
    z, h_n = _ssm(u.reshape(n_batch, m // n_batch, u.shape[-1]), lw["ssm"][ssm_chunk], h0, ssm_chunk, ssm_seqs)
    x = _mixer_out(x, mod, z.reshape(m, z.shape[-1]), o_lat, sga, sgb, lw["mix_out"], n_batch)
    x = _ffn(x, mod, 6, lw["norm_ffn2"], *lw["ffn2"], final_w=final_w)
    return x, ckv, kr, h_n


def kernel(x_prompt, x_sample, c_prompt, c_sample, cache_ckv, cache_kr, state_ssm_re, state_ssm_im, page_table, ada_w, ada_b, norm_ffn1, ffn1_up, ffn1_down, norm_mix, w_in, q_norm, w_uq, kv_norm, w_uk, w_uv, ssm_a_re, ssm_a_im, ssm_log_dt, ssm_b_re, ssm_b_im, ssm_c_re, ssm_c_im, ssm_d, w_glu, w_out, norm_ffn2, ffn2_up, ffn2_down, final_norm):
    bp, tp, d = x_prompt.shape
    bs, ts, _ = x_sample.shape
    depth = ada_w.shape[0]
    q_rank, kv_rank = q_norm.shape[-1], kv_norm.shape[-1]
    rope_dim = cache_kr.shape[-1]
    nh, nope_dim = w_uk.shape[2], w_uk.shape[3]
    n_groups, n_state = ssm_a_re.shape[1], ssm_a_re.shape[2]
    dims = {"n_heads": nh, "q_rank": q_rank, "kv_rank": kv_rank, "rope_dim": rope_dim, "nope_dim": nope_dim,
            "d_ssm": ssm_d.shape[-1]}
    past_len = page_table.shape[1] * cache_ckv.shape[2]
    prompt_chunk = 8 if tp % 8 == 0 else tp
    sample_chunk = ts
    cache_krt = jnp.swapaxes(cache_kr, 2, 3)

    mod_all = _adaln(jnp.concatenate([c_prompt, c_sample], axis=0), ada_w, ada_b)
    cs_p = _rope_tables(jnp.arange(tp), rope_dim)
    cs_s = jnp.tile(_rope_tables(past_len + jnp.arange(ts), rope_dim), (bs, 1))

    xp = x_prompt.reshape(bp * tp, d)
    xs = x_sample.reshape(bs * ts, d)
    zeros_state = jnp.zeros((bp, n_groups, n_state), F32)
    outs = {k: [] for k in ("ckv_p", "kr_p", "hre_p", "him_p", "ckv_s", "kr_s", "hre_s", "him_s")}
    for l in range(depth):
        ssm_args = (ssm_a_re[l], ssm_a_im[l], ssm_log_dt[l], ssm_b_re[l], ssm_b_im[l], ssm_c_re[l], ssm_c_im[l], ssm_d[l])
        lw = {"norm_ffn1": norm_ffn1[l], "norm_mix": norm_mix[l], "norm_ffn2": norm_ffn2[l],
              "ffn1": _chunk_ffn_weights(ffn1_up[l], ffn1_down[l]),
              "ffn2": _chunk_ffn_weights(ffn2_up[l], ffn2_down[l]),
              "mix_in": _mixer_in_weights(w_in[l], q_norm[l], w_uq[l], kv_norm[l], w_uk[l], dims),
              "mix_out": {"wglu": w_glu[l].astype(BF16), "wuv": w_uv[l].transpose(1, 0, 2).astype(BF16),
                          "wout": w_out[l].astype(BF16)},
              "ssm": {c: _ssm_params(*ssm_args, c) for c in {prompt_chunk, sample_chunk}}}
        last = final_norm if l == depth - 1 else None

        mod_p = _Mod(mod_all[l, :bp].reshape(bp, N_MODULATIONS, 1, d), False, tp)
        xp, ckv, kr, h_n = _layer(
            xp, mod_p, bp, cs_p, lw, dims,
            lambda q, kc: _attn_prompt(q, kc.reshape(bp, tp, kc.shape[-1]), kv_rank), bp, prompt_chunk,
            _pack_state(zeros_state, zeros_state), last)
        h_re, h_im = _unpack_state(h_n, n_state)
        outs["ckv_p"].append(ckv.reshape(bp, tp, kv_rank)); outs["kr_p"].append(kr.reshape(bp, tp, rope_dim))
        outs["hre_p"].append(h_re); outs["him_p"].append(h_im)

        mod_rows = jnp.repeat(mod_all[l, bp:].reshape(bs, N_MODULATIONS, d), ts, axis=0)
        mod_s = _Mod(mod_rows.transpose(1, 0, 2), True, ts)

        def attend_sample(q, kc, l=l):
            kw = q.shape[-1]
            qb = q.reshape(nh, bs, ts, kw).transpose(1, 0, 2, 3).reshape(bs, nh * ts, kw)
            k_new = jnp.pad(kc.reshape(bs, ts, kw), ((0, 0), (0, 16 - ts), (0, 0)))
            o = _attn_sample(qb, k_new, cache_ckv, cache_krt, page_table, l, kv_rank, rope_dim, ts)
            return o.reshape(bs, nh, ts, kv_rank).transpose(1, 0, 2, 3).reshape(1, nh, bs * ts, kv_rank)

        xs, ckv, kr, h_n = _layer(
            xs, mod_s, 1, cs_s, lw, dims,
            attend_sample, bs, sample_chunk, _pack_state(state_ssm_re[l], state_ssm_im[l]), last)
        h_re, h_im = _unpack_state(h_n, n_state)
        outs["ckv_s"].append(ckv.reshape(bs, ts, kv_rank)); outs["kr_s"].append(kr.reshape(bs, ts, rope_dim))
        outs["hre_s"].append(h_re); outs["him_s"].append(h_im)

    st = lambda k: jnp.stack(outs[k])
    return (xp.reshape(bp, tp, d), xs.reshape(bs, ts, d),
            st("ckv_p"), st("kr_p"), st("hre_p"), st("him_p"),
            st("ckv_s"), st("kr_s"), st("hre_s"), st("him_s"))
```

```python
import functools
import math

import jax
import jax.numpy as jnp
from jax import lax
from jax.experimental import pallas as pl
from jax.experimental.pallas import tpu as pltpu

F32 = jnp.float32
BF16 = jnp.bfloat16

NORM_EPS = 1e-6
ROPE_BASE = 10000.0
SSM_GROUP_WIDTH = 16
N_MODULATIONS = 9
LANES = 128
VMEM_LIMIT_BYTES = 56 * 1024 * 1024


def _cparams(*sem):
    return pltpu.CompilerParams(dimension_semantics=sem, vmem_limit_bytes=VMEM_LIMIT_BYTES)


def _sigmoid(x):
    return 1.0 / (1.0 + jnp.exp(-x))


def _rms(x, w):
    return x * lax.rsqrt(jnp.mean(x * x, axis=-1, keepdims=True) + NORM_EPS) * w


def _resident(shape):
    zeros = (0,) * len(shape)
    return pl.BlockSpec(shape, lambda *_: zeros, pipeline_mode=pl.Buffered(1))


def _adaln_kernel(c_ref, w_ref, b_ref, o_ref):
    c = c_ref[...]
    a = (c * _sigmoid(c)).astype(BF16)
    o_ref[...] = jnp.dot(a, w_ref[...].astype(BF16), preferred_element_type=F32) + b_ref[...]


def _adaln(c_all, ada_w, ada_b, tn=1536):
    depth, d, n = ada_w.shape
    rows = c_all.shape[0]
    return pl.pallas_call(
        _adaln_kernel,
        out_shape=jax.ShapeDtypeStruct((depth, rows, n), F32),
        grid=(depth, n // tn),
        in_specs=[pl.BlockSpec((rows, d), lambda l, j: (0, 0)),
                  pl.BlockSpec((None, d, tn), lambda l, j: (l, 0, j)),
                  pl.BlockSpec((None, 1, tn), lambda l, j: (l, 0, j))],
        out_specs=pl.BlockSpec((None, rows, tn), lambda l, j: (l, 0, j)),
        compiler_params=_cparams("arbitrary", "arbitrary"),
        name="adaln",
    )(c_all, ada_w, ada_b.reshape(depth, 1, n))


class _Mod:
    def __init__(self, arr, per_token, rows_per_batch):
        self.arr = arr
        self.per_token = per_token
        self.rows_per_batch = rows_per_batch

    def spec(self, k, tm, d):
        if self.per_token:
            return pl.BlockSpec((None, tm, d), lambda i: (k, i, 0))
        tiles_per_batch = self.rows_per_batch // tm
        return pl.BlockSpec((None, None, 1, d), lambda i: (i // tiles_per_batch, k, 0, 0))


def _ffn_kernel(x_ref, sh_ref, sc_ref, g_ref, nw_ref, wup_ref, wdn_ref, *rest, n_chunks, tf, final):
    if final:
        fn_ref, o_ref, h_scr, acc_scr = rest
    else:
        o_ref, h_scr, acc_scr = rest
    x = x_ref[...]
    h_scr[...] = (_rms(x, nw_ref[...]) * (1.0 + sc_ref[...]) + sh_ref[...]).astype(BF16)
    acc_scr[...] = jnp.zeros_like(acc_scr)

    def chunk(c, carry):
        ab = jnp.dot(h_scr[...], wup_ref[c], preferred_element_type=F32)
        a, b = ab[:, :tf], ab[:, tf:]
        act = (a * _sigmoid(a) * b).astype(BF16)
        acc_scr[...] += jnp.dot(act, wdn_ref[c], preferred_element_type=F32)
        return carry

    lax.fori_loop(0, n_chunks, chunk, 0)
    y = x + (0.5 * g_ref[...]) * acc_scr[...]
    if final:
        y = _rms(y, fn_ref[...])
    o_ref[...] = y


def _ffn(x, mod, k0, norm_w, wup_c, wdn_c, final_w=None, tm=512):
    m, d = x.shape
    tm = min(tm, m)
    n_chunks, _, tf2 = wup_c.shape
    tf = tf2 // 2
    final = final_w is not None
    in_specs = [pl.BlockSpec((tm, d), lambda i: (i, 0)),
                mod.spec(k0, tm, d), mod.spec(k0 + 1, tm, d), mod.spec(k0 + 2, tm, d),
                _resident((1, d)), _resident(wup_c.shape), _resident(wdn_c.shape)]
    args = [x, mod.arr, mod.arr, mod.arr, norm_w.reshape(1, d), wup_c, wdn_c]
    if final:
        in_specs.append(_resident((1, d)))
        args.append(final_w.reshape(1, d))
    return pl.pallas_call(
        functools.partial(_ffn_kernel, n_chunks=n_chunks, tf=tf, final=final),
        out_shape=jax.ShapeDtypeStruct((m, d), F32),
        grid=(m // tm,),
        in_specs=in_specs,
        out_specs=pl.BlockSpec((tm, d), lambda i: (i, 0)),
        scratch_shapes=[pltpu.VMEM((tm, d), BF16), pltpu.VMEM((tm, d), F32)],
        compiler_params=_cparams("arbitrary"),
        name="ffn",
    )(*args)


def _chunk_ffn_weights(w_up, w_down, tf=256):
    d, f2 = w_up.shape
    f = f2 // 2
    n_chunks = f // tf
    a = w_up[:, :f].reshape(d, n_chunks, tf).transpose(1, 0, 2)
    b = w_up[:, f:].reshape(d, n_chunks, tf).transpose(1, 0, 2)
    wup_c = jnp.concatenate([a, b], axis=-1).astype(BF16)
    wdn_c = w_down.reshape(n_chunks, tf, d).astype(BF16)
    return wup_c, wdn_c


def _mixer_in_kernel(x_ref, sh_ref, sc_ref, nw_ref, cs_ref, wa_ref, wu_ref, qn_ref, kvn_ref, wuq_ref, wuk_ref,
                     q_ref, kc_ref, *rest,
                     n_heads, q_rank, kv_rank, rope_dim, nope_dim, d_ssm, scale, transposed):
    if transposed:
        vt_ref, ckv_ref, kr_ref, u_ref, sga_ref, sgb_ref = rest
    else:
        ckv_ref, kr_ref, u_ref, sga_ref, sgb_ref = rest
    x = x_ref[...]
    h = (_rms(x, nw_ref[...]) * (1.0 + sc_ref[...]) + sh_ref[...]).astype(BF16)
    cos = cs_ref[:, :LANES]
    sin = cs_ref[:, LANES:]

    t = jnp.dot(h, wa_ref[...], preferred_element_type=F32)
    cq = t[:, :q_rank]
    kr = t[:, q_rank:q_rank + LANES] * cos + t[:, q_rank + LANES:q_rank + 2 * LANES] * sin
    ckv = _rms(t[:, q_rank + 2 * LANES:], kvn_ref[...])
    kr_ref[...] = kr[:, :rope_dim]
    ckv_ref[...] = ckv
    kc_ref[:, :kv_rank] = ckv.astype(BF16)
    kc_ref[:, kv_rank:] = kr.astype(BF16)
    if transposed:
        vt_ref[...] = ckv.T.astype(BF16)

    ug = jnp.dot(h, wu_ref[...], preferred_element_type=F32)
    u_ref[...] = ug[:, :d_ssm]
    d_model = sga_ref.shape[-1]
    sga_ref[...] = _sigmoid(ug[:, d_ssm:d_ssm + d_model])
    sgb_ref[...] = _sigmoid(ug[:, d_ssm + d_model:])

    cqn = _rms(cq, qn_ref[...]).astype(BF16)
    q = jnp.dot(cqn, wuq_ref[...], preferred_element_type=F32)
    base_a = n_heads * nope_dim
    base_b = base_a + n_heads * LANES
    for hd in range(n_heads):
        nope = q[:, hd * nope_dim:(hd + 1) * nope_dim].astype(BF16)
        q_lat = jnp.dot(nope, wuk_ref[hd], preferred_element_type=F32) * scale
        q_rot = (q[:, base_a + hd * LANES:base_a + (hd + 1) * LANES] * cos
                 + q[:, base_b + hd * LANES:base_b + (hd + 1) * LANES] * sin) * scale
        if transposed:
            q_ref[hd, :kv_rank, :] = q_lat.T.astype(BF16)
            q_ref[hd, kv_rank:, :] = q_rot.T.astype(BF16)
        else:
            q_ref[hd, :, :kv_rank] = q_lat.astype(BF16)
            q_ref[hd, :, kv_rank:] = q_rot.astype(BF16)


def _mixer_in(x, mod, norm_w, cs_table, w, dims, n_batch, transposed, tm=256):
    m, d = x.shape
    tm = min(tm, m // n_batch)
    rows = m // n_batch
    tiles = rows // tm
    cs_tiles = cs_table.shape[0] // tm
    nh, q_rank, kv_rank, rope_dim, nope_dim, d_ssm = (dims[k] for k in
                                                       ("n_heads", "q_rank", "kv_rank", "rope_dim", "nope_dim", "d_ssm"))
    kw = kv_rank + LANES
    kern = functools.partial(_mixer_in_kernel, n_heads=nh, q_rank=q_rank, kv_rank=kv_rank, rope_dim=rope_dim,
                             nope_dim=nope_dim, d_ssm=d_ssm, scale=(nope_dim + rope_dim) ** -0.5, transposed=transposed)
    row = lambda width: pl.BlockSpec((tm, width), lambda i: (i, 0))
    out_shape = [jax.ShapeDtypeStruct((n_batch, nh, kw, rows) if transposed else (n_batch, nh, rows, kw), BF16),
                 jax.ShapeDtypeStruct((m, kw), BF16),
                 jax.ShapeDtypeStruct((m, kv_rank), F32),
                 jax.ShapeDtypeStruct((m, rope_dim), F32),
                 jax.ShapeDtypeStruct((m, d_ssm), F32),
                 jax.ShapeDtypeStruct((m, d), F32),
                 jax.ShapeDtypeStruct((m, d), F32)]
    out_specs = [pl.BlockSpec((None, nh, kw, tm), lambda i: (i // tiles, 0, 0, i % tiles)) if transposed else
                 pl.BlockSpec((None, nh, tm, kw), lambda i: (i // tiles, 0, i % tiles, 0)),
                 row(kw), row(kv_rank), row(rope_dim), row(d_ssm), row(d), row(d)]
    if transposed:
        out_shape.insert(2, jax.ShapeDtypeStruct((n_batch, kv_rank, rows), BF16))
        out_specs.insert(2, pl.BlockSpec((None, kv_rank, tm), lambda i: (i // tiles, 0, i % tiles)))
    return pl.pallas_call(
        kern,
        out_shape=out_shape,
        grid=(m // tm,),
        in_specs=[row(d), mod.spec(3, tm, d), mod.spec(4, tm, d), _resident((1, d)),
                  pl.BlockSpec((tm, 2 * LANES), lambda i: (i % cs_tiles, 0)),
                  _resident(w["wa"].shape), _resident(w["wu"].shape), _resident((1, q_rank)),
                  _resident((1, kv_rank)), _resident(w["wuq"].shape), _resident(w["wuk"].shape)],
        out_specs=out_specs,
        compiler_params=_cparams("arbitrary"),
        name="mixer_in",
    )(x, mod.arr, mod.arr, norm_w.reshape(1, d), cs_table, w["wa"], w["wu"], w["qn"], w["kvn"], w["wuq"], w["wuk"])


def _rope_tables(pos, rope_dim):
    half = rope_dim // 2
    inv = ROPE_BASE ** (-jnp.arange(half, dtype=F32) / half)
    ang = pos.astype(F32)[:, None] * inv[None, :]
    cos, sin = jnp.cos(ang), jnp.sin(ang)
    pad = jnp.zeros((pos.shape[0], LANES - rope_dim), F32)
    return jnp.concatenate([cos, cos, pad, -sin, sin, pad], axis=-1)


def _swap_halves(w, rope_dim):
    half = rope_dim // 2
    return jnp.concatenate([w[..., half:], w[..., :half]], axis=-1)


def _pad_lanes(w):
    return jnp.pad(w, [(0, 0)] * (w.ndim - 1) + [(0, LANES - w.shape[-1])])


def _mixer_in_weights(w_in, q_norm, w_uq, kv_norm, w_uk, dims):
    nh, q_rank, kv_rank, rope_dim, nope_dim, d_ssm = (dims[k] for k in
                                                       ("n_heads", "q_rank", "kv_rank", "rope_dim", "nope_dim", "d_ssm"))
    o1, o2 = q_rank + kv_rank, q_rank + kv_rank + rope_dim
    w_cq, w_ckv, w_kr, w_rest = w_in[:, :q_rank], w_in[:, q_rank:o1], w_in[:, o1:o2], w_in[:, o2:]
    wa = jnp.concatenate([w_cq, _pad_lanes(w_kr), _pad_lanes(_swap_halves(w_kr, rope_dim)), w_ckv], axis=-1)
    wq = w_uq.reshape(q_rank, nh, nope_dim + rope_dim)
    wq_nope = wq[..., :nope_dim].reshape(q_rank, nh * nope_dim)
    wq_rope = wq[..., nope_dim:]
    wq_a = _pad_lanes(wq_rope).reshape(q_rank, nh * LANES)
    wq_b = _pad_lanes(_swap_halves(wq_rope, rope_dim)).reshape(q_rank, nh * LANES)
    return {"wa": wa.astype(BF16), "wu": w_rest.astype(BF16),
            "qn": q_norm.reshape(1, q_rank), "kvn": kv_norm.reshape(1, kv_rank),
            "wuq": jnp.concatenate([wq_nope, wq_a, wq_b], axis=-1).astype(BF16),
            "wuk": w_uk.transpose(1, 2, 0).astype(BF16)}


_NT = (((1,), (1,)), ((), ()))


def _attn_prompt_kernel(qt_ref, k_ref, vt_ref, o_ref, m_scr, l_scr, acc_scr, s_scr, *, tq, n_heads):
    qi = pl.program_id(1)
    m_scr[...] = jnp.full_like(m_scr, -jnp.inf)
    l_scr[...] = jnp.zeros_like(l_scr)
    acc_scr[...] = jnp.zeros_like(acc_scr)

    def step(j, masked):
        keys = pl.ds(pl.multiple_of(j * tq, tq), tq)
        k = k_ref[keys, :]
        vt = vt_ref[:, keys]
        if masked:
            keep = (lax.broadcasted_iota(jnp.int32, (tq, tq), 0) <= lax.broadcasted_iota(jnp.int32, (tq, tq), 1))
        for hd in range(n_heads):
            s_scr[hd] = jnp.dot(k, qt_ref[hd], preferred_element_type=F32)
        for hd in range(n_heads):
            s = s_scr[hd]
            if masked:
                s = jnp.where(keep, s, -jnp.inf)
            m_prev = m_scr[hd]
            m_new = jnp.maximum(m_prev, jnp.max(s, axis=0, keepdims=True))
            alpha = jnp.exp(m_prev - m_new)
            p = jnp.exp(s - m_new)
            l_scr[hd] = alpha * l_scr[hd] + jnp.sum(p, axis=0, keepdims=True)
            acc_scr[hd] = alpha * acc_scr[hd] + jnp.dot(vt, p.astype(BF16), preferred_element_type=F32)
            m_scr[hd] = m_new

    def body(j, carry):
        step(j, False)
        return carry

    lax.fori_loop(0, qi, body, 0)
    step(qi, True)
    for hd in range(n_heads):
        o_ref[hd] = (acc_scr[hd] / l_scr[hd]).T.astype(BF16)


def _attn_prompt(qt, kc, vt, tq=256):
    b, nh, kw, t = qt.shape
    kv_rank = vt.shape[1]
    tq = min(tq, t)
    assert t % tq == 0 and tq % LANES == 0 and kv_rank % LANES == 0
    return pl.pallas_call(
        functools.partial(_attn_prompt_kernel, tq=tq, n_heads=nh),
        out_shape=jax.ShapeDtypeStruct((b, nh, t, kv_rank), BF16),
        grid=(b, t // tq),
        in_specs=[pl.BlockSpec((None, nh, kw, tq), lambda bi, qi: (bi, 0, 0, qi)),
                  pl.BlockSpec((None, t, kw), lambda bi, qi: (bi, 0, 0)),
                  pl.BlockSpec((None, kv_rank, t), lambda bi, qi: (bi, 0, 0))],
        out_specs=pl.BlockSpec((None, nh, tq, kv_rank), lambda bi, qi: (bi, 0, qi, 0)),
        scratch_shapes=[pltpu.VMEM((nh, 1, tq), F32), pltpu.VMEM((nh, 1, tq), F32),
                        pltpu.VMEM((nh, kv_rank, tq), F32), pltpu.VMEM((nh, tq, tq), F32)],
        compiler_params=_cparams("arbitrary", "arbitrary"),
        name="attn_prompt",
    )(qt, kc, vt)


def _attn_sample_kernel(pt_ref, q_ref, knew_ref, ckv_hbm, krt_hbm, o_ref, kbuf, krbuf, sem,
                        *, layer, n_pages, page, kv_rank, rope_dim, new_len, n_req, n_spans):
    b = pl.program_id(0)
    slot = lax.rem(b, 2)

    def page_copies(req, slot_, p):
        pid = pt_ref[req, p]
        off = pl.multiple_of(p * page, page)
        return (pltpu.make_async_copy(ckv_hbm.at[layer, pid], kbuf.at[slot_, pl.ds(off, page), :], sem.at[0, slot_]),
                pltpu.make_async_copy(krt_hbm.at[layer, pid], krbuf.at[slot_, :, pl.ds(off, page)], sem.at[1, slot_]))

    def start_all(req, slot_):
        def body(p, carry):
            for cp in page_copies(req, slot_, p):
                cp.start()
            return carry
        lax.fori_loop(0, n_pages, body, 0, unroll=8)

    def wait_all(req, slot_):
        def body(p, carry):
            for cp in page_copies(req, slot_, p):
                cp.wait()
            return carry
        lax.fori_loop(0, n_pages, body, 0, unroll=8)

    @pl.when(b == 0)
    def _():
        start_all(0, 0)

    @pl.when(b + 1 < n_req)
    def _():
        start_all(b + 1, 1 - slot)

    wait_all(b, slot)

    q = q_ref[...]
    rows = q.shape[0]
    q_lat, q_rope = q[:, :kv_rank], q[:, kv_rank:kv_rank + rope_dim]

    def softmax_part(s, v):
        m = jnp.max(s, axis=1, keepdims=True)
        p = jnp.exp(s - m)
        return m, jnp.sum(p, axis=1, keepdims=True), jnp.dot(p.astype(BF16), v, preferred_element_type=F32)

    span = (n_pages // n_spans) * page
    parts = []
    for c in range(n_spans):
        kb = kbuf[slot, c * span:(c + 1) * span, :].astype(BF16)
        krb = krbuf[slot, :, c * span:(c + 1) * span].astype(BF16)
        s = (lax.dot_general(q_lat, kb, _NT, preferred_element_type=F32)
             + jnp.dot(q_rope, krb, preferred_element_type=F32))
        parts.append(softmax_part(s, kb))
    kn = knew_ref[...]
    npad = kn.shape[0]
    sn = lax.dot_general(q, kn, _NT, preferred_element_type=F32)
    q_pos = jnp.bitwise_and(lax.broadcasted_iota(jnp.int32, (rows, npad), 0), new_len - 1)
    k_pos = lax.broadcasted_iota(jnp.int32, (rows, npad), 1)
    parts.append(softmax_part(jnp.where(k_pos <= q_pos, sn, -jnp.inf), kn[:, :kv_rank]))
    m = functools.reduce(jnp.maximum, [pm for pm, _, _ in parts])
    scale = [jnp.exp(pm - m) for pm, _, _ in parts]
    denom = sum(w * pl_ for w, (_, pl_, _) in zip(scale, parts))
    o = sum(w * po for w, (_, _, po) in zip(scale, parts))
    o_ref[...] = (o / denom).astype(BF16)


def _attn_sample(q, k_new, cache_ckv, cache_krt, page_table, layer, kv_rank, rope_dim, new_len):
    bs, rows, kw = q.shape
    n_pages = page_table.shape[1]
    page = cache_ckv.shape[2]
    past = n_pages * page
    new_pad = k_new.shape[1]
    assert new_len & (new_len - 1) == 0 and page % LANES == 0
    kern = functools.partial(_attn_sample_kernel, layer=layer, n_pages=n_pages, page=page, kv_rank=kv_rank,
                             rope_dim=rope_dim, new_len=new_len, n_req=bs,
                             n_spans=1)
    return pl.pallas_call(
        kern,
        out_shape=jax.ShapeDtypeStruct((bs, rows, kv_rank), BF16),
        grid_spec=pltpu.PrefetchScalarGridSpec(
            num_scalar_prefetch=1,
            grid=(bs,),
            in_specs=[pl.BlockSpec((None, rows, kw), lambda b, pt: (b, 0, 0)),
                      pl.BlockSpec((None, new_pad, kw), lambda b, pt: (b, 0, 0)),
                      pl.BlockSpec(memory_space=pl.ANY),
                      pl.BlockSpec(memory_space=pl.ANY)],
            out_specs=pl.BlockSpec((None, rows, kv_rank), lambda b, pt: (b, 0, 0)),
            scratch_shapes=[pltpu.VMEM((2, past, kv_rank), F32),
                            pltpu.VMEM((2, rope_dim, past), F32),
                            pltpu.SemaphoreType.DMA((2, 2))]),
        compiler_params=_cparams("arbitrary"),
        name="attn_sample",
    )(page_table, q, k_new, cache_ckv, cache_krt)


def _gelu_tanh(y):
    return 0.5 * y * (1.0 + jnp.tanh(math.sqrt(2.0 / math.pi) * (y + 0.044715 * (y * y * y))))


def _ssm_kernel(u_ref, kmat_ref, bend_ref, cpow_ref, lre_ref, lim_ref, dsk_ref, h0_ref, z_ref, hn_ref,
                h_scr, s_scr, y_scr, *, chunk, n_seq, n_chunks):
    nb, tt, lanes = u_ref.shape
    rpb = tt // chunk
    half = lre_ref.shape[-1]

    @pl.when(pl.program_id(1) == 0)
    def _():
        h_scr[...] = h0_ref[...]

    u_rows = jnp.concatenate(
        [jnp.concatenate([u_ref[b, pl.ds(s, rpb, stride=chunk), :] for s in range(chunk)], axis=1)
         for b in range(nb)], axis=0)
    ub = u_rows.astype(BF16)
    own = jnp.dot(ub, bend_ref[...], preferred_element_type=F32)
    nt = half // lanes
    tile = lambda x, j: x[:, j * lanes:(j + 1) * lanes]
    def seq_rows(q):
        return pl.ds(q, n_chunks, stride=n_seq) if n_chunks > 1 else pl.ds(q, 1)

    if n_chunks > 1:
        for q in range(n_seq):
            for j in range(2 * nt):
                s_scr[j, seq_rows(q), :] = tile(own[q * n_chunks:(q + 1) * n_chunks], j)
    else:
        for j in range(2 * nt):
            s_scr[j] = tile(own, j)
    lre = [tile(lre_ref[...], j) for j in range(nt)]
    lim = [tile(lim_ref[...], j) for j in range(nt)]
    h = h_scr[...]
    h_re = [tile(h, j) for j in range(nt)]
    h_im = [tile(h, nt + j) for j in range(nt)]
    for n in range(n_chunks):
        rows = pl.ds(n * n_seq, n_seq)
        for j in range(nt):
            s_re, s_im = s_scr[j, rows, :], s_scr[nt + j, rows, :]
            s_scr[j, rows, :] = h_re[j]
            s_scr[nt + j, rows, :] = h_im[j]
            h_re[j], h_im[j] = (lre[j] * h_re[j] - lim[j] * h_im[j] + s_re,
                                lre[j] * h_im[j] + lim[j] * h_re[j] + s_im)
    h_scr[...] = jnp.concatenate(h_re + h_im, axis=1)
    hn_ref[...] = h_scr[...]
    if n_chunks > 1:
        entering = jnp.concatenate(
            [jnp.concatenate([s_scr[j, seq_rows(q), :] for j in range(2 * nt)], axis=1) for q in range(n_seq)], axis=0)
    else:
        entering = jnp.concatenate([s_scr[j] for j in range(2 * nt)], axis=1)
    y = (jnp.dot(ub, kmat_ref[...], preferred_element_type=F32)
         + jnp.dot(entering.astype(BF16), cpow_ref[...], preferred_element_type=F32))
    for b in range(nb):
        for t in range(chunk):
            y_scr[b, pl.ds(t, rpb, stride=chunk), :] = y[b * rpb:(b + 1) * rpb, t * lanes:(t + 1) * lanes]
    z_ref[...] = _gelu_tanh(y_scr[...] + dsk_ref[...] * u_ref[...]).astype(BF16)


def _ssm(u3, p, h0, chunk, n_seq, tt=512):
    nb, t, d = u3.shape
    n_sg = d // LANES
    tt = min(tt, t)
    rows = nb * tt // chunk
    n_chunks = rows // n_seq
    width = chunk * LANES
    n_state = h0.shape[-1]
    per_sg = lambda a, b: pl.BlockSpec((None, a, b), lambda sg, i: (sg, 0, 0))
    tok = pl.BlockSpec((nb, tt, LANES), lambda sg, i: (0, i, sg))
    return pl.pallas_call(
        functools.partial(_ssm_kernel, chunk=chunk, n_seq=n_seq, n_chunks=n_chunks),
        out_shape=[jax.ShapeDtypeStruct((nb, t, d), BF16), jax.ShapeDtypeStruct(h0.shape, F32)],
        grid=(n_sg, t // tt),
        in_specs=[tok, per_sg(width, width), per_sg(width, n_state), per_sg(n_state, width),
                  per_sg(1, n_state // 2), per_sg(1, n_state // 2), per_sg(1, LANES), per_sg(n_seq, n_state)],
        out_specs=[tok, per_sg(n_seq, n_state)],
        scratch_shapes=[pltpu.VMEM((n_seq, n_state), F32), pltpu.VMEM((n_state // LANES, rows, LANES), F32),
                        pltpu.VMEM((nb, tt, LANES), F32)],
        compiler_params=_cparams("arbitrary", "arbitrary"),
        name="ssm",
    )(u3, p["kmat"], p["bend"], p["cpow"], p["lre"], p["lim"], p["dsk"], h0)


def _ssm_params(a_re, a_im, log_dt, b_re, b_im, c_re, c_im, d_skip, chunk):
    hp = lax.Precision.HIGHEST
    g, n_state = a_re.shape
    cw = SSM_GROUP_WIDTH
    gl = LANES // cw
    n_sg = g // gl
    sw = gl * n_state
    lam = lax.complex(a_re, a_im)
    lam_dt = lam * jnp.exp(log_dt)[:, None]
    b_bar = ((jnp.exp(lam_dt) - 1.0) / lam)[..., None] * lax.complex(b_re, b_im)
    steps = jnp.arange(chunk + 1, dtype=F32)[:, None, None]
    mag = jnp.exp(lam_dt.real.reshape(n_sg, sw)[None] * steps)
    ang = lam_dt.imag.reshape(n_sg, sw)[None] * steps
    pw_re, pw_im = mag * jnp.cos(ang), mag * jnp.sin(ang)

    same_group = (jnp.arange(LANES)[:, None] // cw) == (jnp.arange(sw)[None, :] // n_state)

    def by_channel(x):
        return jnp.where(same_group, jnp.tile(x.reshape(n_sg, LANES, n_state), (1, 1, gl)), 0.0)

    bt = b_bar.transpose(0, 2, 1)
    b_re_d, b_im_d = by_channel(bt.real), by_channel(bt.imag)
    c_re_d, c_im_d = by_channel(c_re), by_channel(c_im)

    def times_pw(x_re, x_im, k):
        return x_re * pw_re[k][:, None] - x_im * pw_im[k][:, None], x_re * pw_im[k][:, None] + x_im * pw_re[k][:, None]

    bp = [times_pw(b_re_d, b_im_d, k) for k in range(chunk)]
    lag = [jnp.einsum("Saq,Sbq->Sab", r, c_re_d, precision=hp) - jnp.einsum("Saq,Sbq->Sab", i, c_im_d, precision=hp)
           for r, i in bp]
    zero = jnp.zeros_like(lag[0])
    kmat = jnp.concatenate(
        [jnp.concatenate([lag[t - s] if t >= s else zero for t in range(chunk)], axis=2) for s in range(chunk)], axis=1)
    bend = jnp.concatenate([jnp.concatenate(bp[chunk - 1 - s], axis=-1) for s in range(chunk)], axis=1)
    cp = [times_pw(c_re_d, c_im_d, t + 1) for t in range(chunk)]
    cpow = jnp.concatenate([jnp.concatenate([r.transpose(0, 2, 1) for r, _ in cp], axis=2),
                            jnp.concatenate([-i.transpose(0, 2, 1) for _, i in cp], axis=2)], axis=1)
    return {"kmat": kmat.astype(BF16), "bend": bend.astype(BF16), "cpow": cpow.astype(BF16),
            "lre": pw_re[chunk].reshape(n_sg, 1, sw), "lim": pw_im[chunk].reshape(n_sg, 1, sw),
            "dsk": d_skip.reshape(n_sg, 1, LANES)}


def _pack_state(h_re, h_im):
    b, g, p = h_re.shape
    gl = LANES // SSM_GROUP_WIDTH
    f = lambda h: h.reshape(b, g // gl, gl * p).transpose(1, 0, 2)
    return jnp.concatenate([f(h_re), f(h_im)], axis=-1)


def _unpack_state(h, n_state):
    n_sg, b, w = h.shape
    f = lambda x: x.transpose(1, 0, 2).reshape(b, -1, n_state)
    return f(h[..., :w // 2]), f(h[..., w // 2:])


def _mixer_out_kernel(x_ref, gm_ref, z_ref, o_ref, sga_ref, sgb_ref, wglu_ref, wuv_ref, wout_ref, y_ref, mixed_scr,
                      *, n_heads, v_dim):
    d = x_ref.shape[-1]
    glu = jnp.dot(z_ref[...], wglu_ref[...], preferred_element_type=F32)
    y_ssm = glu[:, :d] * _sigmoid(glu[:, d:])
    mixed_scr[...] = (sgb_ref[...] * y_ssm).astype(mixed_scr.dtype)
    for hd in range(n_heads):
        cols = slice(hd * v_dim, (hd + 1) * v_dim)
        y_attn = jnp.dot(o_ref[hd], wuv_ref[hd], preferred_element_type=F32)
        mixed_scr[:, cols] = mixed_scr[:, cols] + sga_ref[:, cols] * y_attn
    proj = jnp.dot(mixed_scr[...].astype(BF16), wout_ref[...], preferred_element_type=F32)
    y_ref[...] = x_ref[...] + gm_ref[...] * proj


def _mixer_out(x, mod, z, o_lat, sga, sgb, w, n_batch, tm=512):
    m, d = x.shape
    rows = m // n_batch
    tm = min(tm, rows)
    tiles = rows // tm
    _, nh, _, kv_rank = o_lat.shape
    v_dim = w["wuv"].shape[-1]
    row = lambda width: pl.BlockSpec((tm, width), lambda i: (i, 0))
    return pl.pallas_call(
        functools.partial(_mixer_out_kernel, n_heads=nh, v_dim=v_dim),
        out_shape=jax.ShapeDtypeStruct((m, d), F32),
        grid=(m // tm,),
        in_specs=[row(d), mod.spec(5, tm, d), row(d),
                  pl.BlockSpec((None, nh, tm, kv_rank), lambda i: (i // tiles, 0, i % tiles, 0)),
                  row(d), row(d), _resident(w["wglu"].shape), _resident(w["wuv"].shape), _resident(w["wout"].shape)],
        out_specs=row(d),
        scratch_shapes=[pltpu.VMEM((tm, d), F32)],
        compiler_params=_cparams("arbitrary"),
        name="mixer_out",
    )(x, mod.arr, z, o_lat, sga, sgb, w["wglu"], w["wuv"], w["wout"])


def _layer(x, mod, n_batch, cs_table, lw, dims, attend, transposed, ssm_seqs, ssm_chunk, h0, final_w):
    m, d = x.shape
    x = _ffn(x, mod, 0, lw["norm_ffn1"], *lw["ffn1"])
    *qkv, ckv, kr, u, sga, sgb = _mixer_in(x, mod, lw["norm_mix"], cs_table, lw["mix_in"], dims, n_batch, transposed)
    o_lat = attend(*qkv)
    z, h_n = _ssm(u.reshape(n_batch, m // n_batch, u.shape[-1]), lw["ssm"][ssm_chunk], h0, ssm_chunk, ssm_seqs)
    x = _mixer_out(x, mod, z.reshape(m, z.shape[-1]), o_lat, sga, sgb, lw["mix_out"], n_batch)
    x = _ffn(x, mod, 6, lw["norm_ffn2"], *lw["ffn2"], final_w=final_w)
    return x, ckv, kr, h_n


def kernel(x_prompt, x_sample, c_prompt, c_sample, cache_ckv, cache_kr, state_ssm_re, state_ssm_im, page_table, ada_w, ada_b, norm_ffn1, ffn1_up, ffn1_down, norm_mix, w_in, q_norm, w_uq, kv_norm, w_uk, w_uv, ssm_a_re, ssm_a_im, ssm_log_dt, ssm_b_re, ssm_b_im, ssm_c_re, ssm_c_im, ssm_d, w_glu, w_out, norm_ffn2, ffn2_up, ffn2_down, final_norm):
    bp, tp, d = x_prompt.shape
    bs, ts, _ = x_sample.shape
    depth = ada_w.shape[0]
    q_rank, kv_rank = q_norm.shape[-1], kv_norm.shape[-1]
    rope_dim = cache_kr.shape[-1]
    nh, nope_dim = w_uk.shape[2], w_uk.shape[3]
    n_groups, n_state = ssm_a_re.shape[1], ssm_a_re.shape[2]
    dims = {"n_heads": nh, "q_rank": q_rank, "kv_rank": kv_rank, "rope_dim": rope_dim, "nope_dim": nope_dim,
            "d_ssm": ssm_d.shape[-1]}
    past_len = page_table.shape[1] * cache_ckv.shape[2]
    prompt_chunk = 8 if tp % 8 == 0 else tp
    sample_chunk = ts
    cache_krt = jnp.swapaxes(cache_kr, 2, 3)

    mod_all = _adaln(jnp.concatenate([c_prompt, c_sample], axis=0), ada_w, ada_b)
    cs_p = _rope_tables(jnp.arange(tp), rope_dim)
    cs_s = jnp.tile(_rope_tables(past_len + jnp.arange(ts), rope_dim), (bs, 1))

    xp = x_prompt.reshape(bp * tp, d)
    xs = x_sample.reshape(bs * ts, d)
    zeros_state = jnp.zeros((bp, n_groups, n_state), F32)
    outs = {k: [] for k in ("ckv_p", "kr_p", "hre_p", "him_p", "ckv_s", "kr_s", "hre_s", "him_s")}
    for l in range(depth):
        ssm_args = (ssm_a_re[l], ssm_a_im[l], ssm_log_dt[l], ssm_b_re[l], ssm_b_im[l], ssm_c_re[l], ssm_c_im[l], ssm_d[l])
        lw = {"norm_ffn1": norm_ffn1[l], "norm_mix": norm_mix[l], "norm_ffn2": norm_ffn2[l],
              "ffn1": _chunk_ffn_weights(ffn1_up[l], ffn1_down[l]),
              "ffn2": _chunk_ffn_weights(ffn2_up[l], ffn2_down[l]),
              "mix_in": _mixer_in_weights(w_in[l], q_norm[l], w_uq[l], kv_norm[l], w_uk[l], dims),
              "mix_out": {"wglu": w_glu[l].astype(BF16), "wuv": w_uv[l].transpose(1, 0, 2).astype(BF16),
                          "wout": w_out[l].astype(BF16)},
              "ssm": {c: _ssm_params(*ssm_args, c) for c in {prompt_chunk, sample_chunk}}}
        last = final_norm if l == depth - 1 else None

        mod_p = _Mod(mod_all[l, :bp].reshape(bp, N_MODULATIONS, 1, d), False, tp)
        xp, ckv, kr, h_n = _layer(
            xp, mod_p, bp, cs_p, lw, dims,
            lambda qt, kc, vt: _attn_prompt(qt, kc.reshape(bp, tp, kc.shape[-1]), vt), True, bp, prompt_chunk,
            _pack_state(zeros_state, zeros_state), last)
        h_re, h_im = _unpack_state(h_n, n_state)
        outs["ckv_p"].append(ckv.reshape(bp, tp, kv_rank)); outs["kr_p"].append(kr.reshape(bp, tp, rope_dim))
        outs["hre_p"].append(h_re); outs["him_p"].append(h_im)

        mod_rows = jnp.repeat(mod_all[l, bp:].reshape(bs, N_MODULATIONS, d), ts, axis=0)
        mod_s = _Mod(mod_rows.transpose(1, 0, 2), True, ts)

        def attend_sample(q, kc, l=l):
            kw = q.shape[-1]
            qb = q.reshape(nh, bs, ts, kw).transpose(1, 0, 2, 3).reshape(bs, nh * ts, kw)
            k_new = jnp.pad(kc.reshape(bs, ts, kw), ((0, 0), (0, 16 - ts), (0, 0)))
            o = _attn_sample(qb, k_new, cache_ckv, cache_krt, page_table, l, kv_rank, rope_dim, ts)
            return o.reshape(bs, nh, ts, kv_rank).transpose(1, 0, 2, 3).reshape(1, nh, bs * ts, kv_rank)

        xs, ckv, kr, h_n = _layer(
            xs, mod_s, 1, cs_s, lw, dims,
            attend_sample, False, bs, sample_chunk, _pack_state(state_ssm_re[l], state_ssm_im[l]), last)
        h_re, h_im = _unpack_state(h_n, n_state)
        outs["ckv_s"].append(ckv.reshape(bs, ts, kv_rank)); outs["kr_s"].append(kr.reshape(bs, ts, rope_dim))
        outs["hre_s"].append(h_re); outs["him_s"].append(h_im)

    st = lambda k: jnp.stack(outs[k])
    return (xp.reshape(bp, tp, d), xs.reshape(bs, ts, d),
            st("ckv_p"), st("kr_p"), st("hre_p"), st("him_p"),
            st("ckv_s"), st("kr_s"), st("hre_s"), st("him_s"))
```

```python
import functools
import math

import jax
import jax.numpy as jnp
from jax import lax
from jax.experimental import pallas as pl
from jax.experimental.pallas import tpu as pltpu

F32 = jnp.float32
BF16 = jnp.bfloat16

NORM_EPS = 1e-6
ROPE_BASE = 10000.0
SSM_GROUP_WIDTH = 16
N_MODULATIONS = 9
LANES = 128
VMEM_LIMIT_BYTES = 56 * 1024 * 1024
FFN_CHUNK = 256


def _cparams(*sem):
    return pltpu.CompilerParams(dimension_semantics=sem, vmem_limit_bytes=VMEM_LIMIT_BYTES)


def _sigmoid(x):
    return 1.0 / (1.0 + jnp.exp(-x))


def _rms(x, w):
    return x * lax.rsqrt(jnp.mean(x * x, axis=-1, keepdims=True) + NORM_EPS) * w


def _of_layer(arr, layer):
    zeros = (0,) * (arr.ndim - 1)
    return pl.BlockSpec((None,) + arr.shape[1:], lambda *_: (layer,) + zeros, pipeline_mode=pl.Buffered(1))


def _adaln_kernel(c_ref, w_ref, b_ref, o_ref):
    c = c_ref[...]
    a = (c * _sigmoid(c)).astype(BF16)
    o_ref[...] = jnp.dot(a, w_ref[...].astype(BF16), preferred_element_type=F32) + b_ref[...]


def _adaln(c_all, ada_w, ada_b, tn=1536):
    depth, d, n = ada_w.shape
    rows = c_all.shape[0]
    return pl.pallas_call(
        _adaln_kernel,
        out_shape=jax.ShapeDtypeStruct((depth, rows, n), F32),
        grid=(depth, n // tn),
        in_specs=[pl.BlockSpec((rows, d), lambda l, j: (0, 0)),
                  pl.BlockSpec((None, d, tn), lambda l, j: (l, 0, j)),
                  pl.BlockSpec((None, 1, tn), lambda l, j: (l, 0, j))],
        out_specs=pl.BlockSpec((None, rows, tn), lambda l, j: (l, 0, j)),
        compiler_params=_cparams("arbitrary", "arbitrary"),
        name="adaln",
    )(c_all, ada_w, ada_b.reshape(depth, 1, n))


class _Mod:
    def __init__(self, arr, layer, per_token, rows_per_batch):
        self.arr = arr
        self.layer = layer
        self.per_token = per_token
        self.rows_per_batch = rows_per_batch

    def spec(self, k, tm, d):
        layer = self.layer
        if self.per_token:
            return pl.BlockSpec((None, None, tm, d), lambda i: (layer, k, i, 0))
        tiles_per_batch = self.rows_per_batch // tm
        return pl.BlockSpec((None, None, None, 1, d), lambda i: (layer, i // tiles_per_batch, k, 0, 0))


def _ffn_kernel(x_ref, sh_ref, sc_ref, g_ref, nw_ref, wup_ref, wdn_ref, *rest, final):
    if final:
        fn_ref, o_ref, h_scr, acc_scr = rest
    else:
        o_ref, h_scr, acc_scr = rest
    d_ff = wdn_ref.shape[0]
    x = x_ref[...]
    h_scr[...] = (_rms(x, nw_ref[...]) * (1.0 + sc_ref[...]) + sh_ref[...]).astype(BF16)

    def down(c):
        cols = slice(c * FFN_CHUNK, (c + 1) * FFN_CHUNK)
        gate_cols = slice(d_ff + c * FFN_CHUNK, d_ff + (c + 1) * FFN_CHUNK)
        h = h_scr[...]
        a = jnp.dot(h, wup_ref[:, cols], preferred_element_type=F32)
        b = jnp.dot(h, wup_ref[:, gate_cols], preferred_element_type=F32)
        act = (a * _sigmoid(a) * b).astype(BF16)
        return jnp.dot(act, wdn_ref[cols, :], preferred_element_type=F32)

    n_chunks = d_ff // FFN_CHUNK
    acc_scr[...] = down(0)
    for c in range(1, n_chunks - 1):
        acc_scr[...] += down(c)
    y = x + (0.5 * g_ref[...]) * (acc_scr[...] + down(n_chunks - 1))
    if final:
        y = _rms(y, fn_ref[...])
    o_ref[...] = y


def _ffn(x, mod, k0, layer, norm_w, w_up, w_down, final_w=None, tm=512):
    m, d = x.shape
    tm = min(tm, m)
    d_ff = w_down.shape[1]
    assert d_ff % FFN_CHUNK == 0
    final = final_w is not None
    in_specs = [pl.BlockSpec((tm, d), lambda i: (i, 0)),
                mod.spec(k0, tm, d), mod.spec(k0 + 1, tm, d), mod.spec(k0 + 2, tm, d),
                _of_layer(norm_w, layer), _of_layer(w_up, layer), _of_layer(w_down, layer)]
    args = [x, mod.arr, mod.arr, mod.arr, norm_w, w_up, w_down]
    if final:
        in_specs.append(_of_layer(final_w, 0))
        args.append(final_w)
    return pl.pallas_call(
        functools.partial(_ffn_kernel, final=final),
        out_shape=jax.ShapeDtypeStruct((m, d), F32),
        grid=(m // tm,),
        in_specs=in_specs,
        out_specs=pl.BlockSpec((tm, d), lambda i: (i, 0)),
        scratch_shapes=[pltpu.VMEM((tm, d), BF16), pltpu.VMEM((tm, d), F32)],
        compiler_params=_cparams("arbitrary"),
        name="ffn",
    )(*args)


def _mixer_in_kernel(x_ref, sh_ref, sc_ref, nw_ref, cs_ref, wa_ref, wu_ref, qn_ref, kvn_ref, wuq_ref, wuk_ref,
                     q_ref, kc_ref, *rest,
                     n_heads, q_rank, kv_rank, rope_dim, nope_dim, d_ssm, scale, transposed):
    if transposed:
        vt_ref, ckv_ref, kr_ref, u_ref, sga_ref, sgb_ref = rest
    else:
        ckv_ref, kr_ref, u_ref, sga_ref, sgb_ref = rest
    x = x_ref[...]
    h = (_rms(x, nw_ref[...]) * (1.0 + sc_ref[...]) + sh_ref[...]).astype(BF16)
    cos = cs_ref[:, :LANES]
    sin = cs_ref[:, LANES:]

    t = jnp.dot(h, wa_ref[...], preferred_element_type=F32)
    cq = t[:, :q_rank]
    kr = t[:, q_rank:q_rank + LANES] * cos + t[:, q_rank + LANES:q_rank + 2 * LANES] * sin
    ckv = _rms(t[:, q_rank + 2 * LANES:], kvn_ref[...])
    kr_ref[...] = kr[:, :rope_dim]
    ckv_ref[...] = ckv
    kc_ref[:, :kv_rank] = ckv.astype(BF16)
    kc_ref[:, kv_rank:] = kr.astype(BF16)
    if transposed:
        vt_ref[...] = ckv.T.astype(BF16)

    ug = jnp.dot(h, wu_ref[...], preferred_element_type=F32)
    u_ref[...] = ug[:, :d_ssm]
    d_model = sga_ref.shape[-1]
    sga_ref[...] = _sigmoid(ug[:, d_ssm:d_ssm + d_model])
    sgb_ref[...] = _sigmoid(ug[:, d_ssm + d_model:])

    cqn = _rms(cq, qn_ref[...]).astype(BF16)
    q = jnp.dot(cqn, wuq_ref[...], preferred_element_type=F32)
    base_a = n_heads * nope_dim
    base_b = base_a + n_heads * LANES
    for hd in range(n_heads):
        nope = q[:, hd * nope_dim:(hd + 1) * nope_dim].astype(BF16)
        q_lat = jnp.dot(nope, wuk_ref[hd], preferred_element_type=F32) * scale
        q_rot = (q[:, base_a + hd * LANES:base_a + (hd + 1) * LANES] * cos
                 + q[:, base_b + hd * LANES:base_b + (hd + 1) * LANES] * sin) * scale
        if transposed:
            q_ref[hd, :kv_rank, :] = q_lat.T.astype(BF16)
            q_ref[hd, kv_rank:, :] = q_rot.T.astype(BF16)
        else:
            q_ref[hd, :, :kv_rank] = q_lat.astype(BF16)
            q_ref[hd, :, kv_rank:] = q_rot.astype(BF16)


def _mixer_in(x, mod, layer, norm_w, cs_table, w, dims, n_batch, transposed, tm=256):
    m, d = x.shape
    tm = min(tm, m // n_batch)
    rows = m // n_batch
    tiles = rows // tm
    cs_tiles = cs_table.shape[0] // tm
    nh, q_rank, kv_rank, rope_dim, nope_dim, d_ssm = (dims[k] for k in
                                                       ("n_heads", "q_rank", "kv_rank", "rope_dim", "nope_dim", "d_ssm"))
    kw = kv_rank + LANES
    kern = functools.partial(_mixer_in_kernel, n_heads=nh, q_rank=q_rank, kv_rank=kv_rank, rope_dim=rope_dim,
                             nope_dim=nope_dim, d_ssm=d_ssm, scale=(nope_dim + rope_dim) ** -0.5, transposed=transposed)
    row = lambda width: pl.BlockSpec((tm, width), lambda i: (i, 0))
    out_shape = [jax.ShapeDtypeStruct((n_batch, nh, kw, rows) if transposed else (n_batch, nh, rows, kw), BF16),
                 jax.ShapeDtypeStruct((m, kw), BF16),
                 jax.ShapeDtypeStruct((m, kv_rank), F32),
                 jax.ShapeDtypeStruct((m, rope_dim), F32),
                 jax.ShapeDtypeStruct((m, d_ssm), F32),
                 jax.ShapeDtypeStruct((m, d), F32),
                 jax.ShapeDtypeStruct((m, d), F32)]
    out_specs = [pl.BlockSpec((None, nh, kw, tm), lambda i: (i // tiles, 0, 0, i % tiles)) if transposed else
                 pl.BlockSpec((None, nh, tm, kw), lambda i: (i // tiles, 0, i % tiles, 0)),
                 row(kw), row(kv_rank), row(rope_dim), row(d_ssm), row(d), row(d)]
    if transposed:
        out_shape.insert(2, jax.ShapeDtypeStruct((n_batch, kv_rank, rows), BF16))
        out_specs.insert(2, pl.BlockSpec((None, kv_rank, tm), lambda i: (i // tiles, 0, i % tiles)))
    weights = [norm_w, w["wa"], w["wu"], w["qn"], w["kvn"], w["wuq"], w["wuk"]]
    lspec = [_of_layer(a, layer) for a in weights]
    return pl.pallas_call(
        kern,
        out_shape=out_shape,
        grid=(m // tm,),
        in_specs=[row(d), mod.spec(3, tm, d), mod.spec(4, tm, d), lspec[0],
                  pl.BlockSpec((tm, 2 * LANES), lambda i: (i % cs_tiles, 0))] + lspec[1:],
        out_specs=out_specs,
        compiler_params=_cparams("arbitrary"),
        name="mixer_in",
    )(x, mod.arr, mod.arr, norm_w, cs_table, *weights[1:])


def _rope_tables(pos, rope_dim):
    half = rope_dim // 2
    inv = ROPE_BASE ** (-jnp.arange(half, dtype=F32) / half)
    ang = pos.astype(F32)[:, None] * inv[None, :]
    cos, sin = jnp.cos(ang), jnp.sin(ang)
    pad = jnp.zeros((pos.shape[0], LANES - rope_dim), F32)
    return jnp.concatenate([cos, cos, pad, -sin, sin, pad], axis=-1)


def _swap_halves(w, rope_dim):
    half = rope_dim // 2
    return jnp.concatenate([w[..., half:], w[..., :half]], axis=-1)


def _pad_lanes(w):
    return jnp.pad(w, [(0, 0)] * (w.ndim - 1) + [(0, LANES - w.shape[-1])])


def _mixer_in_weights(w_in, q_norm, w_uq, kv_norm, w_uk, dims):
    nh, q_rank, kv_rank, rope_dim, nope_dim, d_ssm = (dims[k] for k in
                                                       ("n_heads", "q_rank", "kv_rank", "rope_dim", "nope_dim", "d_ssm"))
    depth = w_in.shape[0]
    o1, o2 = q_rank + kv_rank, q_rank + kv_rank + rope_dim
    w_cq, w_ckv, w_kr, w_rest = w_in[..., :q_rank], w_in[..., q_rank:o1], w_in[..., o1:o2], w_in[..., o2:]
    wa = jnp.concatenate([w_cq, _pad_lanes(w_kr), _pad_lanes(_swap_halves(w_kr, rope_dim)), w_ckv], axis=-1)
    wq = w_uq.reshape(depth, q_rank, nh, nope_dim + rope_dim)
    wq_nope = wq[..., :nope_dim].reshape(depth, q_rank, nh * nope_dim)
    wq_rope = wq[..., nope_dim:]
    wq_a = _pad_lanes(wq_rope).reshape(depth, q_rank, nh * LANES)
    wq_b = _pad_lanes(_swap_halves(wq_rope, rope_dim)).reshape(depth, q_rank, nh * LANES)
    return {"wa": wa.astype(BF16), "wu": w_rest.astype(BF16),
            "qn": q_norm.reshape(depth, 1, q_rank), "kvn": kv_norm.reshape(depth, 1, kv_rank),
            "wuq": jnp.concatenate([wq_nope, wq_a, wq_b], axis=-1).astype(BF16),
            "wuk": w_uk.transpose(0, 2, 3, 1).astype(BF16)}


_NT = (((1,), (1,)), ((), ()))


def _attn_prompt_kernel(qt_ref, k_ref, vt_ref, o_ref, m_scr, l_scr, acc_scr, s_scr, *, tq, n_heads):
    qi = pl.program_id(1)
    m_scr[...] = jnp.full_like(m_scr, -jnp.inf)
    l_scr[...] = jnp.zeros_like(l_scr)
    acc_scr[...] = jnp.zeros_like(acc_scr)

    def step(j, masked):
        keys = pl.ds(pl.multiple_of(j * tq, tq), tq)
        k = k_ref[keys, :]
        vt = vt_ref[:, keys]
        if masked:
            keep = (lax.broadcasted_iota(jnp.int32, (tq, tq), 0) <= lax.broadcasted_iota(jnp.int32, (tq, tq), 1))
        for hd in range(n_heads):
            s_scr[hd] = jnp.dot(k, qt_ref[hd], preferred_element_type=F32)
        for hd in range(n_heads):
            s = s_scr[hd]
            if masked:
                s = jnp.where(keep, s, -jnp.inf)
            m_prev = m_scr[hd]
            m_new = jnp.maximum(m_prev, jnp.max(s, axis=0, keepdims=True))
            alpha = jnp.exp(m_prev - m_new)
            p = jnp.exp(s - m_new)
            l_scr[hd] = alpha * l_scr[hd] + jnp.sum(p, axis=0, keepdims=True)
            acc_scr[hd] = alpha * acc_scr[hd] + jnp.dot(vt, p.astype(BF16), preferred_element_type=F32)
            m_scr[hd] = m_new

    def body(j, carry):
        step(j, False)
        return carry

    lax.fori_loop(0, qi, body, 0)
    step(qi, True)
    for hd in range(n_heads):
        o_ref[hd] = (acc_scr[hd] / l_scr[hd]).T.astype(BF16)


def _attn_prompt(qt, kc, vt, tq=256):
    b, nh, kw, t = qt.shape
    kv_rank = vt.shape[1]
    tq = min(tq, t)
    assert t % tq == 0 and tq % LANES == 0 and kv_rank % LANES == 0
    return pl.pallas_call(
        functools.partial(_attn_prompt_kernel, tq=tq, n_heads=nh),
        out_shape=jax.ShapeDtypeStruct((b, nh, t, kv_rank), BF16),
        grid=(b, t // tq),
        in_specs=[pl.BlockSpec((None, nh, kw, tq), lambda bi, qi: (bi, 0, 0, qi)),
                  pl.BlockSpec((None, t, kw), lambda bi, qi: (bi, 0, 0)),
                  pl.BlockSpec((None, kv_rank, t), lambda bi, qi: (bi, 0, 0))],
        out_specs=pl.BlockSpec((None, nh, tq, kv_rank), lambda bi, qi: (bi, 0, qi, 0)),
        scratch_shapes=[pltpu.VMEM((nh, 1, tq), F32), pltpu.VMEM((nh, 1, tq), F32),
                        pltpu.VMEM((nh, kv_rank, tq), F32), pltpu.VMEM((nh, tq, tq), F32)],
        compiler_params=_cparams("arbitrary", "arbitrary"),
        name="attn_prompt",
    )(qt, kc, vt)


def _attn_sample_kernel(pt_ref, q_ref, knew_ref, ckv_hbm, krt_hbm, o_ref, kbuf, krbuf, sem,
                        *, layer, n_pages, page, kv_rank, rope_dim, new_len, n_req):
    b = pl.program_id(0)
    slot = lax.rem(b, 2)

    def page_copies(req, slot_, p):
        pid = pt_ref[req, p]
        off = pl.multiple_of(p * page, page)
        return (pltpu.make_async_copy(ckv_hbm.at[layer, pid], kbuf.at[slot_, pl.ds(off, page), :], sem.at[0, slot_]),
                pltpu.make_async_copy(krt_hbm.at[layer, pid], krbuf.at[slot_, :, pl.ds(off, page)], sem.at[1, slot_]))

    def start_all(req, slot_):
        def body(p, carry):
            for cp in page_copies(req, slot_, p):
                cp.start()
            return carry
        lax.fori_loop(0, n_pages, body, 0, unroll=8)

    def wait_all(req, slot_):
        def body(p, carry):
            for cp in page_copies(req, slot_, p):
                cp.wait()
            return carry
        lax.fori_loop(0, n_pages, body, 0, unroll=8)

    @pl.when(b == 0)
    def _():
        start_all(0, 0)

    @pl.when(b + 1 < n_req)
    def _():
        start_all(b + 1, 1 - slot)

    wait_all(b, slot)

    q = q_ref[...]
    rows = q.shape[0]
    kb = kbuf[slot].astype(BF16)
    s = (lax.dot_general(q[:, :kv_rank], kb, _NT, preferred_element_type=F32)
         + jnp.dot(q[:, kv_rank:kv_rank + rope_dim], krbuf[slot].astype(BF16), preferred_element_type=F32))
    kn = knew_ref[...]
    npad = kn.shape[0]
    sn = lax.dot_general(q, kn, _NT, preferred_element_type=F32)
    q_pos = jnp.bitwise_and(lax.broadcasted_iota(jnp.int32, (rows, npad), 0), new_len - 1)
    k_pos = lax.broadcasted_iota(jnp.int32, (rows, npad), 1)
    sn = jnp.where(k_pos <= q_pos, sn, -jnp.inf)
    m = jnp.maximum(jnp.max(s, axis=1, keepdims=True), jnp.max(sn, axis=1, keepdims=True))
    p = jnp.exp(s - m)
    pn = jnp.exp(sn - m)
    denom = jnp.sum(p, axis=1, keepdims=True) + jnp.sum(pn, axis=1, keepdims=True)
    o = (jnp.dot(p.astype(BF16), kb, preferred_element_type=F32)
         + jnp.dot(pn.astype(BF16), kn[:, :kv_rank], preferred_element_type=F32))
    o_ref[...] = (o / denom).astype(BF16)


def _attn_sample(q, k_new, cache_ckv, cache_krt, page_table, layer, kv_rank, rope_dim, new_len):
    bs, rows, kw = q.shape
    n_pages = page_table.shape[1]
    page = cache_ckv.shape[2]
    past = n_pages * page
    new_pad = k_new.shape[1]
    assert new_len & (new_len - 1) == 0 and page % LANES == 0
    kern = functools.partial(_attn_sample_kernel, layer=layer, n_pages=n_pages, page=page, kv_rank=kv_rank,
                             rope_dim=rope_dim, new_len=new_len, n_req=bs)
    return pl.pallas_call(
        kern,
        out_shape=jax.ShapeDtypeStruct((bs, rows, kv_rank), BF16),
        grid_spec=pltpu.PrefetchScalarGridSpec(
            num_scalar_prefetch=1,
            grid=(bs,),
            in_specs=[pl.BlockSpec((None, rows, kw), lambda b, pt: (b, 0, 0)),
                      pl.BlockSpec((None, new_pad, kw), lambda b, pt: (b, 0, 0)),
                      pl.BlockSpec(memory_space=pl.ANY),
                      pl.BlockSpec(memory_space=pl.ANY)],
            out_specs=pl.BlockSpec((None, rows, kv_rank), lambda b, pt: (b, 0, 0)),
            scratch_shapes=[pltpu.VMEM((2, past, kv_rank), F32),
                            pltpu.VMEM((2, rope_dim, past), F32),
                            pltpu.SemaphoreType.DMA((2, 2))]),
        compiler_params=_cparams("arbitrary"),
        name="attn_sample",
    )(page_table, q, k_new, cache_ckv, cache_krt)


def _gelu_tanh(y):
    return 0.5 * y * (1.0 + jnp.tanh(math.sqrt(2.0 / math.pi) * (y + 0.044715 * (y * y * y))))


def _ssm_kernel(u_ref, kmat_ref, bend_ref, cpow_ref, lre_ref, lim_ref, dsk_ref, h0_ref, z_ref, hn_ref,
                h_scr, s_scr, y_scr, *, chunk, n_seq, n_chunks):
    nb, tt, lanes = u_ref.shape
    rpb = tt // chunk
    half = lre_ref.shape[-1]

    @pl.when(pl.program_id(1) == 0)
    def _():
        h_scr[...] = h0_ref[...]

    u_rows = jnp.concatenate(
        [jnp.concatenate([u_ref[b, pl.ds(s, rpb, stride=chunk), :] for s in range(chunk)], axis=1)
         for b in range(nb)], axis=0)
    ub = u_rows.astype(BF16)
    own = jnp.dot(ub, bend_ref[...], preferred_element_type=F32)
    nt = half // lanes
    tile = lambda x, j: x[:, j * lanes:(j + 1) * lanes]

    def seq_rows(q):
        return pl.ds(q, n_chunks, stride=n_seq)

    if n_chunks > 1:
        for q in range(n_seq):
            for j in range(2 * nt):
                s_scr[j, seq_rows(q), :] = tile(own[q * n_chunks:(q + 1) * n_chunks], j)
    else:
        for j in range(2 * nt):
            s_scr[j] = tile(own, j)
    lre = [tile(lre_ref[...], j) for j in range(nt)]
    lim = [tile(lim_ref[...], j) for j in range(nt)]
    h = h_scr[...]
    h_re = [tile(h, j) for j in range(nt)]
    h_im = [tile(h, nt + j) for j in range(nt)]
    for n in range(n_chunks):
        rows = pl.ds(n * n_seq, n_seq)
        for j in range(nt):
            s_re, s_im = s_scr[j, rows, :], s_scr[nt + j, rows, :]
            s_scr[j, rows, :] = h_re[j]
            s_scr[nt + j, rows, :] = h_im[j]
            h_re[j], h_im[j] = (lre[j] * h_re[j] - lim[j] * h_im[j] + s_re,
                                lre[j] * h_im[j] + lim[j] * h_re[j] + s_im)
    h_scr[...] = jnp.concatenate(h_re + h_im, axis=1)
    hn_ref[...] = h_scr[...]
    if n_chunks > 1:
        entering = jnp.concatenate(
            [jnp.concatenate([s_scr[j, seq_rows(q), :] for j in range(2 * nt)], axis=1) for q in range(n_seq)], axis=0)
    else:
        entering = jnp.concatenate([s_scr[j] for j in range(2 * nt)], axis=1)
    y = (jnp.dot(ub, kmat_ref[...], preferred_element_type=F32)
         + jnp.dot(entering.astype(BF16), cpow_ref[...], preferred_element_type=F32))
    for b in range(nb):
        for t in range(chunk):
            y_scr[b, pl.ds(t, rpb, stride=chunk), :] = y[b * rpb:(b + 1) * rpb, t * lanes:(t + 1) * lanes]
    z_ref[...] = _gelu_tanh(y_scr[...] + dsk_ref[...] * u_ref[...]).astype(BF16)


def _ssm(u3, p, layer, h0, chunk, n_seq, tt=512):
    nb, t, d = u3.shape
    n_sg = d // LANES
    tt = min(tt, t)
    rows = nb * tt // chunk
    n_chunks = rows // n_seq
    n_state = h0.shape[-1]
    per_sg = lambda arr: pl.BlockSpec((None, None) + arr.shape[2:], lambda sg, i: (layer, sg, 0, 0))
    tok = pl.BlockSpec((nb, tt, LANES), lambda sg, i: (0, i, sg))
    state = pl.BlockSpec((None, n_seq, n_state), lambda sg, i: (sg, 0, 0))
    ops = [p[k] for k in ("kmat", "bend", "cpow", "lre", "lim", "dsk")]
    return pl.pallas_call(
        functools.partial(_ssm_kernel, chunk=chunk, n_seq=n_seq, n_chunks=n_chunks),
        out_shape=[jax.ShapeDtypeStruct((nb, t, d), BF16), jax.ShapeDtypeStruct(h0.shape, F32)],
        grid=(n_sg, t // tt),
        in_specs=[tok] + [per_sg(a) for a in ops] + [state],
        out_specs=[tok, state],
        scratch_shapes=[pltpu.VMEM((n_seq, n_state), F32), pltpu.VMEM((n_state // LANES, rows, LANES), F32),
                        pltpu.VMEM((nb, tt, LANES), F32)],
        compiler_params=_cparams("arbitrary", "arbitrary"),
        name="ssm",
    )(u3, *ops, h0)


def _ssm_params(a_re, a_im, log_dt, b_re, b_im, c_re, c_im, d_skip, chunk):
    hp = lax.Precision.HIGHEST
    g, n_state = a_re.shape
    cw = SSM_GROUP_WIDTH
    gl = LANES // cw
    n_sg = g // gl
    sw = gl * n_state
    lam = lax.complex(a_re, a_im)
    lam_dt = lam * jnp.exp(log_dt)[:, None]
    b_bar = ((jnp.exp(lam_dt) - 1.0) / lam)[..., None] * lax.complex(b_re, b_im)
    steps = jnp.arange(chunk + 1, dtype=F32)[:, None, None]
    mag = jnp.exp(lam_dt.real.reshape(n_sg, sw)[None] * steps)
    ang = lam_dt.imag.reshape(n_sg, sw)[None] * steps
    pw_re, pw_im = mag * jnp.cos(ang), mag * jnp.sin(ang)

    same_group = (jnp.arange(LANES)[:, None] // cw) == (jnp.arange(sw)[None, :] // n_state)

    def by_channel(x):
        return jnp.where(same_group, jnp.tile(x.reshape(n_sg, LANES, n_state), (1, 1, gl)), 0.0)

    def times_pw(x_re, x_im, p_re, p_im):
        p_re, p_im = p_re[:, :, None], p_im[:, :, None]
        return x_re * p_re - x_im * p_im, x_re * p_im + x_im * p_re

    bt = b_bar.transpose(0, 2, 1)
    c_re_d, c_im_d = by_channel(c_re), by_channel(c_im)
    bp_re, bp_im = times_pw(by_channel(bt.real), by_channel(bt.imag), pw_re[:chunk], pw_im[:chunk])
    lag = (jnp.einsum("kSaq,Sbq->kSab", bp_re, c_re_d, precision=hp)
           - jnp.einsum("kSaq,Sbq->kSab", bp_im, c_im_d, precision=hp))
    t_idx = jnp.arange(chunk)
    diff = t_idx[None, :] - t_idx[:, None]
    kmat = jnp.where((diff >= 0)[:, :, None, None, None], lag[jnp.clip(diff, 0, chunk - 1)], 0.0)
    kmat = kmat.astype(BF16).transpose(2, 0, 3, 1, 4).reshape(n_sg, chunk * LANES, chunk * LANES)
    bend = jnp.concatenate([bp_re[::-1], bp_im[::-1]], axis=-1).astype(BF16)
    bend = bend.transpose(1, 0, 2, 3).reshape(n_sg, chunk * LANES, 2 * sw)
    cp_re, cp_im = times_pw(c_re_d, c_im_d, pw_re[1:], pw_im[1:])
    cpow = jnp.concatenate([cp_re, -cp_im], axis=-1).astype(BF16)
    cpow = cpow.transpose(1, 3, 0, 2).reshape(n_sg, 2 * sw, chunk * LANES)
    return {"kmat": kmat, "bend": bend, "cpow": cpow,
            "lre": pw_re[chunk].reshape(n_sg, 1, sw), "lim": pw_im[chunk].reshape(n_sg, 1, sw),
            "dsk": d_skip.reshape(n_sg, 1, LANES)}


def _pack_state(h_re, h_im):
    b, g, p = h_re.shape
    gl = LANES // SSM_GROUP_WIDTH
    f = lambda h: h.reshape(b, g // gl, gl * p).transpose(1, 0, 2)
    return jnp.concatenate([f(h_re), f(h_im)], axis=-1)


def _unpack_state(h, n_state):
    n_sg, b, w = h.shape
    f = lambda x: x.transpose(1, 0, 2).reshape(b, -1, n_state)
    return f(h[..., :w // 2]), f(h[..., w // 2:])


def _mixer_out_kernel(x_ref, gm_ref, z_ref, o_ref, sga_ref, sgb_ref, wglu_ref, wuv_ref, wout_ref, y_ref, mixed_scr,
                      *, n_heads, v_dim):
    d = x_ref.shape[-1]
    glu = jnp.dot(z_ref[...], wglu_ref[...], preferred_element_type=F32)
    y_ssm = glu[:, :d] * _sigmoid(glu[:, d:])
    mixed_scr[...] = (sgb_ref[...] * y_ssm).astype(mixed_scr.dtype)
    for hd in range(n_heads):
        cols = slice(hd * v_dim, (hd + 1) * v_dim)
        y_attn = jnp.dot(o_ref[hd], wuv_ref[hd], preferred_element_type=F32)
        mixed_scr[:, cols] = mixed_scr[:, cols] + sga_ref[:, cols] * y_attn
    proj = jnp.dot(mixed_scr[...].astype(BF16), wout_ref[...], preferred_element_type=F32)
    y_ref[...] = x_ref[...] + gm_ref[...] * proj


def _mixer_out(x, mod, layer, z, o_lat, sga, sgb, w, n_batch, tm=512):
    m, d = x.shape
    rows = m // n_batch
    tm = min(tm, rows)
    tiles = rows // tm
    _, nh, _, kv_rank = o_lat.shape
    v_dim = w["wuv"].shape[-1]
    row = lambda width: pl.BlockSpec((tm, width), lambda i: (i, 0))
    weights = [w["wglu"], w["wuv"], w["wout"]]
    return pl.pallas_call(
        functools.partial(_mixer_out_kernel, n_heads=nh, v_dim=v_dim),
        out_shape=jax.ShapeDtypeStruct((m, d), F32),
        grid=(m // tm,),
        in_specs=[row(d), mod.spec(5, tm, d), row(d),
                  pl.BlockSpec((None, nh, tm, kv_rank), lambda i: (i // tiles, 0, i % tiles, 0)),
                  row(d), row(d)] + [_of_layer(a, layer) for a in weights],
        out_specs=row(d),
        scratch_shapes=[pltpu.VMEM((tm, d), F32)],
        compiler_params=_cparams("arbitrary"),
        name="mixer_out",
    )(x, mod.arr, z, o_lat, sga, sgb, *weights)


def _layer(x, mod, layer, n_batch, cs_table, w, dims, attend, transposed, ssm_seqs, ssm_chunk, h0, final_w):
    m, d = x.shape
    x = _ffn(x, mod, 0, layer, w["norm_ffn1"], w["ffn1_up"], w["ffn1_down"])
    *qkv, ckv, kr, u, sga, sgb = _mixer_in(x, mod, layer, w["norm_mix"], cs_table, w["mix_in"], dims, n_batch, transposed)
    o_lat = attend(*qkv)
    z, h_n = _ssm(u.reshape(n_batch, m // n_batch, u.shape[-1]), w["ssm"][ssm_chunk], layer, h0, ssm_chunk, ssm_seqs)
    x = _mixer_out(x, mod, layer, z.reshape(m, z.shape[-1]), o_lat, sga, sgb, w["mix_out"], n_batch)
    x = _ffn(x, mod, 6, layer, w["norm_ffn2"], w["ffn2_up"], w["ffn2_down"], final_w=final_w)
    return x, ckv, kr, h_n


def kernel(x_prompt, x_sample, c_prompt, c_sample, cache_ckv, cache_kr, state_ssm_re, state_ssm_im, page_table, ada_w, ada_b, norm_ffn1, ffn1_up, ffn1_down, norm_mix, w_in, q_norm, w_uq, kv_norm, w_uk, w_uv, ssm_a_re, ssm_a_im, ssm_log_dt, ssm_b_re, ssm_b_im, ssm_c_re, ssm_c_im, ssm_d, w_glu, w_out, norm_ffn2, ffn2_up, ffn2_down, final_norm):
    bp, tp, d = x_prompt.shape
    bs, ts, _ = x_sample.shape
    depth = ada_w.shape[0]
    q_rank, kv_rank = q_norm.shape[-1], kv_norm.shape[-1]
    rope_dim = cache_kr.shape[-1]
    nh, nope_dim = w_uk.shape[2], w_uk.shape[3]
    n_groups, n_state = ssm_a_re.shape[1], ssm_a_re.shape[2]
    dims = {"n_heads": nh, "q_rank": q_rank, "kv_rank": kv_rank, "rope_dim": rope_dim, "nope_dim": nope_dim,
            "d_ssm": ssm_d.shape[-1]}
    past_len = page_table.shape[1] * cache_ckv.shape[2]
    prompt_chunk = 8 if tp % 8 == 0 else tp
    sample_chunk = ts
    cache_krt = jnp.swapaxes(cache_kr, 2, 3)

    ssm_args = (ssm_a_re, ssm_a_im, ssm_log_dt, ssm_b_re, ssm_b_im, ssm_c_re, ssm_c_im, ssm_d)
    w = {"norm_ffn1": norm_ffn1.reshape(depth, 1, d), "norm_mix": norm_mix.reshape(depth, 1, d),
         "norm_ffn2": norm_ffn2.reshape(depth, 1, d),
         "ffn1_up": ffn1_up.astype(BF16), "ffn1_down": ffn1_down.astype(BF16),
         "ffn2_up": ffn2_up.astype(BF16), "ffn2_down": ffn2_down.astype(BF16),
         "mix_in": _mixer_in_weights(w_in, q_norm, w_uq, kv_norm, w_uk, dims),
         "mix_out": {"wglu": w_glu.astype(BF16), "wuv": w_uv.transpose(0, 2, 1, 3).astype(BF16),
                     "wout": w_out.astype(BF16)},
         "ssm": {c: jax.vmap(functools.partial(_ssm_params, chunk=c))(*ssm_args) for c in {prompt_chunk, sample_chunk}}}
    final_w = final_norm.reshape(1, 1, d)

    mod_all = _adaln(jnp.concatenate([c_prompt, c_sample], axis=0), ada_w, ada_b)
    mod_p_arr = mod_all[:, :bp].reshape(depth, bp, N_MODULATIONS, 1, d)
    mod_s_arr = jnp.repeat(mod_all[:, bp:].reshape(depth, bs, N_MODULATIONS, d), ts, axis=1).transpose(0, 2, 1, 3)
    cs_p = _rope_tables(jnp.arange(tp), rope_dim)
    cs_s = jnp.tile(_rope_tables(past_len + jnp.arange(ts), rope_dim), (bs, 1))

    xp = x_prompt.reshape(bp * tp, d)
    xs = x_sample.reshape(bs * ts, d)
    zeros_state = jnp.zeros((bp, n_groups, n_state), F32)
    h0_p = _pack_state(zeros_state, zeros_state)
    outs = {k: [] for k in ("ckv_p", "kr_p", "hre_p", "him_p", "ckv_s", "kr_s", "hre_s", "him_s")}
    for l in range(depth):
        last = final_w if l == depth - 1 else None

        xp, ckv, kr, h_n = _layer(
            xp, _Mod(mod_p_arr, l, False, tp), l, bp, cs_p, w, dims,
            lambda qt, kc, vt: _attn_prompt(qt, kc.reshape(bp, tp, kc.shape[-1]), vt), True, bp, prompt_chunk,
            h0_p, last)
        h_re, h_im = _unpack_state(h_n, n_state)
        outs["ckv_p"].append(ckv.reshape(bp, tp, kv_rank)); outs["kr_p"].append(kr.reshape(bp, tp, rope_dim))
        outs["hre_p"].append(h_re); outs["him_p"].append(h_im)

        def attend_sample(q, kc, l=l):
            kw = q.shape[-1]
            qb = q.reshape(nh, bs, ts, kw).transpose(1, 0, 2, 3).reshape(bs, nh * ts, kw)
            k_new = jnp.pad(kc.reshape(bs, ts, kw), ((0, 0), (0, 16 - ts), (0, 0)))
            o = _attn_sample(qb, k_new, cache_ckv, cache_krt, page_table, l, kv_rank, rope_dim, ts)
            return o.reshape(bs, nh, ts, kv_rank).transpose(1, 0, 2, 3).reshape(1, nh, bs * ts, kv_rank)

        xs, ckv, kr, h_n = _layer(
            xs, _Mod(mod_s_arr, l, True, ts), l, 1, cs_s, w, dims,
            attend_sample, False, bs, sample_chunk, _pack_state(state_ssm_re[l], state_ssm_im[l]), last)
        h_re, h_im = _unpack_state(h_n, n_state)
        outs["ckv_s"].append(ckv.reshape(bs, ts, kv_rank)); outs["kr_s"].append(kr.reshape(bs, ts, rope_dim))
        outs["hre_s"].append(h_re); outs["him_s"].append(h_im)

    st = lambda k: jnp.stack(outs[k])
    return (xp.reshape(bp, tp, d), xs.reshape(bs, ts, d),
            st("ckv_p"), st("kr_p"), st("hre_p"), st("him_p"),
            st("ckv_s"), st("kr_s"), st("hre_s"), st("him_s"))
```

```python
import functools
import math

import jax
import jax.numpy as jnp
from jax import lax
from jax.experimental import pallas as pl
from jax.experimental.pallas import tpu as pltpu

F32 = jnp.float32
BF16 = jnp.bfloat16

NORM_EPS = 1e-6
ROPE_BASE = 10000.0
SSM_GROUP_WIDTH = 16
N_MODULATIONS = 9
LANES = 128
VMEM_LIMIT_BYTES = 56 * 1024 * 1024
FFN_CHUNK = 256


def _cparams(*sem):
    return pltpu.CompilerParams(dimension_semantics=sem, vmem_limit_bytes=VMEM_LIMIT_BYTES)


def _sigmoid(x):
    return 1.0 / (1.0 + jnp.exp(-x))


def _rms(x, w):
    return x * lax.rsqrt(jnp.mean(x * x, axis=-1, keepdims=True) + NORM_EPS) * w


def _of_layer(arr, layer):
    zeros = (0,) * (arr.ndim - 1)
    return pl.BlockSpec((None,) + arr.shape[1:], lambda *_: (layer,) + zeros, pipeline_mode=pl.Buffered(1))


def _adaln_kernel(c_ref, w_ref, b_ref, o_ref):
    c = c_ref[...]
    a = (c * _sigmoid(c)).astype(BF16)
    o_ref[...] = jnp.dot(a, w_ref[...].astype(BF16), preferred_element_type=F32) + b_ref[...]


def _adaln(c_all, ada_w, ada_b, tn=1536):
    depth, d, n = ada_w.shape
    rows = c_all.shape[0]
    return pl.pallas_call(
        _adaln_kernel,
        out_shape=jax.ShapeDtypeStruct((depth, rows, n), F32),
        grid=(depth, n // tn),
        in_specs=[pl.BlockSpec((rows, d), lambda l, j: (0, 0)),
                  pl.BlockSpec((None, d, tn), lambda l, j: (l, 0, j)),
                  pl.BlockSpec((None, 1, tn), lambda l, j: (l, 0, j))],
        out_specs=pl.BlockSpec((None, rows, tn), lambda l, j: (l, 0, j)),
        compiler_params=_cparams("arbitrary", "arbitrary"),
        name="adaln",
    )(c_all, ada_w, ada_b.reshape(depth, 1, n))


class _Mod:
    def __init__(self, arr, layer, per_token, rows_per_batch):
        self.arr = arr
        self.layer = layer
        self.per_token = per_token
        self.rows_per_batch = rows_per_batch

    def spec(self, k, tm, d):
        layer = self.layer
        if self.per_token:
            return pl.BlockSpec((None, None, tm, d), lambda i: (layer, k, i, 0))
        tiles_per_batch = self.rows_per_batch // tm
        return pl.BlockSpec((None, None, None, 1, d), lambda i: (layer, i // tiles_per_batch, k, 0, 0))


def _ffn_kernel(x_ref, sh_ref, sc_ref, g_ref, nw_ref, wup_ref, wdn_ref, *rest, final):
    if final:
        fn_ref, o_ref, h_scr, acc_scr = rest
    else:
        o_ref, h_scr, acc_scr = rest
    d_ff = wdn_ref.shape[0]
    x = x_ref[...]
    h_scr[...] = (_rms(x, nw_ref[...]) * (1.0 + sc_ref[...]) + sh_ref[...]).astype(BF16)

    def down(c):
        cols = slice(c * FFN_CHUNK, (c + 1) * FFN_CHUNK)
        gate_cols = slice(d_ff + c * FFN_CHUNK, d_ff + (c + 1) * FFN_CHUNK)
        h = h_scr[...]
        a = jnp.dot(h, wup_ref[:, cols], preferred_element_type=F32)
        b = jnp.dot(h, wup_ref[:, gate_cols], preferred_element_type=F32)
        act = (a * _sigmoid(a) * b).astype(BF16)
        return jnp.dot(act, wdn_ref[cols, :], preferred_element_type=F32)

    n_chunks = d_ff // FFN_CHUNK
    acc_scr[...] = down(0)
    for c in range(1, n_chunks - 1):
        acc_scr[...] += down(c)
    y = x + (0.5 * g_ref[...]) * (acc_scr[...] + down(n_chunks - 1))
    if final:
        y = _rms(y, fn_ref[...])
    o_ref[...] = y


def _ffn(x, mod, k0, layer, norm_w, w_up, w_down, final_w=None, tm=512):
    m, d = x.shape
    tm = min(tm, m)
    d_ff = w_down.shape[1]
    assert d_ff % FFN_CHUNK == 0
    final = final_w is not None
    in_specs = [pl.BlockSpec((tm, d), lambda i: (i, 0)),
                mod.spec(k0, tm, d), mod.spec(k0 + 1, tm, d), mod.spec(k0 + 2, tm, d),
                _of_layer(norm_w, layer), _of_layer(w_up, layer), _of_layer(w_down, layer)]
    args = [x, mod.arr, mod.arr, mod.arr, norm_w, w_up, w_down]
    if final:
        in_specs.append(_of_layer(final_w, 0))
        args.append(final_w)
    return pl.pallas_call(
        functools.partial(_ffn_kernel, final=final),
        out_shape=jax.ShapeDtypeStruct((m, d), F32),
        grid=(m // tm,),
        in_specs=in_specs,
        out_specs=pl.BlockSpec((tm, d), lambda i: (i, 0)),
        scratch_shapes=[pltpu.VMEM((tm, d), BF16), pltpu.VMEM((tm, d), F32)],
        compiler_params=_cparams("arbitrary"),
        name="ffn",
    )(*args)


def _mixer_in_kernel(x_ref, sh_ref, sc_ref, nw_ref, cs_ref, wa_ref, wu_ref, qn_ref, kvn_ref, wuq_ref, wuk_ref,
                     q_ref, kc_ref, *rest,
                     n_heads, q_rank, kv_rank, rope_dim, nope_dim, d_ssm, scale, transposed):
    if transposed:
        vt_ref, ckv_ref, kr_ref, u_ref, sga_ref, sgb_ref = rest
    else:
        ckv_ref, kr_ref, u_ref, sga_ref, sgb_ref = rest
    x = x_ref[...]
    h = (_rms(x, nw_ref[...]) * (1.0 + sc_ref[...]) + sh_ref[...]).astype(BF16)
    cos = cs_ref[:, :LANES]
    sin = cs_ref[:, LANES:]

    t = jnp.dot(h, wa_ref[...], preferred_element_type=F32)
    cq = t[:, :q_rank]
    kr = t[:, q_rank:q_rank + LANES] * cos + t[:, q_rank + LANES:q_rank + 2 * LANES] * sin
    ckv = _rms(t[:, q_rank + 2 * LANES:], kvn_ref[...])
    kr_ref[...] = kr[:, :rope_dim]
    ckv_ref[...] = ckv
    kc_ref[:, :kv_rank] = ckv.astype(BF16)
    kc_ref[:, kv_rank:] = kr.astype(BF16)
    if transposed:
        vt_ref[...] = ckv.T.astype(BF16)

    ug = jnp.dot(h, wu_ref[...], preferred_element_type=F32)
    u_ref[...] = ug[:, :d_ssm]
    d_model = sga_ref.shape[-1]
    sga_ref[...] = _sigmoid(ug[:, d_ssm:d_ssm + d_model])
    sgb_ref[...] = _sigmoid(ug[:, d_ssm + d_model:])

    cqn = _rms(cq, qn_ref[...]).astype(BF16)
    q = jnp.dot(cqn, wuq_ref[...], preferred_element_type=F32)
    base_a = n_heads * nope_dim
    base_b = base_a + n_heads * LANES
    for hd in range(n_heads):
        nope = q[:, hd * nope_dim:(hd + 1) * nope_dim].astype(BF16)
        q_lat = jnp.dot(nope, wuk_ref[hd], preferred_element_type=F32) * scale
        q_rot = (q[:, base_a + hd * LANES:base_a + (hd + 1) * LANES] * cos
                 + q[:, base_b + hd * LANES:base_b + (hd + 1) * LANES] * sin) * scale
        if transposed:
            q_ref[hd, :kv_rank, :] = q_lat.T.astype(BF16)
            q_ref[hd, kv_rank:, :] = q_rot.T.astype(BF16)
        else:
            q_ref[hd, :, :kv_rank] = q_lat.astype(BF16)
            q_ref[hd, :, kv_rank:] = q_rot.astype(BF16)


def _mixer_in(x, mod, layer, norm_w, cs_table, w, dims, n_batch, transposed, tm=256):
    m, d = x.shape
    tm = min(tm, m // n_batch)
    rows = m // n_batch
    tiles = rows // tm
    cs_tiles = cs_table.shape[0] // tm
    nh, q_rank, kv_rank, rope_dim, nope_dim, d_ssm = (dims[k] for k in
                                                       ("n_heads", "q_rank", "kv_rank", "rope_dim", "nope_dim", "d_ssm"))
    kw = kv_rank + LANES
    kern = functools.partial(_mixer_in_kernel, n_heads=nh, q_rank=q_rank, kv_rank=kv_rank, rope_dim=rope_dim,
                             nope_dim=nope_dim, d_ssm=d_ssm, scale=(nope_dim + rope_dim) ** -0.5, transposed=transposed)
    row = lambda width: pl.BlockSpec((tm, width), lambda i: (i, 0))
    out_shape = [jax.ShapeDtypeStruct((n_batch, nh, kw, rows) if transposed else (n_batch, nh, rows, kw), BF16),
                 jax.ShapeDtypeStruct((m, kw), BF16),
                 jax.ShapeDtypeStruct((m, kv_rank), F32),
                 jax.ShapeDtypeStruct((m, rope_dim), F32),
                 jax.ShapeDtypeStruct((m, d_ssm), F32),
                 jax.ShapeDtypeStruct((m, d), F32),
                 jax.ShapeDtypeStruct((m, d), F32)]
    out_specs = [pl.BlockSpec((None, nh, kw, tm), lambda i: (i // tiles, 0, 0, i % tiles)) if transposed else
                 pl.BlockSpec((None, nh, tm, kw), lambda i: (i // tiles, 0, i % tiles, 0)),
                 row(kw), row(kv_rank), row(rope_dim), row(d_ssm), row(d), row(d)]
    if transposed:
        out_shape.insert(2, jax.ShapeDtypeStruct((n_batch, kv_rank, rows), BF16))
        out_specs.insert(2, pl.BlockSpec((None, kv_rank, tm), lambda i: (i // tiles, 0, i % tiles)))
    weights = [norm_w, w["wa"], w["wu"], w["qn"], w["kvn"], w["wuq"], w["wuk"]]
    lspec = [_of_layer(a, layer) for a in weights]
    return pl.pallas_call(
        kern,
        out_shape=out_shape,
        grid=(m // tm,),
        in_specs=[row(d), mod.spec(3, tm, d), mod.spec(4, tm, d), lspec[0],
                  pl.BlockSpec((tm, 2 * LANES), lambda i: (i % cs_tiles, 0))] + lspec[1:],
        out_specs=out_specs,
        compiler_params=_cparams("arbitrary"),
        name="mixer_in",
    )(x, mod.arr, mod.arr, norm_w, cs_table, *weights[1:])


def _rope_tables(pos, rope_dim):
    half = rope_dim // 2
    inv = ROPE_BASE ** (-jnp.arange(half, dtype=F32) / half)
    ang = pos.astype(F32)[:, None] * inv[None, :]
    cos, sin = jnp.cos(ang), jnp.sin(ang)
    pad = jnp.zeros((pos.shape[0], LANES - rope_dim), F32)
    return jnp.concatenate([cos, cos, pad, -sin, sin, pad], axis=-1)


def _swap_halves(w, rope_dim):
    half = rope_dim // 2
    return jnp.concatenate([w[..., half:], w[..., :half]], axis=-1)


def _pad_lanes(w):
    return jnp.pad(w, [(0, 0)] * (w.ndim - 1) + [(0, LANES - w.shape[-1])])


def _mixer_in_weights(w_in, q_norm, w_uq, kv_norm, w_uk, dims):
    nh, q_rank, kv_rank, rope_dim, nope_dim, d_ssm = (dims[k] for k in
                                                       ("n_heads", "q_rank", "kv_rank", "rope_dim", "nope_dim", "d_ssm"))
    depth = w_in.shape[0]
    o1, o2 = q_rank + kv_rank, q_rank + kv_rank + rope_dim
    w_cq, w_ckv, w_kr, w_rest = w_in[..., :q_rank], w_in[..., q_rank:o1], w_in[..., o1:o2], w_in[..., o2:]
    wa = jnp.concatenate([w_cq, _pad_lanes(w_kr), _pad_lanes(_swap_halves(w_kr, rope_dim)), w_ckv], axis=-1)
    wq = w_uq.reshape(depth, q_rank, nh, nope_dim + rope_dim)
    wq_nope = wq[..., :nope_dim].reshape(depth, q_rank, nh * nope_dim)
    wq_rope = wq[..., nope_dim:]
    wq_a = _pad_lanes(wq_rope).reshape(depth, q_rank, nh * LANES)
    wq_b = _pad_lanes(_swap_halves(wq_rope, rope_dim)).reshape(depth, q_rank, nh * LANES)
    return {"wa": wa.astype(BF16), "wu": w_rest.astype(BF16),
            "qn": q_norm.reshape(depth, 1, q_rank), "kvn": kv_norm.reshape(depth, 1, kv_rank),
            "wuq": jnp.concatenate([wq_nope, wq_a, wq_b], axis=-1).astype(BF16),
            "wuk": w_uk.transpose(0, 2, 3, 1).astype(BF16)}


_NT = (((1,), (1,)), ((), ()))


def _attn_prompt_kernel(qt_ref, k_ref, vt_ref, o_ref, m_scr, l_scr, acc_scr, s_scr, *, tq, n_heads):
    qi = pl.program_id(1)

    def step(j, diagonal):
        keys = pl.ds(pl.multiple_of(j * tq, tq), tq)
        k = k_ref[keys, :]
        vt = vt_ref[:, keys]
        if diagonal:
            keep = (lax.broadcasted_iota(jnp.int32, (tq, tq), 0) <= lax.broadcasted_iota(jnp.int32, (tq, tq), 1))
        for hd in range(n_heads):
            s_scr[hd] = jnp.dot(k, qt_ref[hd], preferred_element_type=F32)
        for hd in range(n_heads):
            s = s_scr[hd]
            if diagonal:
                s = jnp.where(keep, s, -jnp.inf)
                m_new = jnp.max(s, axis=0, keepdims=True)
                p = jnp.exp(s - m_new)
                l_scr[hd] = jnp.sum(p, axis=0, keepdims=True)
                acc_scr[hd] = jnp.dot(vt, p.astype(BF16), preferred_element_type=F32)
            else:
                m_prev = m_scr[hd]
                m_new = jnp.maximum(m_prev, jnp.max(s, axis=0, keepdims=True))
                alpha = jnp.exp(m_prev - m_new)
                p = jnp.exp(s - m_new)
                l_scr[hd] = alpha * l_scr[hd] + jnp.sum(p, axis=0, keepdims=True)
                acc_scr[hd] = alpha * acc_scr[hd] + jnp.dot(vt, p.astype(BF16), preferred_element_type=F32)
            m_scr[hd] = m_new

    def body(j, carry):
        step(j, False)
        return carry

    step(qi, True)
    lax.fori_loop(0, qi, body, 0)
    for hd in range(n_heads):
        o_ref[hd] = (acc_scr[hd] / l_scr[hd]).T.astype(BF16)


def _attn_prompt(qt, kc, vt, tq=256):
    b, nh, kw, t = qt.shape
    kv_rank = vt.shape[1]
    tq = min(tq, t)
    assert t % tq == 0 and tq % LANES == 0 and kv_rank % LANES == 0
    return pl.pallas_call(
        functools.partial(_attn_prompt_kernel, tq=tq, n_heads=nh),
        out_shape=jax.ShapeDtypeStruct((b, nh, t, kv_rank), BF16),
        grid=(b, t // tq),
        in_specs=[pl.BlockSpec((None, nh, kw, tq), lambda bi, qi: (bi, 0, 0, qi)),
                  pl.BlockSpec((None, t, kw), lambda bi, qi: (bi, 0, 0)),
                  pl.BlockSpec((None, kv_rank, t), lambda bi, qi: (bi, 0, 0))],
        out_specs=pl.BlockSpec((None, nh, tq, kv_rank), lambda bi, qi: (bi, 0, qi, 0)),
        scratch_shapes=[pltpu.VMEM((nh, 1, tq), F32), pltpu.VMEM((nh, 1, tq), F32),
                        pltpu.VMEM((nh, kv_rank, tq), F32), pltpu.VMEM((nh, tq, tq), F32)],
        compiler_params=_cparams("arbitrary", "arbitrary"),
        name="attn_prompt",
    )(qt, kc, vt)


def _attn_sample_kernel(pt_ref, q_ref, knew_ref, ckv_hbm, krt_hbm, o_ref, kbuf, krbuf, sem,
                        *, layer, n_pages, page, kv_rank, rope_dim, new_len, n_req):
    b = pl.program_id(0)
    slot = lax.rem(b, 2)

    def page_copies(req, slot_, p):
        pid = pt_ref[req, p]
        off = pl.multiple_of(p * page, page)
        return (pltpu.make_async_copy(ckv_hbm.at[layer, pid], kbuf.at[slot_, pl.ds(off, page), :], sem.at[0, slot_]),
                pltpu.make_async_copy(krt_hbm.at[layer, pid], krbuf.at[slot_, :, pl.ds(off, page)], sem.at[1, slot_]))

    def start_all(req, slot_):
        def body(p, carry):
            for cp in page_copies(req, slot_, p):
                cp.start()
            return carry
        lax.fori_loop(0, n_pages, body, 0, unroll=8)

    def wait_all(req, slot_):
        def body(p, carry):
            for cp in page_copies(req, slot_, p):
                cp.wait()
            return carry
        lax.fori_loop(0, n_pages, body, 0, unroll=8)

    @pl.when(b == 0)
    def _():
        start_all(0, 0)

    @pl.when(b + 1 < n_req)
    def _():
        start_all(b + 1, 1 - slot)

    wait_all(b, slot)

    q = q_ref[...]
    rows = q.shape[0]
    kb = kbuf[slot].astype(BF16)
    s = (lax.dot_general(q[:, :kv_rank], kb, _NT, preferred_element_type=F32)
         + jnp.dot(q[:, kv_rank:kv_rank + rope_dim], krbuf[slot].astype(BF16), preferred_element_type=F32))
    kn = knew_ref[...]
    npad = kn.shape[0]
    sn = lax.dot_general(q, kn, _NT, preferred_element_type=F32)
    q_pos = jnp.bitwise_and(lax.broadcasted_iota(jnp.int32, (rows, npad), 0), new_len - 1)
    k_pos = lax.broadcasted_iota(jnp.int32, (rows, npad), 1)
    sn = jnp.where(k_pos <= q_pos, sn, -jnp.inf)
    m = jnp.maximum(jnp.max(s, axis=1, keepdims=True), jnp.max(sn, axis=1, keepdims=True))
    p = jnp.exp(s - m)
    pn = jnp.exp(sn - m)
    denom = jnp.sum(p, axis=1, keepdims=True) + jnp.sum(pn, axis=1, keepdims=True)
    o = (jnp.dot(p.astype(BF16), kb, preferred_element_type=F32)
         + jnp.dot(pn.astype(BF16), kn[:, :kv_rank], preferred_element_type=F32))
    o_ref[...] = (o / denom).astype(BF16)


def _attn_sample(q, k_new, cache_ckv, cache_krt, page_table, layer, kv_rank, rope_dim, new_len):
    bs, rows, kw = q.shape
    n_pages = page_table.shape[1]
    page = cache_ckv.shape[2]
    past = n_pages * page
    new_pad = k_new.shape[1]
    assert new_len & (new_len - 1) == 0 and page % LANES == 0
    kern = functools.partial(_attn_sample_kernel, layer=layer, n_pages=n_pages, page=page, kv_rank=kv_rank,
                             rope_dim=rope_dim, new_len=new_len, n_req=bs)
    return pl.pallas_call(
        kern,
        out_shape=jax.ShapeDtypeStruct((bs, rows, kv_rank), BF16),
        grid_spec=pltpu.PrefetchScalarGridSpec(
            num_scalar_prefetch=1,
            grid=(bs,),
            in_specs=[pl.BlockSpec((None, rows, kw), lambda b, pt: (b, 0, 0)),
                      pl.BlockSpec((None, new_pad, kw), lambda b, pt: (b, 0, 0)),
                      pl.BlockSpec(memory_space=pl.ANY),
                      pl.BlockSpec(memory_space=pl.ANY)],
            out_specs=pl.BlockSpec((None, rows, kv_rank), lambda b, pt: (b, 0, 0)),
            scratch_shapes=[pltpu.VMEM((2, past, kv_rank), F32),
                            pltpu.VMEM((2, rope_dim, past), F32),
                            pltpu.SemaphoreType.DMA((2, 2))]),
        compiler_params=_cparams("arbitrary"),
        name="attn_sample",
    )(page_table, q, k_new, cache_ckv, cache_krt)


def _gelu_tanh(y):
    return 0.5 * y * (1.0 + jnp.tanh(math.sqrt(2.0 / math.pi) * (y + 0.044715 * (y * y * y))))


def _ssm_kernel(u_ref, kmat_ref, bend_ref, cpow_ref, lre_ref, lim_ref, dsk_ref, h0_ref, z_ref, hn_ref,
                h_scr, s_scr, y_scr, *, chunk, n_seq, n_chunks):
    nb, tt, lanes = u_ref.shape
    rpb = tt // chunk
    half = lre_ref.shape[-1]

    @pl.when(pl.program_id(1) == 0)
    def _():
        h_scr[...] = h0_ref[...]

    u_rows = jnp.concatenate(
        [jnp.concatenate([u_ref[b, pl.ds(s, rpb, stride=chunk), :] for s in range(chunk)], axis=1)
         for b in range(nb)], axis=0)
    ub = u_rows.astype(BF16)
    own = jnp.dot(ub, bend_ref[...], preferred_element_type=F32)
    nt = half // lanes
    tile = lambda x, j: x[:, j * lanes:(j + 1) * lanes]

    def seq_rows(q):
        return pl.ds(q, n_chunks, stride=n_seq)

    if n_chunks > 1:
        for q in range(n_seq):
            for j in range(2 * nt):
                s_scr[j, seq_rows(q), :] = tile(own[q * n_chunks:(q + 1) * n_chunks], j)
    else:
        for j in range(2 * nt):
            s_scr[j] = tile(own, j)
    lre = [tile(lre_ref[...], j) for j in range(nt)]
    lim = [tile(lim_ref[...], j) for j in range(nt)]
    h = h_scr[...]
    h_re = [tile(h, j) for j in range(nt)]
    h_im = [tile(h, nt + j) for j in range(nt)]
    for n in range(n_chunks):
        rows = pl.ds(n * n_seq, n_seq)
        for j in range(nt):
            s_re, s_im = s_scr[j, rows, :], s_scr[nt + j, rows, :]
            s_scr[j, rows, :] = h_re[j]
            s_scr[nt + j, rows, :] = h_im[j]
            h_re[j], h_im[j] = (lre[j] * h_re[j] - lim[j] * h_im[j] + s_re,
                                lre[j] * h_im[j] + lim[j] * h_re[j] + s_im)
    h_scr[...] = jnp.concatenate(h_re + h_im, axis=1)
    hn_ref[...] = h_scr[...]
    if n_chunks > 1:
        entering = jnp.concatenate(
            [jnp.concatenate([s_scr[j, seq_rows(q), :] for j in range(2 * nt)], axis=1) for q in range(n_seq)], axis=0)
    else:
        entering = jnp.concatenate([s_scr[j] for j in range(2 * nt)], axis=1)
    y = (jnp.dot(ub, kmat_ref[...], preferred_element_type=F32)
         + jnp.dot(entering.astype(BF16), cpow_ref[...], preferred_element_type=F32))
    for b in range(nb):
        for t in range(chunk):
            y_scr[b, pl.ds(t, rpb, stride=chunk), :] = y[b * rpb:(b + 1) * rpb, t * lanes:(t + 1) * lanes]
    z_ref[...] = _gelu_tanh(y_scr[...] + dsk_ref[...] * u_ref[...]).astype(BF16)


def _ssm(u3, p, layer, h0, chunk, n_seq, tt=512):
    nb, t, d = u3.shape
    n_sg = d // LANES
    tt = min(tt, t)
    rows = nb * tt // chunk
    n_chunks = rows // n_seq
    n_state = h0.shape[-1]
    per_sg = lambda arr: pl.BlockSpec((None, None) + arr.shape[2:], lambda sg, i: (layer, sg, 0, 0))
    tok = pl.BlockSpec((nb, tt, LANES), lambda sg, i: (0, i, sg))
    state = pl.BlockSpec((None, n_seq, n_state), lambda sg, i: (sg, 0, 0))
    ops = [p[k] for k in ("kmat", "bend", "cpow", "lre", "lim", "dsk")]
    return pl.pallas_call(
        functools.partial(_ssm_kernel, chunk=chunk, n_seq=n_seq, n_chunks=n_chunks),
        out_shape=[jax.ShapeDtypeStruct((nb, t, d), BF16), jax.ShapeDtypeStruct(h0.shape, F32)],
        grid=(n_sg, t // tt),
        in_specs=[tok] + [per_sg(a) for a in ops] + [state],
        out_specs=[tok, state],
        scratch_shapes=[pltpu.VMEM((n_seq, n_state), F32), pltpu.VMEM((n_state // LANES, rows, LANES), F32),
                        pltpu.VMEM((nb, tt, LANES), F32)],
        compiler_params=_cparams("arbitrary", "arbitrary"),
        name="ssm",
    )(u3, *ops, h0)


def _ssm_params(a_re, a_im, log_dt, b_re, b_im, c_re, c_im, d_skip, chunk):
    hp = lax.Precision.HIGHEST
    g, n_state = a_re.shape
    cw = SSM_GROUP_WIDTH
    gl = LANES // cw
    n_sg = g // gl
    sw = gl * n_state
    lam = lax.complex(a_re, a_im)
    lam_dt = lam * jnp.exp(log_dt)[:, None]
    b_bar = ((jnp.exp(lam_dt) - 1.0) / lam)[..., None] * lax.complex(b_re, b_im)
    steps = jnp.arange(chunk + 1, dtype=F32)[:, None, None]
    mag = jnp.exp(lam_dt.real.reshape(n_sg, sw)[None] * steps)
    ang = lam_dt.imag.reshape(n_sg, sw)[None] * steps
    pw_re, pw_im = mag * jnp.cos(ang), mag * jnp.sin(ang)

    same_group = (jnp.arange(LANES)[:, None] // cw) == (jnp.arange(sw)[None, :] // n_state)

    def by_channel(x):
        return jnp.where(same_group, jnp.tile(x.reshape(n_sg, LANES, n_state), (1, 1, gl)), 0.0)

    def times_pw(x_re, x_im, p_re, p_im):
        p_re, p_im = p_re[:, :, None], p_im[:, :, None]
        return x_re * p_re - x_im * p_im, x_re * p_im + x_im * p_re

    bt = b_bar.transpose(0, 2, 1)
    c_re_d, c_im_d = by_channel(c_re), by_channel(c_im)
    bp_re, bp_im = times_pw(by_channel(bt.real), by_channel(bt.imag), pw_re[:chunk], pw_im[:chunk])
    lag = (jnp.einsum("kSaq,Sbq->kSab", bp_re, c_re_d, precision=hp)
           - jnp.einsum("kSaq,Sbq->kSab", bp_im, c_im_d, precision=hp))
    t_idx = jnp.arange(chunk)
    diff = t_idx[None, :] - t_idx[:, None]
    kmat = jnp.where((diff >= 0)[:, :, None, None, None], lag[jnp.clip(diff, 0, chunk - 1)], 0.0)
    kmat = kmat.astype(BF16).transpose(2, 0, 3, 1, 4).reshape(n_sg, chunk * LANES, chunk * LANES)
    bend = jnp.concatenate([bp_re[::-1], bp_im[::-1]], axis=-1).astype(BF16)
    bend = bend.transpose(1, 0, 2, 3).reshape(n_sg, chunk * LANES, 2 * sw)
    cp_re, cp_im = times_pw(c_re_d, c_im_d, pw_re[1:], pw_im[1:])
    cpow = jnp.concatenate([cp_re, -cp_im], axis=-1).astype(BF16)
    cpow = cpow.transpose(1, 3, 0, 2).reshape(n_sg, 2 * sw, chunk * LANES)
    return {"kmat": kmat, "bend": bend, "cpow": cpow,
            "lre": pw_re[chunk].reshape(n_sg, 1, sw), "lim": pw_im[chunk].reshape(n_sg, 1, sw),
            "dsk": d_skip.reshape(n_sg, 1, LANES)}


def _pack_state(h_re, h_im):
    b, g, p = h_re.shape
    gl = LANES // SSM_GROUP_WIDTH
    f = lambda h: h.reshape(b, g // gl, gl * p).transpose(1, 0, 2)
    return jnp.concatenate([f(h_re), f(h_im)], axis=-1)


def _unpack_state(h, n_state):
    n_sg, b, w = h.shape
    f = lambda x: x.transpose(1, 0, 2).reshape(b, -1, n_state)
    return f(h[..., :w // 2]), f(h[..., w // 2:])


def _mixer_out_kernel(x_ref, gm_ref, z_ref, o_ref, sga_ref, sgb_ref, wglu_ref, wuv_ref, wout_ref, y_ref, mixed_scr,
                      *, n_heads, v_dim):
    d = x_ref.shape[-1]
    glu = jnp.dot(z_ref[...], wglu_ref[...], preferred_element_type=F32)
    y_ssm = glu[:, :d] * _sigmoid(glu[:, d:])
    mixed_scr[...] = (sgb_ref[...] * y_ssm).astype(mixed_scr.dtype)
    for hd in range(n_heads):
        cols = slice(hd * v_dim, (hd + 1) * v_dim)
        y_attn = jnp.dot(o_ref[hd], wuv_ref[hd], preferred_element_type=F32)
        mixed_scr[:, cols] = mixed_scr[:, cols] + sga_ref[:, cols] * y_attn
    proj = jnp.dot(mixed_scr[...].astype(BF16), wout_ref[...], preferred_element_type=F32)
    y_ref[...] = x_ref[...] + gm_ref[...] * proj


def _mixer_out(x, mod, layer, z, o_lat, sga, sgb, w, n_batch, tm=512):
    m, d = x.shape
    rows = m // n_batch
    tm = min(tm, rows)
    tiles = rows // tm
    _, nh, _, kv_rank = o_lat.shape
    v_dim = w["wuv"].shape[-1]
    row = lambda width: pl.BlockSpec((tm, width), lambda i: (i, 0))
    weights = [w["wglu"], w["wuv"], w["wout"]]
    return pl.pallas_call(
        functools.partial(_mixer_out_kernel, n_heads=nh, v_dim=v_dim),
        out_shape=jax.ShapeDtypeStruct((m, d), F32),
        grid=(m // tm,),
        in_specs=[row(d), mod.spec(5, tm, d), row(d),
                  pl.BlockSpec((None, nh, tm, kv_rank), lambda i: (i // tiles, 0, i % tiles, 0)),
                  row(d), row(d)] + [_of_layer(a, layer) for a in weights],
        out_specs=row(d),
        scratch_shapes=[pltpu.VMEM((tm, d), F32)],
        compiler_params=_cparams("arbitrary"),
        name="mixer_out",
    )(x, mod.arr, z, o_lat, sga, sgb, *weights)


def _layer(x, mod, layer, n_batch, cs_table, w, dims, attend, transposed, ssm_seqs, ssm_chunk, h0, final_w):
    m, d = x.shape
    x = _ffn(x, mod, 0, layer, w["norm_ffn1"], w["ffn1_up"], w["ffn1_down"])
    *qkv, ckv, kr, u, sga, sgb = _mixer_in(x, mod, layer, w["norm_mix"], cs_table, w["mix_in"], dims, n_batch, transposed)
    o_lat = attend(*qkv)
    z, h_n = _ssm(u.reshape(n_batch, m // n_batch, u.shape[-1]), w["ssm"][ssm_chunk], layer, h0, ssm_chunk, ssm_seqs)
    x = _mixer_out(x, mod, layer, z.reshape(m, z.shape[-1]), o_lat, sga, sgb, w["mix_out"], n_batch)
    x = _ffn(x, mod, 6, layer, w["norm_ffn2"], w["ffn2_up"], w["ffn2_down"], final_w=final_w)
    return x, ckv, kr, h_n


def kernel(x_prompt, x_sample, c_prompt, c_sample, cache_ckv, cache_kr, state_ssm_re, state_ssm_im, page_table, ada_w, ada_b, norm_ffn1, ffn1_up, ffn1_down, norm_mix, w_in, q_norm, w_uq, kv_norm, w_uk, w_uv, ssm_a_re, ssm_a_im, ssm_log_dt, ssm_b_re, ssm_b_im, ssm_c_re, ssm_c_im, ssm_d, w_glu, w_out, norm_ffn2, ffn2_up, ffn2_down, final_norm):
    bp, tp, d = x_prompt.shape
    bs, ts, _ = x_sample.shape
    depth = ada_w.shape[0]
    q_rank, kv_rank = q_norm.shape[-1], kv_norm.shape[-1]
    rope_dim = cache_kr.shape[-1]
    nh, nope_dim = w_uk.shape[2], w_uk.shape[3]
    n_groups, n_state = ssm_a_re.shape[1], ssm_a_re.shape[2]
    dims = {"n_heads": nh, "q_rank": q_rank, "kv_rank": kv_rank, "rope_dim": rope_dim, "nope_dim": nope_dim,
            "d_ssm": ssm_d.shape[-1]}
    past_len = page_table.shape[1] * cache_ckv.shape[2]
    sample_chunk = ts
    prompt_chunk = ts if tp % ts == 0 else tp
    cache_krt = jnp.swapaxes(cache_kr, 2, 3)

    ssm_args = (ssm_a_re, ssm_a_im, ssm_log_dt, ssm_b_re, ssm_b_im, ssm_c_re, ssm_c_im, ssm_d)
    w = {"norm_ffn1": norm_ffn1.reshape(depth, 1, d), "norm_mix": norm_mix.reshape(depth, 1, d),
         "norm_ffn2": norm_ffn2.reshape(depth, 1, d),
         "ffn1_up": ffn1_up.astype(BF16), "ffn1_down": ffn1_down.astype(BF16),
         "ffn2_up": ffn2_up.astype(BF16), "ffn2_down": ffn2_down.astype(BF16),
         "mix_in": _mixer_in_weights(w_in, q_norm, w_uq, kv_norm, w_uk, dims),
         "mix_out": {"wglu": w_glu.astype(BF16), "wuv": w_uv.transpose(0, 2, 1, 3).astype(BF16),
                     "wout": w_out.astype(BF16)},
         "ssm": {c: jax.vmap(functools.partial(_ssm_params, chunk=c))(*ssm_args) for c in {prompt_chunk, sample_chunk}}}
    final_w = final_norm.reshape(1, 1, d)

    mod_all = _adaln(jnp.concatenate([c_prompt, c_sample], axis=0), ada_w, ada_b)
    mod_p_arr = mod_all[:, :bp].reshape(depth, bp, N_MODULATIONS, 1, d)
    mod_s_arr = jnp.repeat(mod_all[:, bp:].reshape(depth, bs, N_MODULATIONS, d), ts, axis=1).transpose(0, 2, 1, 3)
    cs_p = _rope_tables(jnp.arange(tp), rope_dim)
    cs_s = jnp.tile(_rope_tables(past_len + jnp.arange(ts), rope_dim), (bs, 1))

    xp = x_prompt.reshape(bp * tp, d)
    xs = x_sample.reshape(bs * ts, d)
    zeros_state = jnp.zeros((bp, n_groups, n_state), F32)
    h0_p = _pack_state(zeros_state, zeros_state)
    outs = {k: [] for k in ("ckv_p", "kr_p", "hre_p", "him_p", "ckv_s", "kr_s", "hre_s", "him_s")}
    for l in range(depth):
        last = final_w if l == depth - 1 else None

        xp, ckv, kr, h_n = _layer(
            xp, _Mod(mod_p_arr, l, False, tp), l, bp, cs_p, w, dims,
            lambda qt, kc, vt: _attn_prompt(qt, kc.reshape(bp, tp, kc.shape[-1]), vt), True, bp, prompt_chunk,
            h0_p, last)
        h_re, h_im = _unpack_state(h_n, n_state)
        outs["ckv_p"].append(ckv.reshape(bp, tp, kv_rank)); outs["kr_p"].append(kr.reshape(bp, tp, rope_dim))
        outs["hre_p"].append(h_re); outs["him_p"].append(h_im)

        def attend_sample(q, kc, l=l):
            kw = q.shape[-1]
            qb = q.reshape(nh, bs, ts, kw).transpose(1, 0, 2, 3).reshape(bs, nh * ts, kw)
            k_new = jnp.pad(kc.reshape(bs, ts, kw), ((0, 0), (0, 16 - ts), (0, 0)))
            o = _attn_sample(qb, k_new, cache_ckv, cache_krt, page_table, l, kv_rank, rope_dim, ts)
            return o.reshape(bs, nh, ts, kv_rank).transpose(1, 0, 2, 3).reshape(1, nh, bs * ts, kv_rank)

        xs, ckv, kr, h_n = _layer(
            xs, _Mod(mod_s_arr, l, True, ts), l, 1, cs_s, w, dims,
            attend_sample, False, bs, sample_chunk, _pack_state(state_ssm_re[l], state_ssm_im[l]), last)
        h_re, h_im = _unpack_state(h_n, n_state)
        outs["ckv_s"].append(ckv.reshape(bs, ts, kv_rank)); outs["kr_s"].append(kr.reshape(bs, ts, rope_dim))
        outs["hre_s"].append(h_re); outs["him_s"].append(h_im)

    st = lambda k: jnp.stack(outs[k])
    return (xp.reshape(bp, tp, d), xs.reshape(bs, ts, d),
            st("ckv_p"), st("kr_p"), st("hre_p"), st("him_p"),
            st("ckv_s"), st("kr_s"), st("hre_s"), st("him_s"))
```

```python
import functools
import math

import jax
import jax.numpy as jnp
from jax import lax
from jax.experimental import pallas as pl
from jax.experimental.pallas import tpu as pltpu

F32 = jnp.float32
BF16 = jnp.bfloat16

NORM_EPS = 1e-6
ROPE_BASE = 10000.0
SSM_GROUP_WIDTH = 16
N_MODULATIONS = 9
LANES = 128
VMEM_LIMIT_BYTES = 56 * 1024 * 1024
FFN_CHUNK = 256


def _cparams(*sem):
    return pltpu.CompilerParams(dimension_semantics=sem, vmem_limit_bytes=VMEM_LIMIT_BYTES)


def _sigmoid(x):
    return 1.0 / (1.0 + jnp.exp(-x))


def _rms(x, w):
    return x * lax.rsqrt(jnp.mean(x * x, axis=-1, keepdims=True) + NORM_EPS) * w


def _of_layer(arr, layer):
    zeros = (0,) * (arr.ndim - 1)
    return pl.BlockSpec((None,) + arr.shape[1:], lambda *_: (layer,) + zeros, pipeline_mode=pl.Buffered(1))


def _adaln_kernel(c_ref, w_ref, b_ref, o_ref):
    c = c_ref[...]
    a = (c * _sigmoid(c)).astype(BF16)
    o_ref[...] = jnp.dot(a, w_ref[...].astype(BF16), preferred_element_type=F32) + b_ref[...]


def _adaln(c_all, ada_w, ada_b, tn=1536):
    depth, d, n = ada_w.shape
    rows = c_all.shape[0]
    return pl.pallas_call(
        _adaln_kernel,
        out_shape=jax.ShapeDtypeStruct((depth, rows, n), F32),
        grid=(depth, n // tn),
        in_specs=[pl.BlockSpec((rows, d), lambda l, j: (0, 0)),
                  pl.BlockSpec((None, d, tn), lambda l, j: (l, 0, j)),
                  pl.BlockSpec((None, 1, tn), lambda l, j: (l, 0, j))],
        out_specs=pl.BlockSpec((None, rows, tn), lambda l, j: (l, 0, j)),
        compiler_params=_cparams("arbitrary", "arbitrary"),
        name="adaln",
    )(c_all, ada_w, ada_b.reshape(depth, 1, n))


class _Mod:
    def __init__(self, arr, layer, per_token, rows_per_batch):
        self.arr = arr
        self.layer = layer
        self.per_token = per_token
        self.rows_per_batch = rows_per_batch

    def spec(self, k, tm, d):
        layer = self.layer
        if self.per_token:
            return pl.BlockSpec((None, None, tm, d), lambda i: (layer, k, i, 0))
        tiles_per_batch = self.rows_per_batch // tm
        return pl.BlockSpec((None, None, None, 1, d), lambda i: (layer, i // tiles_per_batch, k, 0, 0))


def _ffn_kernel(x_ref, sh_ref, sc_ref, g_ref, nw_ref, wup_ref, wdn_ref, *rest, final):
    if final:
        fn_ref, o_ref, h_scr, acc_scr = rest
    else:
        o_ref, h_scr, acc_scr = rest
    d_ff = wdn_ref.shape[0]
    x = x_ref[...]
    h_scr[...] = (_rms(x, nw_ref[...]) * (1.0 + sc_ref[...]) + sh_ref[...]).astype(BF16)

    def down(c):
        cols = slice(c * FFN_CHUNK, (c + 1) * FFN_CHUNK)
        gate_cols = slice(d_ff + c * FFN_CHUNK, d_ff + (c + 1) * FFN_CHUNK)
        h = h_scr[...]
        a = jnp.dot(h, wup_ref[:, cols], preferred_element_type=F32)
        b = jnp.dot(h, wup_ref[:, gate_cols], preferred_element_type=F32)
        act = (a * _sigmoid(a) * b).astype(BF16)
        return jnp.dot(act, wdn_ref[cols, :], preferred_element_type=F32)

    n_chunks = d_ff // FFN_CHUNK
    acc_scr[...] = down(0)
    for c in range(1, n_chunks - 1):
        acc_scr[...] += down(c)
    y = x + (0.5 * g_ref[...]) * (acc_scr[...] + down(n_chunks - 1))
    if final:
        y = _rms(y, fn_ref[...])
    o_ref[...] = y


def _ffn(x, mod, k0, layer, norm_w, w_up, w_down, final_w=None, tm=512):
    m, d = x.shape
    tm = min(tm, m)
    d_ff = w_down.shape[1]
    assert d_ff % FFN_CHUNK == 0
    final = final_w is not None
    in_specs = [pl.BlockSpec((tm, d), lambda i: (i, 0)),
                mod.spec(k0, tm, d), mod.spec(k0 + 1, tm, d), mod.spec(k0 + 2, tm, d),
                _of_layer(norm_w, layer), _of_layer(w_up, layer), _of_layer(w_down, layer)]
    args = [x, mod.arr, mod.arr, mod.arr, norm_w, w_up, w_down]
    if final:
        in_specs.append(_of_layer(final_w, 0))
        args.append(final_w)
    return pl.pallas_call(
        functools.partial(_ffn_kernel, final=final),
        out_shape=jax.ShapeDtypeStruct((m, d), F32),
        grid=(m // tm,),
        in_specs=in_specs,
        out_specs=pl.BlockSpec((tm, d), lambda i: (i, 0)),
        scratch_shapes=[pltpu.VMEM((tm, d), BF16), pltpu.VMEM((tm, d), F32)],
        compiler_params=_cparams("arbitrary"),
        name="ffn",
    )(*args)


def _mixer_in_kernel(x_ref, sh_ref, sc_ref, nw_ref, cs_ref, wa_ref, wu_ref, qn_ref, kvn_ref, wuq_ref, wuk_ref,
                     q_ref, kc_ref, *rest,
                     n_heads, q_rank, kv_rank, rope_dim, nope_dim, d_ssm, scale, transposed, sub):
    if transposed:
        vt_ref, ckv_ref, kr_ref, u_ref, sga_ref, sgb_ref = rest
    else:
        ckv_ref, kr_ref, u_ref, sga_ref, sgb_ref = rest
    d_model = sga_ref.shape[-1]
    base = n_heads * nope_dim

    def rows_of(ref, r):
        return ref[...] if ref.shape[0] == 1 else ref[r, :]

    for r0 in range(0, x_ref.shape[0], sub):
        r = slice(r0, r0 + sub)
        x = x_ref[r, :]
        h = (_rms(x, nw_ref[...]) * (1.0 + rows_of(sc_ref, r)) + rows_of(sh_ref, r)).astype(BF16)
        cos = cs_ref[r, :LANES]
        sin = cs_ref[r, LANES:]

        def rotate(pair):
            return pair * cos + pltpu.roll(pair, rope_dim, axis=1) * sin

        t = jnp.dot(h, wa_ref[...], preferred_element_type=F32)
        cq = t[:, :q_rank]
        kr = rotate(t[:, q_rank:q_rank + LANES])
        ckv = _rms(t[:, q_rank + LANES:], kvn_ref[...])
        kr_ref[r, :] = kr[:, :rope_dim]
        ckv_ref[r, :] = ckv
        kc_ref[r, :kv_rank] = ckv.astype(BF16)
        kc_ref[r, kv_rank:] = kr.astype(BF16)
        if transposed:
            vt_ref[:, r] = ckv.T.astype(BF16)

        ug = jnp.dot(h, wu_ref[...], preferred_element_type=F32)
        u_ref[r, :] = ug[:, :d_ssm]
        sga_ref[r, :] = _sigmoid(ug[:, d_ssm:d_ssm + d_model])
        sgb_ref[r, :] = _sigmoid(ug[:, d_ssm + d_model:])

        cqn = _rms(cq, qn_ref[...]).astype(BF16)
        q = jnp.dot(cqn, wuq_ref[...], preferred_element_type=F32)
        for hd in range(n_heads):
            nope = q[:, hd * nope_dim:(hd + 1) * nope_dim].astype(BF16)
            q_lat = jnp.dot(nope, wuk_ref[hd], preferred_element_type=F32) * scale
            q_rot = rotate(q[:, base + hd * LANES:base + (hd + 1) * LANES]) * scale
            if transposed:
                q_ref[hd, :kv_rank, r] = q_lat.T.astype(BF16)
                q_ref[hd, kv_rank:, r] = q_rot.T.astype(BF16)
            else:
                q_ref[hd, r, :kv_rank] = q_lat.astype(BF16)
                q_ref[hd, r, kv_rank:] = q_rot.astype(BF16)


def _mixer_in(x, mod, layer, norm_w, cs_table, w, dims, n_batch, transposed, tm=512, sub=256):
    m, d = x.shape
    tm = min(tm, m // n_batch)
    rows = m // n_batch
    tiles = rows // tm
    cs_tiles = cs_table.shape[0] // tm
    nh, q_rank, kv_rank, rope_dim, nope_dim, d_ssm = (dims[k] for k in
                                                       ("n_heads", "q_rank", "kv_rank", "rope_dim", "nope_dim", "d_ssm"))
    kw = kv_rank + LANES
    kern = functools.partial(_mixer_in_kernel, n_heads=nh, q_rank=q_rank, kv_rank=kv_rank, rope_dim=rope_dim,
                             nope_dim=nope_dim, d_ssm=d_ssm, scale=(nope_dim + rope_dim) ** -0.5, transposed=transposed,
                             sub=min(sub, tm))
    row = lambda width: pl.BlockSpec((tm, width), lambda i: (i, 0))
    out_shape = [jax.ShapeDtypeStruct((n_batch, nh, kw, rows) if transposed else (n_batch, nh, rows, kw), BF16),
                 jax.ShapeDtypeStruct((m, kw), BF16),
                 jax.ShapeDtypeStruct((m, kv_rank), F32),
                 jax.ShapeDtypeStruct((m, rope_dim), F32),
                 jax.ShapeDtypeStruct((m, d_ssm), F32),
                 jax.ShapeDtypeStruct((m, d), F32),
                 jax.ShapeDtypeStruct((m, d), F32)]
    out_specs = [pl.BlockSpec((None, nh, kw, tm), lambda i: (i // tiles, 0, 0, i % tiles)) if transposed else
                 pl.BlockSpec((None, nh, tm, kw), lambda i: (i // tiles, 0, i % tiles, 0)),
                 row(kw), row(kv_rank), row(rope_dim), row(d_ssm), row(d), row(d)]
    if transposed:
        out_shape.insert(2, jax.ShapeDtypeStruct((n_batch, kv_rank, rows), BF16))
        out_specs.insert(2, pl.BlockSpec((None, kv_rank, tm), lambda i: (i // tiles, 0, i % tiles)))
    weights = [norm_w, w["wa"], w["wu"], w["qn"], w["kvn"], w["wuq"], w["wuk"]]
    lspec = [_of_layer(a, layer) for a in weights]
    return pl.pallas_call(
        kern,
        out_shape=out_shape,
        grid=(m // tm,),
        in_specs=[row(d), mod.spec(3, tm, d), mod.spec(4, tm, d), lspec[0],
                  pl.BlockSpec((tm, 2 * LANES), lambda i: (i % cs_tiles, 0))] + lspec[1:],
        out_specs=out_specs,
        compiler_params=_cparams("arbitrary"),
        name="mixer_in",
    )(x, mod.arr, mod.arr, norm_w, cs_table, *weights[1:])


def _rope_tables(pos, rope_dim):
    half = rope_dim // 2
    inv = ROPE_BASE ** (-jnp.arange(half, dtype=F32) / half)
    ang = pos.astype(F32)[:, None] * inv[None, :]
    cos, sin = jnp.cos(ang), jnp.sin(ang)
    pad = jnp.zeros((pos.shape[0], LANES - rope_dim), F32)
    return jnp.concatenate([cos, cos, pad, -sin, sin, pad], axis=-1)


def _swap_halves(w, rope_dim):
    half = rope_dim // 2
    return jnp.concatenate([w[..., half:], w[..., :half]], axis=-1)


def _pad_lanes(w):
    return jnp.pad(w, [(0, 0)] * (w.ndim - 1) + [(0, LANES - w.shape[-1])])


def _mixer_in_weights(w_in, q_norm, w_uq, kv_norm, w_uk, dims):
    nh, q_rank, kv_rank, rope_dim, nope_dim, d_ssm = (dims[k] for k in
                                                       ("n_heads", "q_rank", "kv_rank", "rope_dim", "nope_dim", "d_ssm"))
    depth = w_in.shape[0]
    o1, o2 = q_rank + kv_rank, q_rank + kv_rank + rope_dim
    w_cq, w_ckv, w_kr, w_rest = w_in[..., :q_rank], w_in[..., q_rank:o1], w_in[..., o1:o2], w_in[..., o2:]
    assert 2 * rope_dim == LANES
    wa = jnp.concatenate([w_cq, w_kr, _swap_halves(w_kr, rope_dim), w_ckv], axis=-1)
    wq = w_uq.reshape(depth, q_rank, nh, nope_dim + rope_dim)
    wq_nope = wq[..., :nope_dim].reshape(depth, q_rank, nh * nope_dim)
    wq_rope = wq[..., nope_dim:]
    wq_pair = jnp.concatenate([wq_rope, _swap_halves(wq_rope, rope_dim)], axis=-1).reshape(depth, q_rank, nh * LANES)
    return {"wa": wa.astype(BF16), "wu": w_rest.astype(BF16),
            "qn": q_norm.reshape(depth, 1, q_rank), "kvn": kv_norm.reshape(depth, 1, kv_rank),
            "wuq": jnp.concatenate([wq_nope, wq_pair], axis=-1).astype(BF16),
            "wuk": w_uk.transpose(0, 2, 3, 1).astype(BF16)}


_NT = (((1,), (1,)), ((), ()))


def _attn_prompt_kernel(qt_ref, k_ref, vt_ref, o_ref, m_scr, l_scr, acc_scr, s_scr, *, tq, n_heads):
    qi = pl.program_id(1)

    def step(j, diagonal):
        keys = pl.ds(pl.multiple_of(j * tq, tq), tq)
        k = k_ref[keys, :]
        vt = vt_ref[:, keys]
        if diagonal:
            keep = (lax.broadcasted_iota(jnp.int32, (tq, tq), 0) <= lax.broadcasted_iota(jnp.int32, (tq, tq), 1))
        for hd in range(n_heads):
            s_scr[hd] = jnp.dot(k, qt_ref[hd], preferred_element_type=F32)
        for hd in range(n_heads):
            s = s_scr[hd]
            if diagonal:
                s = jnp.where(keep, s, -jnp.inf)
                m_new = jnp.max(s, axis=0, keepdims=True)
                p = jnp.exp(s - m_new)
                l_scr[hd] = jnp.sum(p, axis=0, keepdims=True)
                acc_scr[hd] = jnp.dot(vt, p.astype(BF16), preferred_element_type=F32)
            else:
                m_prev = m_scr[hd]
                m_new = jnp.maximum(m_prev, jnp.max(s, axis=0, keepdims=True))
                alpha = jnp.exp(m_prev - m_new)
                p = jnp.exp(s - m_new)
                l_scr[hd] = alpha * l_scr[hd] + jnp.sum(p, axis=0, keepdims=True)
                acc_scr[hd] = alpha * acc_scr[hd] + jnp.dot(vt, p.astype(BF16), preferred_element_type=F32)
            m_scr[hd] = m_new

    def body(j, carry):
        step(j, False)
        return carry

    step(qi, True)
    lax.fori_loop(0, qi, body, 0)
    for hd in range(n_heads):
        o_ref[hd] = (acc_scr[hd] / l_scr[hd]).T.astype(BF16)


def _attn_prompt(qt, kc, vt, tq=256):
    b, nh, kw, t = qt.shape
    kv_rank = vt.shape[1]
    tq = min(tq, t)
    assert t % tq == 0 and tq % LANES == 0 and kv_rank % LANES == 0
    return pl.pallas_call(
        functools.partial(_attn_prompt_kernel, tq=tq, n_heads=nh),
        out_shape=jax.ShapeDtypeStruct((b, nh, t, kv_rank), BF16),
        grid=(b, t // tq),
        in_specs=[pl.BlockSpec((None, nh, kw, tq), lambda bi, qi: (bi, 0, 0, qi)),
                  pl.BlockSpec((None, t, kw), lambda bi, qi: (bi, 0, 0)),
                  pl.BlockSpec((None, kv_rank, t), lambda bi, qi: (bi, 0, 0))],
        out_specs=pl.BlockSpec((None, nh, tq, kv_rank), lambda bi, qi: (bi, 0, qi, 0)),
        scratch_shapes=[pltpu.VMEM((nh, 1, tq), F32), pltpu.VMEM((nh, 1, tq), F32),
                        pltpu.VMEM((nh, kv_rank, tq), F32), pltpu.VMEM((nh, tq, tq), F32)],
        compiler_params=_cparams("arbitrary", "arbitrary"),
        name="attn_prompt",
    )(qt, kc, vt)


def _attn_sample_kernel(pt_ref, q_ref, knew_ref, ckv_hbm, krt_hbm, o_ref, kbuf, krbuf, sem,
                        *, layer, n_pages, page, kv_rank, rope_dim, new_len, n_req):
    b = pl.program_id(0)
    slot = lax.rem(b, 2)

    def page_copies(req, slot_, p):
        pid = pt_ref[req, p]
        off = pl.multiple_of(p * page, page)
        return (pltpu.make_async_copy(ckv_hbm.at[layer, pid], kbuf.at[slot_, pl.ds(off, page), :], sem.at[0, slot_]),
                pltpu.make_async_copy(krt_hbm.at[layer, pid], krbuf.at[slot_, :, pl.ds(off, page)], sem.at[1, slot_]))

    def start_all(req, slot_):
        def body(p, carry):
            for cp in page_copies(req, slot_, p):
                cp.start()
            return carry
        lax.fori_loop(0, n_pages, body, 0, unroll=8)

    def wait_all(req, slot_):
        def body(p, carry):
            for cp in page_copies(req, slot_, p):
                cp.wait()
            return carry
        lax.fori_loop(0, n_pages, body, 0, unroll=8)

    @pl.when(b == 0)
    def _():
        start_all(0, 0)

    @pl.when(b + 1 < n_req)
    def _():
        start_all(b + 1, 1 - slot)

    wait_all(b, slot)

    q = q_ref[...]
    rows = q.shape[0]
    kb = kbuf[slot].astype(BF16)
    s = (lax.dot_general(q[:, :kv_rank], kb, _NT, preferred_element_type=F32)
         + jnp.dot(q[:, kv_rank:kv_rank + rope_dim], krbuf[slot].astype(BF16), preferred_element_type=F32))
    kn = knew_ref[...]
    npad = kn.shape[0]
    sn = lax.dot_general(q, kn, _NT, preferred_element_type=F32)
    q_pos = jnp.bitwise_and(lax.broadcasted_iota(jnp.int32, (rows, npad), 0), new_len - 1)
    k_pos = lax.broadcasted_iota(jnp.int32, (rows, npad), 1)
    sn = jnp.where(k_pos <= q_pos, sn, -jnp.inf)
    m = jnp.maximum(jnp.max(s, axis=1, keepdims=True), jnp.max(sn, axis=1, keepdims=True))
    p = jnp.exp(s - m)
    pn = jnp.exp(sn - m)
    denom = jnp.sum(p, axis=1, keepdims=True) + jnp.sum(pn, axis=1, keepdims=True)
    o = (jnp.dot(p.astype(BF16), kb, preferred_element_type=F32)
         + jnp.dot(pn.astype(BF16), kn[:, :kv_rank], preferred_element_type=F32))
    o_ref[...] = (o / denom).astype(BF16)


def _attn_sample(q, k_new, cache_ckv, cache_krt, page_table, layer, kv_rank, rope_dim, new_len):
    bs, rows, kw = q.shape
    n_pages = page_table.shape[1]
    page = cache_ckv.shape[2]
    past = n_pages * page
    new_pad = k_new.shape[1]
    assert new_len & (new_len - 1) == 0 and page % LANES == 0
    kern = functools.partial(_attn_sample_kernel, layer=layer, n_pages=n_pages, page=page, kv_rank=kv_rank,
                             rope_dim=rope_dim, new_len=new_len, n_req=bs)
    return pl.pallas_call(
        kern,
        out_shape=jax.ShapeDtypeStruct((bs, rows, kv_rank), BF16),
        grid_spec=pltpu.PrefetchScalarGridSpec(
            num_scalar_prefetch=1,
            grid=(bs,),
            in_specs=[pl.BlockSpec((None, rows, kw), lambda b, pt: (b, 0, 0)),
                      pl.BlockSpec((None, new_pad, kw), lambda b, pt: (b, 0, 0)),
                      pl.BlockSpec(memory_space=pl.ANY),
                      pl.BlockSpec(memory_space=pl.ANY)],
            out_specs=pl.BlockSpec((None, rows, kv_rank), lambda b, pt: (b, 0, 0)),
            scratch_shapes=[pltpu.VMEM((2, past, kv_rank), F32),
                            pltpu.VMEM((2, rope_dim, past), F32),
                            pltpu.SemaphoreType.DMA((2, 2))]),
        compiler_params=_cparams("arbitrary"),
        name="attn_sample",
    )(page_table, q, k_new, cache_ckv, cache_krt)


def _gelu_tanh(y):
    return 0.5 * y * (1.0 + jnp.tanh(math.sqrt(2.0 / math.pi) * (y + 0.044715 * (y * y * y))))


def _ssm_kernel(u_ref, kmat_ref, bend_ref, cpow_ref, lre_ref, lim_ref, dsk_ref, h0_ref, z_ref, hn_ref,
                h_scr, s_scr, y_scr, *, chunk, n_seq, n_chunks):
    nb, tt, lanes = u_ref.shape
    rpb = tt // chunk
    half = lre_ref.shape[-1]

    @pl.when(pl.program_id(1) == 0)
    def _():
        h_scr[...] = h0_ref[...]

    u_rows = jnp.concatenate(
        [jnp.concatenate([u_ref[b, pl.ds(s, rpb, stride=chunk), :] for s in range(chunk)], axis=1)
         for b in range(nb)], axis=0)
    ub = u_rows.astype(BF16)
    own = jnp.dot(ub, bend_ref[...], preferred_element_type=F32)
    nt = half // lanes
    tile = lambda x, j: x[:, j * lanes:(j + 1) * lanes]

    def seq_rows(q):
        return pl.ds(q, n_chunks, stride=n_seq)

    if n_chunks > 1:
        for q in range(n_seq):
            for j in range(2 * nt):
                s_scr[j, seq_rows(q), :] = tile(own[q * n_chunks:(q + 1) * n_chunks], j)
    else:
        for j in range(2 * nt):
            s_scr[j] = tile(own, j)
    lre = [tile(lre_ref[...], j) for j in range(nt)]
    lim = [tile(lim_ref[...], j) for j in range(nt)]
    h = h_scr[...]
    h_re = [tile(h, j) for j in range(nt)]
    h_im = [tile(h, nt + j) for j in range(nt)]
    for n in range(n_chunks):
        rows = pl.ds(n * n_seq, n_seq)
        for j in range(nt):
            s_re, s_im = s_scr[j, rows, :], s_scr[nt + j, rows, :]
            s_scr[j, rows, :] = h_re[j]
            s_scr[nt + j, rows, :] = h_im[j]
            h_re[j], h_im[j] = (lre[j] * h_re[j] - lim[j] * h_im[j] + s_re,
                                lre[j] * h_im[j] + lim[j] * h_re[j] + s_im)
    h_scr[...] = jnp.concatenate(h_re + h_im, axis=1)
    hn_ref[...] = h_scr[...]
    if n_chunks > 1:
        entering = jnp.concatenate(
            [jnp.concatenate([s_scr[j, seq_rows(q), :] for j in range(2 * nt)], axis=1) for q in range(n_seq)], axis=0)
    else:
        entering = jnp.concatenate([s_scr[j] for j in range(2 * nt)], axis=1)
    y = (jnp.dot(ub, kmat_ref[...], preferred_element_type=F32)
         + jnp.dot(entering.astype(BF16), cpow_ref[...], preferred_element_type=F32))
    for b in range(nb):
        for t in range(chunk):
            y_scr[b, pl.ds(t, rpb, stride=chunk), :] = y[b * rpb:(b + 1) * rpb, t * lanes:(t + 1) * lanes]
    z_ref[...] = _gelu_tanh(y_scr[...] + dsk_ref[...] * u_ref[...]).astype(BF16)


def _ssm(u3, p, layer, h0, chunk, n_seq, tt=512):
    nb, t, d = u3.shape
    n_sg = d // LANES
    tt = min(tt, t)
    rows = nb * tt // chunk
    n_chunks = rows // n_seq
    n_state = h0.shape[-1]
    per_sg = lambda arr: pl.BlockSpec((None, None) + arr.shape[2:], lambda sg, i: (layer, sg, 0, 0))
    tok = pl.BlockSpec((nb, tt, LANES), lambda sg, i: (0, i, sg))
    state = pl.BlockSpec((None, n_seq, n_state), lambda sg, i: (sg, 0, 0))
    ops = [p[k] for k in ("kmat", "bend", "cpow", "lre", "lim", "dsk")]
    return pl.pallas_call(
        functools.partial(_ssm_kernel, chunk=chunk, n_seq=n_seq, n_chunks=n_chunks),
        out_shape=[jax.ShapeDtypeStruct((nb, t, d), BF16), jax.ShapeDtypeStruct(h0.shape, F32)],
        grid=(n_sg, t // tt),
        in_specs=[tok] + [per_sg(a) for a in ops] + [state],
        out_specs=[tok, state],
        scratch_shapes=[pltpu.VMEM((n_seq, n_state), F32), pltpu.VMEM((n_state // LANES, rows, LANES), F32),
                        pltpu.VMEM((nb, tt, LANES), F32)],
        compiler_params=_cparams("arbitrary", "arbitrary"),
        name="ssm",
    )(u3, *ops, h0)


def _ssm_params(a_re, a_im, log_dt, b_re, b_im, c_re, c_im, d_skip, chunk):
    hp = lax.Precision.HIGHEST
    g, n_state = a_re.shape
    cw = SSM_GROUP_WIDTH
    gl = LANES // cw
    n_sg = g // gl
    sw = gl * n_state
    lam = lax.complex(a_re, a_im)
    lam_dt = lam * jnp.exp(log_dt)[:, None]
    b_bar = ((jnp.exp(lam_dt) - 1.0) / lam)[..., None] * lax.complex(b_re, b_im)
    steps = jnp.arange(chunk + 1, dtype=F32)[:, None, None]
    mag = jnp.exp(lam_dt.real.reshape(n_sg, sw)[None] * steps)
    ang = lam_dt.imag.reshape(n_sg, sw)[None] * steps
    pw_re, pw_im = mag * jnp.cos(ang), mag * jnp.sin(ang)

    same_group = (jnp.arange(LANES)[:, None] // cw) == (jnp.arange(sw)[None, :] // n_state)

    def by_channel(x):
        return jnp.where(same_group, jnp.tile(x.reshape(n_sg, LANES, n_state), (1, 1, gl)), 0.0)

    def times_pw(x_re, x_im, p_re, p_im):
        p_re, p_im = p_re[:, :, None], p_im[:, :, None]
        return x_re * p_re - x_im * p_im, x_re * p_im + x_im * p_re

    bt = b_bar.transpose(0, 2, 1)
    c_re_d, c_im_d = by_channel(c_re), by_channel(c_im)
    bp_re, bp_im = times_pw(by_channel(bt.real), by_channel(bt.imag), pw_re[:chunk], pw_im[:chunk])
    lag = (jnp.einsum("kSaq,Sbq->kSab", bp_re, c_re_d, precision=hp)
           - jnp.einsum("kSaq,Sbq->kSab", bp_im, c_im_d, precision=hp))
    t_idx = jnp.arange(chunk)
    diff = t_idx[None, :] - t_idx[:, None]
    kmat = jnp.where((diff >= 0)[:, :, None, None, None], lag[jnp.clip(diff, 0, chunk - 1)], 0.0)
    kmat = kmat.astype(BF16).transpose(2, 0, 3, 1, 4).reshape(n_sg, chunk * LANES, chunk * LANES)
    bend = jnp.concatenate([bp_re[::-1], bp_im[::-1]], axis=-1).astype(BF16)
    bend = bend.transpose(1, 0, 2, 3).reshape(n_sg, chunk * LANES, 2 * sw)
    cp_re, cp_im = times_pw(c_re_d, c_im_d, pw_re[1:], pw_im[1:])
    cpow = jnp.concatenate([cp_re, -cp_im], axis=-1).astype(BF16)
    cpow = cpow.transpose(1, 3, 0, 2).reshape(n_sg, 2 * sw, chunk * LANES)
    return {"kmat": kmat, "bend": bend, "cpow": cpow,
            "lre": pw_re[chunk].reshape(n_sg, 1, sw), "lim": pw_im[chunk].reshape(n_sg, 1, sw),
            "dsk": d_skip.reshape(n_sg, 1, LANES)}


def _pack_state(h_re, h_im):
    b, g, p = h_re.shape
    gl = LANES // SSM_GROUP_WIDTH
    f = lambda h: h.reshape(b, g // gl, gl * p).transpose(1, 0, 2)
    return jnp.concatenate([f(h_re), f(h_im)], axis=-1)


def _unpack_state(h, n_state):
    n_sg, b, w = h.shape
    f = lambda x: x.transpose(1, 0, 2).reshape(b, -1, n_state)
    return f(h[..., :w // 2]), f(h[..., w // 2:])


def _mixer_out_kernel(x_ref, gm_ref, z_ref, o_ref, sga_ref, sgb_ref, wglu_ref, wuv_ref, wout_ref, y_ref, mixed_scr,
                      *, n_heads, v_dim):
    d = x_ref.shape[-1]
    glu = jnp.dot(z_ref[...], wglu_ref[...], preferred_element_type=F32)
    y_ssm = glu[:, :d] * _sigmoid(glu[:, d:])
    mixed_scr[...] = (sgb_ref[...] * y_ssm).astype(mixed_scr.dtype)
    for hd in range(n_heads):
        cols = slice(hd * v_dim, (hd + 1) * v_dim)
        y_attn = jnp.dot(o_ref[hd], wuv_ref[hd], preferred_element_type=F32)
        mixed_scr[:, cols] = mixed_scr[:, cols] + sga_ref[:, cols] * y_attn
    proj = jnp.dot(mixed_scr[...].astype(BF16), wout_ref[...], preferred_element_type=F32)
    y_ref[...] = x_ref[...] + gm_ref[...] * proj


def _mixer_out(x, mod, layer, z, o_lat, sga, sgb, w, n_batch, tm=512):
    m, d = x.shape
    rows = m // n_batch
    tm = min(tm, rows)
    tiles = rows // tm
    _, nh, _, kv_rank = o_lat.shape
    v_dim = w["wuv"].shape[-1]
    row = lambda width: pl.BlockSpec((tm, width), lambda i: (i, 0))
    weights = [w["wglu"], w["wuv"], w["wout"]]
    return pl.pallas_call(
        functools.partial(_mixer_out_kernel, n_heads=nh, v_dim=v_dim),
        out_shape=jax.ShapeDtypeStruct((m, d), F32),
        grid=(m // tm,),
        in_specs=[row(d), mod.spec(5, tm, d), row(d),
                  pl.BlockSpec((None, nh, tm, kv_rank), lambda i: (i // tiles, 0, i % tiles, 0)),
                  row(d), row(d)] + [_of_layer(a, layer) for a in weights],
        out_specs=row(d),
        scratch_shapes=[pltpu.VMEM((tm, d), F32)],
        compiler_params=_cparams("arbitrary"),
        name="mixer_out",
    )(x, mod.arr, z, o_lat, sga, sgb, *weights)


def _layer(x, mod, layer, n_batch, cs_table, w, dims, attend, transposed, ssm_seqs, ssm_chunk, h0, final_w):
    m, d = x.shape
    x = _ffn(x, mod, 0, layer, w["norm_ffn1"], w["ffn1_up"], w["ffn1_down"])
    *qkv, ckv, kr, u, sga, sgb = _mixer_in(x, mod, layer, w["norm_mix"], cs_table, w["mix_in"], dims, n_batch, transposed)
    o_lat = attend(*qkv)
    z, h_n = _ssm(u.reshape(n_batch, m // n_batch, u.shape[-1]), w["ssm"][ssm_chunk], layer, h0, ssm_chunk, ssm_seqs)
    x = _mixer_out(x, mod, layer, z.reshape(m, z.shape[-1]), o_lat, sga, sgb, w["mix_out"], n_batch)
    x = _ffn(x, mod, 6, layer, w["norm_ffn2"], w["ffn2_up"], w["ffn2_down"], final_w=final_w)
    return x, ckv, kr, h_n


def kernel(x_prompt, x_sample, c_prompt, c_sample, cache_ckv, cache_kr, state_ssm_re, state_ssm_im, page_table, ada_w, ada_b, norm_ffn1, ffn1_up, ffn1_down, norm_mix, w_in, q_norm, w_uq, kv_norm, w_uk, w_uv, ssm_a_re, ssm_a_im, ssm_log_dt, ssm_b_re, ssm_b_im, ssm_c_re, ssm_c_im, ssm_d, w_glu, w_out, norm_ffn2, ffn2_up, ffn2_down, final_norm):
    bp, tp, d = x_prompt.shape
    bs, ts, _ = x_sample.shape
    depth = ada_w.shape[0]
    q_rank, kv_rank = q_norm.shape[-1], kv_norm.shape[-1]
    rope_dim = cache_kr.shape[-1]
    nh, nope_dim = w_uk.shape[2], w_uk.shape[3]
    n_groups, n_state = ssm_a_re.shape[1], ssm_a_re.shape[2]
    dims = {"n_heads": nh, "q_rank": q_rank, "kv_rank": kv_rank, "rope_dim": rope_dim, "nope_dim": nope_dim,
            "d_ssm": ssm_d.shape[-1]}
    past_len = page_table.shape[1] * cache_ckv.shape[2]
    sample_chunk = ts
    prompt_chunk = ts if tp % ts == 0 else tp
    cache_krt = jnp.swapaxes(cache_kr, 2, 3)

    ssm_args = (ssm_a_re, ssm_a_im, ssm_log_dt, ssm_b_re, ssm_b_im, ssm_c_re, ssm_c_im, ssm_d)
    w = {"norm_ffn1": norm_ffn1.reshape(depth, 1, d), "norm_mix": norm_mix.reshape(depth, 1, d),
         "norm_ffn2": norm_ffn2.reshape(depth, 1, d),
         "ffn1_up": ffn1_up.astype(BF16), "ffn1_down": ffn1_down.astype(BF16),
         "ffn2_up": ffn2_up.astype(BF16), "ffn2_down": ffn2_down.astype(BF16),
         "mix_in": _mixer_in_weights(w_in, q_norm, w_uq, kv_norm, w_uk, dims),
         "mix_out": {"wglu": w_glu.astype(BF16), "wuv": w_uv.transpose(0, 2, 1, 3).astype(BF16),
                     "wout": w_out.astype(BF16)},
         "ssm": {c: jax.vmap(functools.partial(_ssm_params, chunk=c))(*ssm_args) for c in {prompt_chunk, sample_chunk}}}
    final_w = final_norm.reshape(1, 1, d)

    mod_all = _adaln(jnp.concatenate([c_prompt, c_sample], axis=0), ada_w, ada_b)
    mod_p_arr = mod_all[:, :bp].reshape(depth, bp, N_MODULATIONS, 1, d)
    mod_s_arr = jnp.repeat(mod_all[:, bp:].reshape(depth, bs, N_MODULATIONS, d), ts, axis=1).transpose(0, 2, 1, 3)
    cs_p = _rope_tables(jnp.arange(tp), rope_dim)
    cs_s = jnp.tile(_rope_tables(past_len + jnp.arange(ts), rope_dim), (bs, 1))

    xp = x_prompt.reshape(bp * tp, d)
    xs = x_sample.reshape(bs * ts, d)
    zeros_state = jnp.zeros((bp, n_groups, n_state), F32)
    h0_p = _pack_state(zeros_state, zeros_state)
    outs = {k: [] for k in ("ckv_p", "kr_p", "hre_p", "him_p", "ckv_s", "kr_s", "hre_s", "him_s")}
    for l in range(depth):
        last = final_w if l == depth - 1 else None

        xp, ckv, kr, h_n = _layer(
            xp, _Mod(mod_p_arr, l, False, tp), l, bp, cs_p, w, dims,
            lambda qt, kc, vt: _attn_prompt(qt, kc.reshape(bp, tp, kc.shape[-1]), vt), True, bp, prompt_chunk,
            h0_p, last)
        h_re, h_im = _unpack_state(h_n, n_state)
        outs["ckv_p"].append(ckv.reshape(bp, tp, kv_rank)); outs["kr_p"].append(kr.reshape(bp, tp, rope_dim))
        outs["hre_p"].append(h_re); outs["him_p"].append(h_im)

        def attend_sample(q, kc, l=l):
            kw = q.shape[-1]
            qb = q.reshape(nh, bs, ts, kw).transpose(1, 0, 2, 3).reshape(bs, nh * ts, kw)
            k_new = jnp.pad(kc.reshape(bs, ts, kw), ((0, 0), (0, 16 - ts), (0, 0)))
            o = _attn_sample(qb, k_new, cache_ckv, cache_krt, page_table, l, kv_rank, rope_dim, ts)
            return o.reshape(bs, nh, ts, kv_rank).transpose(1, 0, 2, 3).reshape(1, nh, bs * ts, kv_rank)

        xs, ckv, kr, h_n = _layer(
            xs, _Mod(mod_s_arr, l, True, ts), l, 1, cs_s, w, dims,
            attend_sample, False, bs, sample_chunk, _pack_state(state_ssm_re[l], state_ssm_im[l]), last)
        h_re, h_im = _unpack_state(h_n, n_state)
        outs["ckv_s"].append(ckv.reshape(bs, ts, kv_rank)); outs["kr_s"].append(kr.reshape(bs, ts, rope_dim))
        outs["hre_s"].append(h_re); outs["him_s"].append(h_im)

    st = lambda k: jnp.stack(outs[k])
    return (xp.reshape(bp, tp, d), xs.reshape(bs, ts, d),
            st("ckv_p"), st("kr_p"), st("hre_p"), st("him_p"),
            st("ckv_s"), st("kr_s"), st("hre_s"), st("him_s"))
```

```python
import functools
import math

import jax
import jax.numpy as jnp
from jax import lax
from jax.experimental import pallas as pl
from jax.experimental.pallas import tpu as pltpu

F32 = jnp.float32
BF16 = jnp.bfloat16

NORM_EPS = 1e-6
ROPE_BASE = 10000.0
SSM_GROUP_WIDTH = 16
N_MODULATIONS = 9
LANES = 128
VMEM_LIMIT_BYTES = 56 * 1024 * 1024
FFN_CHUNK = 256


def _cparams(*sem):
    return pltpu.CompilerParams(dimension_semantics=sem, vmem_limit_bytes=VMEM_LIMIT_BYTES)


def _sigmoid(x):
    return 1.0 / (1.0 + jnp.exp(-x))


def _rms(x, w):
    return x * lax.rsqrt(jnp.mean(x * x, axis=-1, keepdims=True) + NORM_EPS) * w


def _of_layer(arr, layer):
    zeros = (0,) * (arr.ndim - 1)
    return pl.BlockSpec((None,) + arr.shape[1:], lambda *_: (layer,) + zeros, pipeline_mode=pl.Buffered(1))


def _adaln_kernel(c_ref, w_ref, b_ref, o_ref):
    c = c_ref[...]
    a = (c * _sigmoid(c)).astype(BF16)
    o_ref[...] = jnp.dot(a, w_ref[...].astype(BF16), preferred_element_type=F32) + b_ref[...]


def _adaln(c_all, ada_w, ada_b, tn=1536):
    depth, d, n = ada_w.shape
    rows = c_all.shape[0]
    return pl.pallas_call(
        _adaln_kernel,
        out_shape=jax.ShapeDtypeStruct((depth, rows, n), F32),
        grid=(depth, n // tn),
        in_specs=[pl.BlockSpec((rows, d), lambda l, j: (0, 0)),
                  pl.BlockSpec((None, d, tn), lambda l, j: (l, 0, j)),
                  pl.BlockSpec((None, 1, tn), lambda l, j: (l, 0, j))],
        out_specs=pl.BlockSpec((None, rows, tn), lambda l, j: (l, 0, j)),
        compiler_params=_cparams("arbitrary", "arbitrary"),
        name="adaln",
    )(c_all, ada_w, ada_b.reshape(depth, 1, n))


class _Mod:
    def __init__(self, arr, layer, per_token, rows_per_batch):
        self.arr = arr
        self.layer = layer
        self.per_token = per_token
        self.rows_per_batch = rows_per_batch

    def spec(self, k, tm, d):
        layer = self.layer
        if self.per_token:
            return pl.BlockSpec((None, None, tm, d), lambda i: (layer, k, i, 0))
        tiles_per_batch = self.rows_per_batch // tm
        return pl.BlockSpec((None, None, None, 1, d), lambda i: (layer, i // tiles_per_batch, k, 0, 0))


def _ffn_kernel(x_ref, sh_ref, sc_ref, g_ref, nw_ref, wup_ref, wdn_ref, *rest, final):
    if final:
        fn_ref, o_ref, h_scr, acc_scr = rest
    else:
        o_ref, h_scr, acc_scr = rest
    d_ff = wdn_ref.shape[0]
    x = x_ref[...]
    h_scr[...] = (_rms(x, nw_ref[...]) * (1.0 + sc_ref[...]) + sh_ref[...]).astype(BF16)

    def down(c):
        cols = slice(c * FFN_CHUNK, (c + 1) * FFN_CHUNK)
        gate_cols = slice(d_ff + c * FFN_CHUNK, d_ff + (c + 1) * FFN_CHUNK)
        h = h_scr[...]
        a = jnp.dot(h, wup_ref[:, cols], preferred_element_type=F32)
        b = jnp.dot(h, wup_ref[:, gate_cols], preferred_element_type=F32)
        act = (a * _sigmoid(a) * b).astype(BF16)
        return jnp.dot(act, wdn_ref[cols, :], preferred_element_type=F32)

    n_chunks = d_ff // FFN_CHUNK
    acc_scr[...] = down(0)
    for c in range(1, n_chunks - 1):
        acc_scr[...] += down(c)
    y = x + (0.5 * g_ref[...]) * (acc_scr[...] + down(n_chunks - 1))
    if final:
        y = _rms(y, fn_ref[...])
    o_ref[...] = y


def _ffn(x, mod, k0, layer, norm_w, w_up, w_down, final_w=None, tm=1024):
    m, d = x.shape
    tm = min(tm, m if mod.per_token else mod.rows_per_batch)
    d_ff = w_down.shape[1]
    assert d_ff % FFN_CHUNK == 0
    final = final_w is not None
    in_specs = [pl.BlockSpec((tm, d), lambda i: (i, 0)),
                mod.spec(k0, tm, d), mod.spec(k0 + 1, tm, d), mod.spec(k0 + 2, tm, d),
                _of_layer(norm_w, layer), _of_layer(w_up, layer), _of_layer(w_down, layer)]
    args = [x, mod.arr, mod.arr, mod.arr, norm_w, w_up, w_down]
    if final:
        in_specs.append(_of_layer(final_w, 0))
        args.append(final_w)
    return pl.pallas_call(
        functools.partial(_ffn_kernel, final=final),
        out_shape=jax.ShapeDtypeStruct((m, d), F32),
        grid=(m // tm,),
        in_specs=in_specs,
        out_specs=pl.BlockSpec((tm, d), lambda i: (i, 0)),
        scratch_shapes=[pltpu.VMEM((tm, d), BF16), pltpu.VMEM((tm, d), F32)],
        compiler_params=_cparams("arbitrary"),
        name="ffn",
    )(*args)


def _mixer_in_kernel(x_ref, sh_ref, sc_ref, nw_ref, cs_ref, wa_ref, wu_ref, qn_ref, kvn_ref, wuq_ref, wuk_ref,
                     q_ref, kc_ref, *rest,
                     n_heads, q_rank, kv_rank, rope_dim, nope_dim, d_ssm, scale, transposed, sub):
    if transposed:
        vt_ref, ckv_ref, kr_ref, u_ref, sga_ref, sgb_ref = rest
    else:
        ckv_ref, kr_ref, u_ref, sga_ref, sgb_ref = rest
    d_model = sga_ref.shape[-1]
    base = n_heads * nope_dim

    def rows_of(ref, r):
        return ref[...] if ref.shape[0] == 1 else ref[r, :]

    for r0 in range(0, x_ref.shape[0], sub):
        r = slice(r0, r0 + sub)
        x = x_ref[r, :]
        h = (_rms(x, nw_ref[...]) * (1.0 + rows_of(sc_ref, r)) + rows_of(sh_ref, r)).astype(BF16)
        cos = cs_ref[r, :LANES]
        sin = cs_ref[r, LANES:]

        def rotate(pair):
            return pair * cos + pltpu.roll(pair, rope_dim, axis=1) * sin

        t = jnp.dot(h, wa_ref[...], preferred_element_type=F32)
        cq = t[:, :q_rank]
        kr = rotate(t[:, q_rank:q_rank + LANES])
        ckv = _rms(t[:, q_rank + LANES:], kvn_ref[...])
        kr_ref[r, :] = kr[:, :rope_dim]
        ckv_ref[r, :] = ckv
        kc_ref[r, :kv_rank] = ckv.astype(BF16)
        kc_ref[r, kv_rank:] = kr.astype(BF16)
        if transposed:
            vt_ref[:, r] = ckv.T.astype(BF16)

        ug = jnp.dot(h, wu_ref[...], preferred_element_type=F32)
        u_ref[r, :] = ug[:, :d_ssm]
        sga_ref[r, :] = _sigmoid(ug[:, d_ssm:d_ssm + d_model])
        sgb_ref[r, :] = _sigmoid(ug[:, d_ssm + d_model:])

        cqn = _rms(cq, qn_ref[...]).astype(BF16)
        q = jnp.dot(cqn, wuq_ref[...], preferred_element_type=F32)
        for hd in range(n_heads):
            nope = q[:, hd * nope_dim:(hd + 1) * nope_dim].astype(BF16)
            q_lat = jnp.dot(nope, wuk_ref[hd], preferred_element_type=F32) * scale
            q_rot = rotate(q[:, base + hd * LANES:base + (hd + 1) * LANES]) * scale
            if transposed:
                q_ref[hd, :kv_rank, r] = q_lat.T.astype(BF16)
                q_ref[hd, kv_rank:, r] = q_rot.T.astype(BF16)
            else:
                q_ref[hd, r, :kv_rank] = q_lat.astype(BF16)
                q_ref[hd, r, kv_rank:] = q_rot.astype(BF16)


def _mixer_in(x, mod, layer, norm_w, cs_table, w, dims, n_batch, transposed, tm=512, sub=256):
    m, d = x.shape
    tm = min(tm, m // n_batch)
    rows = m // n_batch
    tiles = rows // tm
    cs_tiles = cs_table.shape[0] // tm
    nh, q_rank, kv_rank, rope_dim, nope_dim, d_ssm = (dims[k] for k in
                                                       ("n_heads", "q_rank", "kv_rank", "rope_dim", "nope_dim", "d_ssm"))
    kw = kv_rank + LANES
    kern = functools.partial(_mixer_in_kernel, n_heads=nh, q_rank=q_rank, kv_rank=kv_rank, rope_dim=rope_dim,
                             nope_dim=nope_dim, d_ssm=d_ssm, scale=(nope_dim + rope_dim) ** -0.5, transposed=transposed,
                             sub=min(sub, tm))
    row = lambda width: pl.BlockSpec((tm, width), lambda i: (i, 0))
    out_shape = [jax.ShapeDtypeStruct((n_batch, nh, kw, rows) if transposed else (n_batch, nh, rows, kw), BF16),
                 jax.ShapeDtypeStruct((m, kw), BF16),
                 jax.ShapeDtypeStruct((m, kv_rank), F32),
                 jax.ShapeDtypeStruct((m, rope_dim), F32),
                 jax.ShapeDtypeStruct((m, d_ssm), F32),
                 jax.ShapeDtypeStruct((m, d), F32),
                 jax.ShapeDtypeStruct((m, d), F32)]
    out_specs = [pl.BlockSpec((None, nh, kw, tm), lambda i: (i // tiles, 0, 0, i % tiles)) if transposed else
                 pl.BlockSpec((None, nh, tm, kw), lambda i: (i // tiles, 0, i % tiles, 0)),
                 row(kw), row(kv_rank), row(rope_dim), row(d_ssm), row(d), row(d)]
    if transposed:
        out_shape.insert(2, jax.ShapeDtypeStruct((n_batch, kv_rank, rows), BF16))
        out_specs.insert(2, pl.BlockSpec((None, kv_rank, tm), lambda i: (i // tiles, 0, i % tiles)))
    weights = [norm_w, w["wa"], w["wu"], w["qn"], w["kvn"], w["wuq"], w["wuk"]]
    lspec = [_of_layer(a, layer) for a in weights]
    return pl.pallas_call(
        kern,
        out_shape=out_shape,
        grid=(m // tm,),
        in_specs=[row(d), mod.spec(3, tm, d), mod.spec(4, tm, d), lspec[0],
                  pl.BlockSpec((tm, 2 * LANES), lambda i: (i % cs_tiles, 0))] + lspec[1:],
        out_specs=out_specs,
        compiler_params=_cparams("arbitrary"),
        name="mixer_in",
    )(x, mod.arr, mod.arr, norm_w, cs_table, *weights[1:])


def _rope_tables(pos, rope_dim):
    half = rope_dim // 2
    inv = ROPE_BASE ** (-jnp.arange(half, dtype=F32) / half)
    ang = pos.astype(F32)[:, None] * inv[None, :]
    cos, sin = jnp.cos(ang), jnp.sin(ang)
    pad = jnp.zeros((pos.shape[0], LANES - rope_dim), F32)
    return jnp.concatenate([cos, cos, pad, -sin, sin, pad], axis=-1)


def _swap_halves(w, rope_dim):
    half = rope_dim // 2
    return jnp.concatenate([w[..., half:], w[..., :half]], axis=-1)


def _pad_lanes(w):
    return jnp.pad(w, [(0, 0)] * (w.ndim - 1) + [(0, LANES - w.shape[-1])])


def _mixer_in_weights(w_in, q_norm, w_uq, kv_norm, w_uk, dims):
    nh, q_rank, kv_rank, rope_dim, nope_dim, d_ssm = (dims[k] for k in
                                                       ("n_heads", "q_rank", "kv_rank", "rope_dim", "nope_dim", "d_ssm"))
    depth = w_in.shape[0]
    o1, o2 = q_rank + kv_rank, q_rank + kv_rank + rope_dim
    w_cq, w_ckv, w_kr, w_rest = w_in[..., :q_rank], w_in[..., q_rank:o1], w_in[..., o1:o2], w_in[..., o2:]
    assert 2 * rope_dim == LANES
    wa = jnp.concatenate([w_cq, w_kr, _swap_halves(w_kr, rope_dim), w_ckv], axis=-1)
    wq = w_uq.reshape(depth, q_rank, nh, nope_dim + rope_dim)
    wq_nope = wq[..., :nope_dim].reshape(depth, q_rank, nh * nope_dim)
    wq_rope = wq[..., nope_dim:]
    wq_pair = jnp.concatenate([wq_rope, _swap_halves(wq_rope, rope_dim)], axis=-1).reshape(depth, q_rank, nh * LANES)
    return {"wa": wa.astype(BF16), "wu": w_rest.astype(BF16),
            "qn": q_norm.reshape(depth, 1, q_rank), "kvn": kv_norm.reshape(depth, 1, kv_rank),
            "wuq": jnp.concatenate([wq_nope, wq_pair], axis=-1).astype(BF16),
            "wuk": w_uk.transpose(0, 2, 3, 1).astype(BF16)}


_NT = (((1,), (1,)), ((), ()))


def _attn_prompt_kernel(qt_ref, k_ref, vt_ref, o_ref, m_scr, l_scr, acc_scr, s_scr, *, tq, n_heads):
    qi = pl.program_id(1)

    def step(j, diagonal):
        keys = pl.ds(pl.multiple_of(j * tq, tq), tq)
        k = k_ref[keys, :]
        vt = vt_ref[:, keys]
        if diagonal:
            keep = (lax.broadcasted_iota(jnp.int32, (tq, tq), 0) <= lax.broadcasted_iota(jnp.int32, (tq, tq), 1))
        for hd in range(n_heads):
            s_scr[hd] = jnp.dot(k, qt_ref[hd], preferred_element_type=F32)
        for hd in range(n_heads):
            s = s_scr[hd]
            if diagonal:
                s = jnp.where(keep, s, -jnp.inf)
                m_new = jnp.max(s, axis=0, keepdims=True)
                p = jnp.exp(s - m_new)
                l_scr[hd] = jnp.sum(p, axis=0, keepdims=True)
                acc_scr[hd] = jnp.dot(vt, p.astype(BF16), preferred_element_type=F32)
            else:
                m_prev = m_scr[hd]
                m_new = jnp.maximum(m_prev, jnp.max(s, axis=0, keepdims=True))
                alpha = jnp.exp(m_prev - m_new)
                p = jnp.exp(s - m_new)
                l_scr[hd] = alpha * l_scr[hd] + jnp.sum(p, axis=0, keepdims=True)
                acc_scr[hd] = alpha * acc_scr[hd] + jnp.dot(vt, p.astype(BF16), preferred_element_type=F32)
            m_scr[hd] = m_new

    def body(j, carry):
        step(j, False)
        return carry

    step(qi, True)
    lax.fori_loop(0, qi, body, 0)
    for hd in range(n_heads):
        o_ref[hd] = (acc_scr[hd] / l_scr[hd]).astype(BF16)


def _attn_prompt(qt, kc, vt, tq=512):
    b, nh, kw, t = qt.shape
    kv_rank = vt.shape[1]
    tq = min(tq, t)
    assert t % tq == 0 and tq % LANES == 0 and kv_rank % LANES == 0
    return pl.pallas_call(
        functools.partial(_attn_prompt_kernel, tq=tq, n_heads=nh),
        out_shape=jax.ShapeDtypeStruct((b, nh, kv_rank, t), BF16),
        grid=(b, t // tq),
        in_specs=[pl.BlockSpec((None, nh, kw, tq), lambda bi, qi: (bi, 0, 0, qi)),
                  pl.BlockSpec((None, t, kw), lambda bi, qi: (bi, 0, 0)),
                  pl.BlockSpec((None, kv_rank, t), lambda bi, qi: (bi, 0, 0))],
        out_specs=pl.BlockSpec((None, nh, kv_rank, tq), lambda bi, qi: (bi, 0, 0, qi)),
        scratch_shapes=[pltpu.VMEM((nh, 1, tq), F32), pltpu.VMEM((nh, 1, tq), F32),
                        pltpu.VMEM((nh, kv_rank, tq), F32), pltpu.VMEM((nh, tq, tq), F32)],
        compiler_params=_cparams("arbitrary", "arbitrary"),
        name="attn_prompt",
    )(qt, kc, vt)


def _attn_sample_kernel(pt_ref, q_ref, knew_ref, ckv_hbm, krt_hbm, o_ref, kbuf, krbuf, sem,
                        *, layer, n_pages, page, kv_rank, rope_dim, new_len, n_req):
    b = pl.program_id(0)
    slot = lax.rem(b, 2)

    def page_copies(req, slot_, p):
        pid = pt_ref[req, p]
        off = pl.multiple_of(p * page, page)
        return (pltpu.make_async_copy(ckv_hbm.at[layer, pid], kbuf.at[slot_, pl.ds(off, page), :], sem.at[0, slot_]),
                pltpu.make_async_copy(krt_hbm.at[layer, pid], krbuf.at[slot_, :, pl.ds(off, page)], sem.at[1, slot_]))

    def start_all(req, slot_):
        def body(p, carry):
            for cp in page_copies(req, slot_, p):
                cp.start()
            return carry
        lax.fori_loop(0, n_pages, body, 0, unroll=8)

    def wait_all(req, slot_):
        def body(p, carry):
            for cp in page_copies(req, slot_, p):
                cp.wait()
            return carry
        lax.fori_loop(0, n_pages, body, 0, unroll=8)

    @pl.when(b == 0)
    def _():
        start_all(0, 0)

    @pl.when(b + 1 < n_req)
    def _():
        start_all(b + 1, 1 - slot)

    wait_all(b, slot)

    q = q_ref[...]
    rows = q.shape[0]
    kb = kbuf[slot].astype(BF16)
    s = (lax.dot_general(q[:, :kv_rank], kb, _NT, preferred_element_type=F32)
         + jnp.dot(q[:, kv_rank:kv_rank + rope_dim], krbuf[slot].astype(BF16), preferred_element_type=F32))
    kn = knew_ref[...]
    npad = kn.shape[0]
    sn = lax.dot_general(q, kn, _NT, preferred_element_type=F32)
    q_pos = jnp.bitwise_and(lax.broadcasted_iota(jnp.int32, (rows, npad), 0), new_len - 1)
    k_pos = lax.broadcasted_iota(jnp.int32, (rows, npad), 1)
    sn = jnp.where(k_pos <= q_pos, sn, -jnp.inf)
    m = jnp.maximum(jnp.max(s, axis=1, keepdims=True), jnp.max(sn, axis=1, keepdims=True))
    p = jnp.exp(s - m)
    pn = jnp.exp(sn - m)
    denom = jnp.sum(p, axis=1, keepdims=True) + jnp.sum(pn, axis=1, keepdims=True)
    o = (jnp.dot(p.astype(BF16), kb, preferred_element_type=F32)
         + jnp.dot(pn.astype(BF16), kn[:, :kv_rank], preferred_element_type=F32))
    o_ref[...] = (o / denom).astype(BF16)


def _attn_sample(q, k_new, cache_ckv, cache_krt, page_table, layer, kv_rank, rope_dim, new_len):
    bs, rows, kw = q.shape
    n_pages = page_table.shape[1]
    page = cache_ckv.shape[2]
    past = n_pages * page
    new_pad = k_new.shape[1]
    assert new_len & (new_len - 1) == 0 and page % LANES == 0
    kern = functools.partial(_attn_sample_kernel, layer=layer, n_pages=n_pages, page=page, kv_rank=kv_rank,
                             rope_dim=rope_dim, new_len=new_len, n_req=bs)
    return pl.pallas_call(
        kern,
        out_shape=jax.ShapeDtypeStruct((bs, rows, kv_rank), BF16),
        grid_spec=pltpu.PrefetchScalarGridSpec(
            num_scalar_prefetch=1,
            grid=(bs,),
            in_specs=[pl.BlockSpec((None, rows, kw), lambda b, pt: (b, 0, 0)),
                      pl.BlockSpec((None, new_pad, kw), lambda b, pt: (b, 0, 0)),
                      pl.BlockSpec(memory_space=pl.ANY),
                      pl.BlockSpec(memory_space=pl.ANY)],
            out_specs=pl.BlockSpec((None, rows, kv_rank), lambda b, pt: (b, 0, 0)),
            scratch_shapes=[pltpu.VMEM((2, past, kv_rank), F32),
                            pltpu.VMEM((2, rope_dim, past), F32),
                            pltpu.SemaphoreType.DMA((2, 2))]),
        compiler_params=_cparams("arbitrary"),
        name="attn_sample",
    )(page_table, q, k_new, cache_ckv, cache_krt)


def _gelu_tanh(y):
    return 0.5 * y * (1.0 + jnp.tanh(math.sqrt(2.0 / math.pi) * (y + 0.044715 * (y * y * y))))


def _ssm_kernel(u_ref, kmat_ref, bend_ref, cpow_ref, lre_ref, lim_ref, dsk_ref, h0_ref, z_ref, hn_ref,
                h_scr, s_scr, y_scr, *, chunk, n_seq, n_chunks):
    nb, tt, lanes = u_ref.shape
    rpb = tt // chunk
    half = lre_ref.shape[-1]

    @pl.when(pl.program_id(1) == 0)
    def _():
        h_scr[...] = h0_ref[...]

    u_rows = jnp.concatenate(
        [jnp.concatenate([u_ref[b, pl.ds(s, rpb, stride=chunk), :] for s in range(chunk)], axis=1)
         for b in range(nb)], axis=0)
    ub = u_rows.astype(BF16)
    own = jnp.dot(ub, bend_ref[...], preferred_element_type=F32)
    nt = half // lanes
    tile = lambda x, j: x[:, j * lanes:(j + 1) * lanes]

    def seq_rows(q):
        return pl.ds(q, n_chunks, stride=n_seq)

    if n_chunks > 1:
        for q in range(n_seq):
            for j in range(2 * nt):
                s_scr[j, seq_rows(q), :] = tile(own[q * n_chunks:(q + 1) * n_chunks], j)
    else:
        for j in range(2 * nt):
            s_scr[j] = tile(own, j)
    lre = [tile(lre_ref[...], j) for j in range(nt)]
    lim = [tile(lim_ref[...], j) for j in range(nt)]
    h = h_scr[...]
    h_re = [tile(h, j) for j in range(nt)]
    h_im = [tile(h, nt + j) for j in range(nt)]
    for n in range(n_chunks):
        rows = pl.ds(n * n_seq, n_seq)
        for j in range(nt):
            s_re, s_im = s_scr[j, rows, :], s_scr[nt + j, rows, :]
            s_scr[j, rows, :] = h_re[j]
            s_scr[nt + j, rows, :] = h_im[j]
            h_re[j], h_im[j] = (lre[j] * h_re[j] - lim[j] * h_im[j] + s_re,
                                lre[j] * h_im[j] + lim[j] * h_re[j] + s_im)
    h_scr[...] = jnp.concatenate(h_re + h_im, axis=1)
    hn_ref[...] = h_scr[...]
    if n_chunks > 1:
        entering = jnp.concatenate(
            [jnp.concatenate([s_scr[j, seq_rows(q), :] for j in range(2 * nt)], axis=1) for q in range(n_seq)], axis=0)
    else:
        entering = jnp.concatenate([s_scr[j] for j in range(2 * nt)], axis=1)
    y = (jnp.dot(ub, kmat_ref[...], preferred_element_type=F32)
         + jnp.dot(entering.astype(BF16), cpow_ref[...], preferred_element_type=F32))
    for b in range(nb):
        for t in range(chunk):
            y_scr[b, pl.ds(t, rpb, stride=chunk), :] = y[b * rpb:(b + 1) * rpb, t * lanes:(t + 1) * lanes]
    z_ref[...] = _gelu_tanh(y_scr[...] + dsk_ref[...] * u_ref[...]).astype(BF16)


def _ssm(u3, p, layer, h0, chunk, n_seq, tt=512):
    nb, t, d = u3.shape
    n_sg = d // LANES
    tt = min(tt, t)
    rows = nb * tt // chunk
    n_chunks = rows // n_seq
    n_state = h0.shape[-1]
    per_sg = lambda arr: pl.BlockSpec((None, None) + arr.shape[2:], lambda sg, i: (layer, sg, 0, 0))
    tok = pl.BlockSpec((nb, tt, LANES), lambda sg, i: (0, i, sg))
    state = pl.BlockSpec((None, n_seq, n_state), lambda sg, i: (sg, 0, 0))
    ops = [p[k] for k in ("kmat", "bend", "cpow", "lre", "lim", "dsk")]
    return pl.pallas_call(
        functools.partial(_ssm_kernel, chunk=chunk, n_seq=n_seq, n_chunks=n_chunks),
        out_shape=[jax.ShapeDtypeStruct((nb, t, d), BF16), jax.ShapeDtypeStruct(h0.shape, F32)],
        grid=(n_sg, t // tt),
        in_specs=[tok] + [per_sg(a) for a in ops] + [state],
        out_specs=[tok, state],
        scratch_shapes=[pltpu.VMEM((n_seq, n_state), F32), pltpu.VMEM((n_state // LANES, rows, LANES), F32),
                        pltpu.VMEM((nb, tt, LANES), F32)],
        compiler_params=_cparams("arbitrary", "arbitrary"),
        name="ssm",
    )(u3, *ops, h0)


def _ssm_params(a_re, a_im, log_dt, b_re, b_im, c_re, c_im, d_skip, chunk):
    hp = lax.Precision.HIGHEST
    g, n_state = a_re.shape
    cw = SSM_GROUP_WIDTH
    gl = LANES // cw
    n_sg = g // gl
    sw = gl * n_state
    lam = lax.complex(a_re, a_im)
    lam_dt = lam * jnp.exp(log_dt)[:, None]
    b_bar = ((jnp.exp(lam_dt) - 1.0) / lam)[..., None] * lax.complex(b_re, b_im)
    steps = jnp.arange(chunk + 1, dtype=F32)[:, None, None]
    mag = jnp.exp(lam_dt.real.reshape(n_sg, sw)[None] * steps)
    ang = lam_dt.imag.reshape(n_sg, sw)[None] * steps
    pw_re, pw_im = mag * jnp.cos(ang), mag * jnp.sin(ang)

    same_group = (jnp.arange(LANES)[:, None] // cw) == (jnp.arange(sw)[None, :] // n_state)

    def by_channel(x):
        return jnp.where(same_group, jnp.tile(x.reshape(n_sg, LANES, n_state), (1, 1, gl)), 0.0)

    def times_pw(x_re, x_im, p_re, p_im):
        p_re, p_im = p_re[:, :, None], p_im[:, :, None]
        return x_re * p_re - x_im * p_im, x_re * p_im + x_im * p_re

    bt = b_bar.transpose(0, 2, 1)
    c_re_d, c_im_d = by_channel(c_re), by_channel(c_im)
    bp_re, bp_im = times_pw(by_channel(bt.real), by_channel(bt.imag), pw_re[:chunk], pw_im[:chunk])
    lag = (jnp.einsum("kSaq,Sbq->kSab", bp_re, c_re_d, precision=hp)
           - jnp.einsum("kSaq,Sbq->kSab", bp_im, c_im_d, precision=hp))
    t_idx = jnp.arange(chunk)
    diff = t_idx[None, :] - t_idx[:, None]
    kmat = jnp.where((diff >= 0)[:, :, None, None, None], lag[jnp.clip(diff, 0, chunk - 1)], 0.0)
    kmat = kmat.astype(BF16).transpose(2, 0, 3, 1, 4).reshape(n_sg, chunk * LANES, chunk * LANES)
    bend = jnp.concatenate([bp_re[::-1], bp_im[::-1]], axis=-1).astype(BF16)
    bend = bend.transpose(1, 0, 2, 3).reshape(n_sg, chunk * LANES, 2 * sw)
    cp_re, cp_im = times_pw(c_re_d, c_im_d, pw_re[1:], pw_im[1:])
    cpow = jnp.concatenate([cp_re, -cp_im], axis=-1).astype(BF16)
    cpow = cpow.transpose(1, 3, 0, 2).reshape(n_sg, 2 * sw, chunk * LANES)
    return {"kmat": kmat, "bend": bend, "cpow": cpow,
            "lre": pw_re[chunk].reshape(n_sg, 1, sw), "lim": pw_im[chunk].reshape(n_sg, 1, sw),
            "dsk": d_skip.reshape(n_sg, 1, LANES)}


def _pack_state(h_re, h_im):
    b, g, p = h_re.shape
    gl = LANES // SSM_GROUP_WIDTH
    f = lambda h: h.reshape(b, g // gl, gl * p).transpose(1, 0, 2)
    return jnp.concatenate([f(h_re), f(h_im)], axis=-1)


def _unpack_state(h, n_state):
    n_sg, b, w = h.shape
    f = lambda x: x.transpose(1, 0, 2).reshape(b, -1, n_state)
    return f(h[..., :w // 2]), f(h[..., w // 2:])


def _mixer_out_kernel(x_ref, gm_ref, z_ref, o_ref, sga_ref, sgb_ref, wglu_ref, wuv_ref, wout_ref, y_ref, mixed_scr,
                      *, n_heads, v_dim):
    d = x_ref.shape[-1]
    glu = jnp.dot(z_ref[...], wglu_ref[...], preferred_element_type=F32)
    y_ssm = glu[:, :d] * _sigmoid(glu[:, d:])
    mixed_scr[...] = (sgb_ref[...] * y_ssm).astype(mixed_scr.dtype)
    for hd in range(n_heads):
        cols = slice(hd * v_dim, (hd + 1) * v_dim)
        y_attn = lax.dot_general(o_ref[hd], wuv_ref[hd], (((0,), (0,)), ((), ())), preferred_element_type=F32)
        mixed_scr[:, cols] = mixed_scr[:, cols] + sga_ref[:, cols] * y_attn
    proj = jnp.dot(mixed_scr[...].astype(BF16), wout_ref[...], preferred_element_type=F32)
    y_ref[...] = x_ref[...] + gm_ref[...] * proj


def _mixer_out(x, mod, layer, z, o_lat, sga, sgb, w, n_batch, tm=512):
    m, d = x.shape
    rows = m // n_batch
    tm = min(tm, rows)
    tiles = rows // tm
    _, nh, kv_rank, _ = o_lat.shape
    v_dim = w["wuv"].shape[-1]
    row = lambda width: pl.BlockSpec((tm, width), lambda i: (i, 0))
    weights = [w["wglu"], w["wuv"], w["wout"]]
    return pl.pallas_call(
        functools.partial(_mixer_out_kernel, n_heads=nh, v_dim=v_dim),
        out_shape=jax.ShapeDtypeStruct((m, d), F32),
        grid=(m // tm,),
        in_specs=[row(d), mod.spec(5, tm, d), row(d),
                  pl.BlockSpec((None, nh, kv_rank, tm), lambda i: (i // tiles, 0, 0, i % tiles)),
                  row(d), row(d)] + [_of_layer(a, layer) for a in weights],
        out_specs=row(d),
        scratch_shapes=[pltpu.VMEM((tm, d), F32)],
        compiler_params=_cparams("arbitrary"),
        name="mixer_out",
    )(x, mod.arr, z, o_lat, sga, sgb, *weights)


def _layer(x, mod, layer, n_batch, cs_table, w, dims, attend, transposed, ssm_seqs, ssm_chunk, h0, final_w):
    m, d = x.shape
    x = _ffn(x, mod, 0, layer, w["norm_ffn1"], w["ffn1_up"], w["ffn1_down"])
    *qkv, ckv, kr, u, sga, sgb = _mixer_in(x, mod, layer, w["norm_mix"], cs_table, w["mix_in"], dims, n_batch, transposed)
    o_lat = attend(*qkv)
    z, h_n = _ssm(u.reshape(n_batch, m // n_batch, u.shape[-1]), w["ssm"][ssm_chunk], layer, h0, ssm_chunk, ssm_seqs)
    x = _mixer_out(x, mod, layer, z.reshape(m, z.shape[-1]), o_lat, sga, sgb, w["mix_out"], n_batch)
    x = _ffn(x, mod, 6, layer, w["norm_ffn2"], w["ffn2_up"], w["ffn2_down"], final_w=final_w)
    return x, ckv, kr, h_n


def kernel(x_prompt, x_sample, c_prompt, c_sample, cache_ckv, cache_kr, state_ssm_re, state_ssm_im, page_table, ada_w, ada_b, norm_ffn1, ffn1_up, ffn1_down, norm_mix, w_in, q_norm, w_uq, kv_norm, w_uk, w_uv, ssm_a_re, ssm_a_im, ssm_log_dt, ssm_b_re, ssm_b_im, ssm_c_re, ssm_c_im, ssm_d, w_glu, w_out, norm_ffn2, ffn2_up, ffn2_down, final_norm):
    bp, tp, d = x_prompt.shape
    bs, ts, _ = x_sample.shape
    depth = ada_w.shape[0]
    q_rank, kv_rank = q_norm.shape[-1], kv_norm.shape[-1]
    rope_dim = cache_kr.shape[-1]
    nh, nope_dim = w_uk.shape[2], w_uk.shape[3]
    n_groups, n_state = ssm_a_re.shape[1], ssm_a_re.shape[2]
    dims = {"n_heads": nh, "q_rank": q_rank, "kv_rank": kv_rank, "rope_dim": rope_dim, "nope_dim": nope_dim,
            "d_ssm": ssm_d.shape[-1]}
    past_len = page_table.shape[1] * cache_ckv.shape[2]
    sample_chunk = ts
    prompt_chunk = ts if tp % ts == 0 else tp
    cache_krt = jnp.swapaxes(cache_kr, 2, 3)

    ssm_args = (ssm_a_re, ssm_a_im, ssm_log_dt, ssm_b_re, ssm_b_im, ssm_c_re, ssm_c_im, ssm_d)
    w = {"norm_ffn1": norm_ffn1.reshape(depth, 1, d), "norm_mix": norm_mix.reshape(depth, 1, d),
         "norm_ffn2": norm_ffn2.reshape(depth, 1, d),
         "ffn1_up": ffn1_up.astype(BF16), "ffn1_down": ffn1_down.astype(BF16),
         "ffn2_up": ffn2_up.astype(BF16), "ffn2_down": ffn2_down.astype(BF16),
         "mix_in": _mixer_in_weights(w_in, q_norm, w_uq, kv_norm, w_uk, dims),
         "mix_out": {"wglu": w_glu.astype(BF16), "wuv": w_uv.transpose(0, 2, 1, 3).astype(BF16),
                     "wout": w_out.astype(BF16)},
         "ssm": {c: jax.vmap(functools.partial(_ssm_params, chunk=c))(*ssm_args) for c in {prompt_chunk, sample_chunk}}}
    final_w = final_norm.reshape(1, 1, d)

    mod_all = _adaln(jnp.concatenate([c_prompt, c_sample], axis=0), ada_w, ada_b)
    mod_p_arr = mod_all[:, :bp].reshape(depth, bp, N_MODULATIONS, 1, d)
    mod_s_arr = jnp.repeat(mod_all[:, bp:].reshape(depth, bs, N_MODULATIONS, d), ts, axis=1).transpose(0, 2, 1, 3)
    cs_p = _rope_tables(jnp.arange(tp), rope_dim)
    cs_s = jnp.tile(_rope_tables(past_len + jnp.arange(ts), rope_dim), (bs, 1))

    xp = x_prompt.reshape(bp * tp, d)
    xs = x_sample.reshape(bs * ts, d)
    zeros_state = jnp.zeros((bp, n_groups, n_state), F32)
    h0_p = _pack_state(zeros_state, zeros_state)
    outs = {k: [] for k in ("ckv_p", "kr_p", "hre_p", "him_p", "ckv_s", "kr_s", "hre_s", "him_s")}
    for l in range(depth):
        last = final_w if l == depth - 1 else None

        xp, ckv, kr, h_n = _layer(
            xp, _Mod(mod_p_arr, l, False, tp), l, bp, cs_p, w, dims,
            lambda qt, kc, vt: _attn_prompt(qt, kc.reshape(bp, tp, kc.shape[-1]), vt), True, bp, prompt_chunk,
            h0_p, last)
        h_re, h_im = _unpack_state(h_n, n_state)
        outs["ckv_p"].append(ckv.reshape(bp, tp, kv_rank)); outs["kr_p"].append(kr.reshape(bp, tp, rope_dim))
        outs["hre_p"].append(h_re); outs["him_p"].append(h_im)

        def attend_sample(q, kc, l=l):
            kw = q.shape[-1]
            qb = q.reshape(nh, bs, ts, kw).transpose(1, 0, 2, 3).reshape(bs, nh * ts, kw)
            k_new = jnp.pad(kc.reshape(bs, ts, kw), ((0, 0), (0, 16 - ts), (0, 0)))
            o = _attn_sample(qb, k_new, cache_ckv, cache_krt, page_table, l, kv_rank, rope_dim, ts)
            return o.reshape(bs, nh, ts, kv_rank).transpose(1, 3, 0, 2).reshape(1, nh, kv_rank, bs * ts)

        xs, ckv, kr, h_n = _layer(
            xs, _Mod(mod_s_arr, l, True, ts), l, 1, cs_s, w, dims,
            attend_sample, False, bs, sample_chunk, _pack_state(state_ssm_re[l], state_ssm_im[l]), last)
        h_re, h_im = _unpack_state(h_n, n_state)
        outs["ckv_s"].append(ckv.reshape(bs, ts, kv_rank)); outs["kr_s"].append(kr.reshape(bs, ts, rope_dim))
        outs["hre_s"].append(h_re); outs["him_s"].append(h_im)

    st = lambda k: jnp.stack(outs[k])
    return (xp.reshape(bp, tp, d), xs.reshape(bs, ts, d),
            st("ckv_p"), st("kr_p"), st("hre_p"), st("him_p"),
            st("ckv_s"), st("kr_s"), st("hre_s"), st("him_s"))
```

```python
import functools
import math

import jax
import jax.numpy as jnp
from jax import lax
from jax.experimental import pallas as pl
from jax.experimental.pallas import tpu as pltpu

F32 = jnp.float32
BF16 = jnp.bfloat16

NORM_EPS = 1e-6
ROPE_BASE = 10000.0
SSM_GROUP_WIDTH = 16
N_MODULATIONS = 9
LANES = 128
VMEM_LIMIT_BYTES = 56 * 1024 * 1024
FFN_CHUNK = 256


def _cparams(*sem):
    return pltpu.CompilerParams(dimension_semantics=sem, vmem_limit_bytes=VMEM_LIMIT_BYTES)


def _sigmoid(x):
    return 1.0 / (1.0 + jnp.exp(-x))


def _rms(x, w):
    return x * lax.rsqrt(jnp.mean(x * x, axis=-1, keepdims=True) + NORM_EPS) * w


def _of_layer(arr, layer):
    zeros = (0,) * (arr.ndim - 1)
    return pl.BlockSpec((None,) + arr.shape[1:], lambda *_: (layer,) + zeros, pipeline_mode=pl.Buffered(1))


def _adaln_kernel(c_ref, w_ref, b_ref, o_ref):
    c = c_ref[...]
    a = (c * _sigmoid(c)).astype(BF16)
    o_ref[...] = jnp.dot(a, w_ref[...].astype(BF16), preferred_element_type=F32) + b_ref[...]


def _adaln(c_all, ada_w, ada_b, tn=1536):
    depth, d, n = ada_w.shape
    rows = c_all.shape[0]
    return pl.pallas_call(
        _adaln_kernel,
        out_shape=jax.ShapeDtypeStruct((depth, rows, n), F32),
        grid=(depth, n // tn),
        in_specs=[pl.BlockSpec((rows, d), lambda l, j: (0, 0)),
                  pl.BlockSpec((None, d, tn), lambda l, j: (l, 0, j)),
                  pl.BlockSpec((None, 1, tn), lambda l, j: (l, 0, j))],
        out_specs=pl.BlockSpec((None, rows, tn), lambda l, j: (l, 0, j)),
        compiler_params=_cparams("arbitrary", "arbitrary"),
        name="adaln",
    )(c_all, ada_w, ada_b.reshape(depth, 1, n))


class _Mod:
    def __init__(self, arr, layer, per_token, rows_per_batch):
        self.arr = arr
        self.layer = layer
        self.per_token = per_token
        self.rows_per_batch = rows_per_batch

    def spec(self, k, tm, d):
        layer = self.layer
        if self.per_token:
            return pl.BlockSpec((None, None, tm, d), lambda i: (layer, k, i, 0))
        tiles_per_batch = self.rows_per_batch // tm
        return pl.BlockSpec((None, None, None, 1, d), lambda i: (layer, i // tiles_per_batch, k, 0, 0))


def _ffn_kernel(x_ref, sh_ref, sc_ref, g_ref, nw_ref, wup_ref, wdn_ref, *rest, final):
    if final:
        fn_ref, o_ref, h_scr, acc_scr = rest
    else:
        o_ref, h_scr, acc_scr = rest
    d_ff = wdn_ref.shape[0]
    x = x_ref[...]
    h_scr[...] = (_rms(x, nw_ref[...]) * (1.0 + sc_ref[...]) + sh_ref[...]).astype(BF16)

    def down(c):
        cols = slice(c * FFN_CHUNK, (c + 1) * FFN_CHUNK)
        gate_cols = slice(d_ff + c * FFN_CHUNK, d_ff + (c + 1) * FFN_CHUNK)
        h = h_scr[...]
        a = jnp.dot(h, wup_ref[:, cols], preferred_element_type=F32)
        b = jnp.dot(h, wup_ref[:, gate_cols], preferred_element_type=F32)
        act = (a * _sigmoid(a) * b).astype(BF16)
        return jnp.dot(act, wdn_ref[cols, :], preferred_element_type=F32)

    n_chunks = d_ff // FFN_CHUNK
    acc_scr[...] = down(0)
    for c in range(1, n_chunks - 1):
        acc_scr[...] += down(c)
    y = x + (0.5 * g_ref[...]) * (acc_scr[...] + down(n_chunks - 1))
    if final:
        y = _rms(y, fn_ref[...])
    o_ref[...] = y


def _ffn(x, mod, k0, layer, norm_w, w_up, w_down, final_w=None, tm=1024):
    m, d = x.shape
    tm = min(tm, m if mod.per_token else mod.rows_per_batch)
    d_ff = w_down.shape[1]
    assert d_ff % FFN_CHUNK == 0
    final = final_w is not None
    in_specs = [pl.BlockSpec((tm, d), lambda i: (i, 0)),
                mod.spec(k0, tm, d), mod.spec(k0 + 1, tm, d), mod.spec(k0 + 2, tm, d),
                _of_layer(norm_w, layer), _of_layer(w_up, layer), _of_layer(w_down, layer)]
    args = [x, mod.arr, mod.arr, mod.arr, norm_w, w_up, w_down]
    if final:
        in_specs.append(_of_layer(final_w, 0))
        args.append(final_w)
    return pl.pallas_call(
        functools.partial(_ffn_kernel, final=final),
        out_shape=jax.ShapeDtypeStruct((m, d), F32),
        grid=(m // tm,),
        in_specs=in_specs,
        out_specs=pl.BlockSpec((tm, d), lambda i: (i, 0)),
        scratch_shapes=[pltpu.VMEM((tm, d), BF16), pltpu.VMEM((tm, d), F32)],
        compiler_params=_cparams("arbitrary"),
        name="ffn",
    )(*args)


def _mixer_in_kernel(x_ref, sh_ref, sc_ref, nw_ref, cs_ref, wa_ref, wu_ref, qn_ref, kvn_ref, wuq_ref, wuk_ref,
                     q_ref, kc_ref, *rest,
                     n_heads, q_rank, kv_rank, rope_dim, nope_dim, d_ssm, scale, transposed, sub):
    if transposed:
        vt_ref, ckv_ref, kr_ref, u_ref, sga_ref, sgb_ref = rest
    else:
        ckv_ref, kr_ref, u_ref, sga_ref, sgb_ref = rest
    d_model = sga_ref.shape[-1]
    base = n_heads * nope_dim

    def rows_of(ref, r):
        return ref[...] if ref.shape[0] == 1 else ref[r, :]

    for r0 in range(0, x_ref.shape[0], sub):
        r = slice(r0, r0 + sub)
        x = x_ref[r, :]
        h = (_rms(x, nw_ref[...]) * (1.0 + rows_of(sc_ref, r)) + rows_of(sh_ref, r)).astype(BF16)
        cos = cs_ref[r, :LANES]
        sin = cs_ref[r, LANES:]

        def rotate(pair):
            return pair * cos + pltpu.roll(pair, rope_dim, axis=1) * sin

        t = jnp.dot(h, wa_ref[...], preferred_element_type=F32)
        cq = t[:, :q_rank]
        kr = rotate(t[:, q_rank:q_rank + LANES])
        ckv = _rms(t[:, q_rank + LANES:], kvn_ref[...])
        kr_ref[r, :] = kr[:, :rope_dim]
        ckv_ref[r, :] = ckv
        kc_ref[r, :kv_rank] = ckv.astype(BF16)
        kc_ref[r, kv_rank:] = kr.astype(BF16)
        if transposed:
            vt_ref[:, r] = ckv.T.astype(BF16)

        ug = jnp.dot(h, wu_ref[...], preferred_element_type=F32)
        u_ref[r, :] = ug[:, :d_ssm]
        sga_ref[r, :] = _sigmoid(ug[:, d_ssm:d_ssm + d_model])
        sgb_ref[r, :] = _sigmoid(ug[:, d_ssm + d_model:])

        cqn = _rms(cq, qn_ref[...]).astype(BF16)
        q = jnp.dot(cqn, wuq_ref[...], preferred_element_type=F32)
        for hd in range(n_heads):
            nope = q[:, hd * nope_dim:(hd + 1) * nope_dim].astype(BF16)
            q_lat = jnp.dot(nope, wuk_ref[hd], preferred_element_type=F32) * scale
            q_rot = rotate(q[:, base + hd * LANES:base + (hd + 1) * LANES]) * scale
            if transposed:
                q_ref[hd, :kv_rank, r] = q_lat.T.astype(BF16)
                q_ref[hd, kv_rank:, r] = q_rot.T.astype(BF16)
            else:
                q_ref[hd, r, :kv_rank] = q_lat.astype(BF16)
                q_ref[hd, r, kv_rank:] = q_rot.astype(BF16)


def _mixer_in(x, mod, layer, norm_w, cs_table, w, dims, n_batch, transposed, tm=512, sub=256):
    m, d = x.shape
    tm = min(tm, m // n_batch)
    rows = m // n_batch
    tiles = rows // tm
    cs_tiles = cs_table.shape[0] // tm
    nh, q_rank, kv_rank, rope_dim, nope_dim, d_ssm = (dims[k] for k in
                                                       ("n_heads", "q_rank", "kv_rank", "rope_dim", "nope_dim", "d_ssm"))
    kw = kv_rank + LANES
    kern = functools.partial(_mixer_in_kernel, n_heads=nh, q_rank=q_rank, kv_rank=kv_rank, rope_dim=rope_dim,
                             nope_dim=nope_dim, d_ssm=d_ssm, scale=(nope_dim + rope_dim) ** -0.5, transposed=transposed,
                             sub=min(sub, tm))
    row = lambda width: pl.BlockSpec((tm, width), lambda i: (i, 0))
    out_shape = [jax.ShapeDtypeStruct((n_batch, nh, kw, rows) if transposed else (n_batch, nh, rows, kw), BF16),
                 jax.ShapeDtypeStruct((m, kw), BF16),
                 jax.ShapeDtypeStruct((m, kv_rank), F32),
                 jax.ShapeDtypeStruct((m, rope_dim), F32),
                 jax.ShapeDtypeStruct((m, d_ssm), F32),
                 jax.ShapeDtypeStruct((m, d), F32),
                 jax.ShapeDtypeStruct((m, d), F32)]
    out_specs = [pl.BlockSpec((None, nh, kw, tm), lambda i: (i // tiles, 0, 0, i % tiles)) if transposed else
                 pl.BlockSpec((None, nh, tm, kw), lambda i: (i // tiles, 0, i % tiles, 0)),
                 row(kw), row(kv_rank), row(rope_dim), row(d_ssm), row(d), row(d)]
    if transposed:
        out_shape.insert(2, jax.ShapeDtypeStruct((n_batch, kv_rank, rows), BF16))
        out_specs.insert(2, pl.BlockSpec((None, kv_rank, tm), lambda i: (i // tiles, 0, i % tiles)))
    weights = [norm_w, w["wa"], w["wu"], w["qn"], w["kvn"], w["wuq"], w["wuk"]]
    lspec = [_of_layer(a, layer) for a in weights]
    return pl.pallas_call(
        kern,
        out_shape=out_shape,
        grid=(m // tm,),
        in_specs=[row(d), mod.spec(3, tm, d), mod.spec(4, tm, d), lspec[0],
                  pl.BlockSpec((tm, 2 * LANES), lambda i: (i % cs_tiles, 0))] + lspec[1:],
        out_specs=out_specs,
        compiler_params=_cparams("arbitrary"),
        name="mixer_in",
    )(x, mod.arr, mod.arr, norm_w, cs_table, *weights[1:])


def _rope_tables(pos, rope_dim):
    half = rope_dim // 2
    inv = ROPE_BASE ** (-jnp.arange(half, dtype=F32) / half)
    ang = pos.astype(F32)[:, None] * inv[None, :]
    cos, sin = jnp.cos(ang), jnp.sin(ang)
    pad = jnp.zeros((pos.shape[0], LANES - rope_dim), F32)
    return jnp.concatenate([cos, cos, pad, -sin, sin, pad], axis=-1)


def _swap_halves(w, rope_dim):
    half = rope_dim // 2
    return jnp.concatenate([w[..., half:], w[..., :half]], axis=-1)


def _pad_lanes(w):
    return jnp.pad(w, [(0, 0)] * (w.ndim - 1) + [(0, LANES - w.shape[-1])])


def _mixer_in_weights(w_in, q_norm, w_uq, kv_norm, w_uk, dims):
    nh, q_rank, kv_rank, rope_dim, nope_dim, d_ssm = (dims[k] for k in
                                                       ("n_heads", "q_rank", "kv_rank", "rope_dim", "nope_dim", "d_ssm"))
    depth = w_in.shape[0]
    o1, o2 = q_rank + kv_rank, q_rank + kv_rank + rope_dim
    w_cq, w_ckv, w_kr, w_rest = w_in[..., :q_rank], w_in[..., q_rank:o1], w_in[..., o1:o2], w_in[..., o2:]
    assert 2 * rope_dim == LANES
    wa = jnp.concatenate([w_cq, w_kr, _swap_halves(w_kr, rope_dim), w_ckv], axis=-1)
    wq = w_uq.reshape(depth, q_rank, nh, nope_dim + rope_dim)
    wq_nope = wq[..., :nope_dim].reshape(depth, q_rank, nh * nope_dim)
    wq_rope = wq[..., nope_dim:]
    wq_pair = jnp.concatenate([wq_rope, _swap_halves(wq_rope, rope_dim)], axis=-1).reshape(depth, q_rank, nh * LANES)
    return {"wa": wa.astype(BF16), "wu": w_rest.astype(BF16),
            "qn": q_norm.reshape(depth, 1, q_rank), "kvn": kv_norm.reshape(depth, 1, kv_rank),
            "wuq": jnp.concatenate([wq_nope, wq_pair], axis=-1).astype(BF16),
            "wuk": w_uk.transpose(0, 2, 3, 1).astype(BF16)}


_NT = (((1,), (1,)), ((), ()))


def _attn_prompt_kernel(qt_ref, k_ref, vt_ref, o_ref, m_scr, l_scr, acc_scr, s_scr, *, tq, n_heads):
    qi = pl.program_id(1)
    half = tq // 2

    def block(key0, n_keys, q_lanes, first, triangular):
        keys = pl.ds(pl.multiple_of(key0, n_keys), n_keys)
        k = k_ref[keys, :]
        vt = vt_ref[:, keys]
        n_q = q_lanes.stop - q_lanes.start
        if triangular:
            keep = (lax.broadcasted_iota(jnp.int32, (n_keys, n_q), 0) <= lax.broadcasted_iota(jnp.int32, (n_keys, n_q), 1))
        for hd in range(n_heads):
            s_scr[hd, :n_keys, q_lanes] = jnp.dot(k, qt_ref[hd, :, q_lanes], preferred_element_type=F32)
        for hd in range(n_heads):
            s = s_scr[hd, :n_keys, q_lanes]
            if triangular:
                s = jnp.where(keep, s, -jnp.inf)
            if first:
                m_new = jnp.max(s, axis=0, keepdims=True)
                p = jnp.exp(s - m_new)
                l_scr[hd, :, q_lanes] = jnp.sum(p, axis=0, keepdims=True)
                acc_scr[hd, :, q_lanes] = jnp.dot(vt, p.astype(BF16), preferred_element_type=F32)
            else:
                m_prev = m_scr[hd, :, q_lanes]
                m_new = jnp.maximum(m_prev, jnp.max(s, axis=0, keepdims=True))
                alpha = jnp.exp(m_prev - m_new)
                p = jnp.exp(s - m_new)
                l_scr[hd, :, q_lanes] = alpha * l_scr[hd, :, q_lanes] + jnp.sum(p, axis=0, keepdims=True)
                acc_scr[hd, :, q_lanes] = (alpha * acc_scr[hd, :, q_lanes]
                                           + jnp.dot(vt, p.astype(BF16), preferred_element_type=F32))
            m_scr[hd, :, q_lanes] = m_new

    def body(j, carry):
        block(pl.multiple_of(j * tq, tq), tq, slice(0, tq), False, False)
        return carry

    own = pl.multiple_of(qi * tq, tq)
    block(own, half, slice(0, tq), True, True)
    block(own + half, half, slice(half, tq), False, True)
    lax.fori_loop(0, qi, body, 0)
    for hd in range(n_heads):
        o_ref[hd] = (acc_scr[hd] / l_scr[hd]).astype(BF16)


def _attn_prompt(qt, kc, vt, tq=512):
    b, nh, kw, t = qt.shape
    kv_rank = vt.shape[1]
    tq = min(tq, t)
    assert t % tq == 0 and tq % (2 * LANES) == 0 and kv_rank % LANES == 0
    return pl.pallas_call(
        functools.partial(_attn_prompt_kernel, tq=tq, n_heads=nh),
        out_shape=jax.ShapeDtypeStruct((b, nh, kv_rank, t), BF16),
        grid=(b, t // tq),
        in_specs=[pl.BlockSpec((None, nh, kw, tq), lambda bi, qi: (bi, 0, 0, qi)),
                  pl.BlockSpec((None, t, kw), lambda bi, qi: (bi, 0, 0)),
                  pl.BlockSpec((None, kv_rank, t), lambda bi, qi: (bi, 0, 0))],
        out_specs=pl.BlockSpec((None, nh, kv_rank, tq), lambda bi, qi: (bi, 0, 0, qi)),
        scratch_shapes=[pltpu.VMEM((nh, 1, tq), F32), pltpu.VMEM((nh, 1, tq), F32),
                        pltpu.VMEM((nh, kv_rank, tq), F32), pltpu.VMEM((nh, tq, tq), F32)],
        compiler_params=_cparams("arbitrary", "arbitrary"),
        name="attn_prompt",
    )(qt, kc, vt)


def _attn_sample_kernel(pt_ref, q_ref, knew_ref, ckv_hbm, krt_hbm, o_ref, kbuf, krbuf, sem,
                        *, layer, n_pages, page, kv_rank, rope_dim, new_len, n_req):
    b = pl.program_id(0)
    slot = lax.rem(b, 2)

    def page_copies(req, slot_, p):
        pid = pt_ref[req, p]
        off = pl.multiple_of(p * page, page)
        return (pltpu.make_async_copy(ckv_hbm.at[layer, pid], kbuf.at[slot_, pl.ds(off, page), :], sem.at[0, slot_]),
                pltpu.make_async_copy(krt_hbm.at[layer, pid], krbuf.at[slot_, :, pl.ds(off, page)], sem.at[1, slot_]))

    def start_all(req, slot_):
        def body(p, carry):
            for cp in page_copies(req, slot_, p):
                cp.start()
            return carry
        lax.fori_loop(0, n_pages, body, 0, unroll=8)

    def wait_all(slot_):
        pltpu.make_async_copy(kbuf.at[slot_], kbuf.at[slot_], sem.at[0, slot_]).wait()
        pltpu.make_async_copy(krbuf.at[slot_], krbuf.at[slot_], sem.at[1, slot_]).wait()

    @pl.when(b == 0)
    def _():
        start_all(0, 0)

    @pl.when(b + 1 < n_req)
    def _():
        start_all(b + 1, 1 - slot)

    wait_all(slot)

    q = q_ref[...]
    rows = q.shape[0]
    kb = kbuf[slot].astype(BF16)
    s = (lax.dot_general(q[:, :kv_rank], kb, _NT, preferred_element_type=F32)
         + jnp.dot(q[:, kv_rank:kv_rank + rope_dim], krbuf[slot].astype(BF16), preferred_element_type=F32))
    kn = knew_ref[...]
    npad = kn.shape[0]
    sn = lax.dot_general(q, kn, _NT, preferred_element_type=F32)
    q_pos = jnp.bitwise_and(lax.broadcasted_iota(jnp.int32, (rows, npad), 0), new_len - 1)
    k_pos = lax.broadcasted_iota(jnp.int32, (rows, npad), 1)
    sn = jnp.where(k_pos <= q_pos, sn, -jnp.inf)
    m = jnp.maximum(jnp.max(s, axis=1, keepdims=True), jnp.max(sn, axis=1, keepdims=True))
    p = jnp.exp(s - m)
    pn = jnp.exp(sn - m)
    denom = jnp.sum(p, axis=1, keepdims=True) + jnp.sum(pn, axis=1, keepdims=True)
    o = (jnp.dot(p.astype(BF16), kb, preferred_element_type=F32)
         + jnp.dot(pn.astype(BF16), kn[:, :kv_rank], preferred_element_type=F32))
    o_ref[...] = (o / denom).astype(BF16)


def _attn_sample(q, k_new, cache_ckv, cache_krt, page_table, layer, kv_rank, rope_dim, new_len):
    bs, rows, kw = q.shape
    n_pages = page_table.shape[1]
    page = cache_ckv.shape[2]
    past = n_pages * page
    new_pad = k_new.shape[1]
    assert new_len & (new_len - 1) == 0 and page % LANES == 0
    kern = functools.partial(_attn_sample_kernel, layer=layer, n_pages=n_pages, page=page, kv_rank=kv_rank,
                             rope_dim=rope_dim, new_len=new_len, n_req=bs)
    return pl.pallas_call(
        kern,
        out_shape=jax.ShapeDtypeStruct((bs, rows, kv_rank), BF16),
        grid_spec=pltpu.PrefetchScalarGridSpec(
            num_scalar_prefetch=1,
            grid=(bs,),
            in_specs=[pl.BlockSpec((None, rows, kw), lambda b, pt: (b, 0, 0)),
                      pl.BlockSpec((None, new_pad, kw), lambda b, pt: (b, 0, 0)),
                      pl.BlockSpec(memory_space=pl.ANY),
                      pl.BlockSpec(memory_space=pl.ANY)],
            out_specs=pl.BlockSpec((None, rows, kv_rank), lambda b, pt: (b, 0, 0)),
            scratch_shapes=[pltpu.VMEM((2, past, kv_rank), F32),
                            pltpu.VMEM((2, rope_dim, past), F32),
                            pltpu.SemaphoreType.DMA((2, 2))]),
        compiler_params=_cparams("arbitrary"),
        name="attn_sample",
    )(page_table, q, k_new, cache_ckv, cache_krt)


def _gelu_tanh(y):
    return 0.5 * y * (1.0 + jnp.tanh(math.sqrt(2.0 / math.pi) * (y + 0.044715 * (y * y * y))))


def _ssm_kernel(u_ref, kmat_ref, bend_ref, cpow_ref, lre_ref, lim_ref, dsk_ref, h0_ref, z_ref, hn_ref,
                h_scr, s_scr, y_scr, *, chunk, n_seq, n_chunks):
    nb, tt, lanes = u_ref.shape
    rpb = tt // chunk
    half = lre_ref.shape[-1]

    @pl.when(pl.program_id(1) == 0)
    def _():
        h_scr[...] = h0_ref[...]

    u_rows = jnp.concatenate(
        [jnp.concatenate([u_ref[b, pl.ds(s, rpb, stride=chunk), :] for s in range(chunk)], axis=1)
         for b in range(nb)], axis=0)
    ub = u_rows.astype(BF16)
    own = jnp.dot(ub, bend_ref[...], preferred_element_type=F32)
    nt = half // lanes
    tile = lambda x, j: x[:, j * lanes:(j + 1) * lanes]

    def seq_rows(q):
        return pl.ds(q, n_chunks, stride=n_seq)

    if n_chunks > 1:
        for q in range(n_seq):
            for j in range(2 * nt):
                s_scr[j, seq_rows(q), :] = tile(own[q * n_chunks:(q + 1) * n_chunks], j)
    else:
        for j in range(2 * nt):
            s_scr[j] = tile(own, j)
    lre = [tile(lre_ref[...], j) for j in range(nt)]
    lim = [tile(lim_ref[...], j) for j in range(nt)]
    h = h_scr[...]
    h_re = [tile(h, j) for j in range(nt)]
    h_im = [tile(h, nt + j) for j in range(nt)]
    for n in range(n_chunks):
        rows = pl.ds(n * n_seq, n_seq)
        for j in range(nt):
            s_re, s_im = s_scr[j, rows, :], s_scr[nt + j, rows, :]
            s_scr[j, rows, :] = h_re[j]
            s_scr[nt + j, rows, :] = h_im[j]
            h_re[j], h_im[j] = (lre[j] * h_re[j] - lim[j] * h_im[j] + s_re,
                                lre[j] * h_im[j] + lim[j] * h_re[j] + s_im)
    h_scr[...] = jnp.concatenate(h_re + h_im, axis=1)
    hn_ref[...] = h_scr[...]
    if n_chunks > 1:
        entering = jnp.concatenate(
            [jnp.concatenate([s_scr[j, seq_rows(q), :] for j in range(2 * nt)], axis=1) for q in range(n_seq)], axis=0)
    else:
        entering = jnp.concatenate([s_scr[j] for j in range(2 * nt)], axis=1)
    y = (jnp.dot(ub, kmat_ref[...], preferred_element_type=F32)
         + jnp.dot(entering.astype(BF16), cpow_ref[...], preferred_element_type=F32))
    for b in range(nb):
        for t in range(chunk):
            y_scr[b, pl.ds(t, rpb, stride=chunk), :] = y[b * rpb:(b + 1) * rpb, t * lanes:(t + 1) * lanes]
    z_ref[...] = _gelu_tanh(y_scr[...] + dsk_ref[...] * u_ref[...]).astype(BF16)


def _ssm(u3, p, layer, h0, chunk, n_seq, tt=512):
    nb, t, d = u3.shape
    n_sg = d // LANES
    tt = min(tt, t)
    rows = nb * tt // chunk
    n_chunks = rows // n_seq
    n_state = h0.shape[-1]
    per_sg = lambda arr: pl.BlockSpec((None, None) + arr.shape[2:], lambda sg, i: (layer, sg, 0, 0))
    tok = pl.BlockSpec((nb, tt, LANES), lambda sg, i: (0, i, sg))
    state = pl.BlockSpec((None, n_seq, n_state), lambda sg, i: (sg, 0, 0))
    ops = [p[k] for k in ("kmat", "bend", "cpow", "lre", "lim", "dsk")]
    return pl.pallas_call(
        functools.partial(_ssm_kernel, chunk=chunk, n_seq=n_seq, n_chunks=n_chunks),
        out_shape=[jax.ShapeDtypeStruct((nb, t, d), BF16), jax.ShapeDtypeStruct(h0.shape, F32)],
        grid=(n_sg, t // tt),
        in_specs=[tok] + [per_sg(a) for a in ops] + [state],
        out_specs=[tok, state],
        scratch_shapes=[pltpu.VMEM((n_seq, n_state), F32), pltpu.VMEM((n_state // LANES, rows, LANES), F32),
                        pltpu.VMEM((nb, tt, LANES), F32)],
        compiler_params=_cparams("arbitrary", "arbitrary"),
        name="ssm",
    )(u3, *ops, h0)


def _ssm_params(a_re, a_im, log_dt, b_re, b_im, c_re, c_im, d_skip, chunk):
    hp = lax.Precision.HIGHEST
    g, n_state = a_re.shape
    cw = SSM_GROUP_WIDTH
    gl = LANES // cw
    n_sg = g // gl
    sw = gl * n_state
    lam = lax.complex(a_re, a_im)
    lam_dt = lam * jnp.exp(log_dt)[:, None]
    b_bar = ((jnp.exp(lam_dt) - 1.0) / lam)[..., None] * lax.complex(b_re, b_im)
    steps = jnp.arange(chunk + 1, dtype=F32)[:, None, None]
    mag = jnp.exp(lam_dt.real.reshape(n_sg, sw)[None] * steps)
    ang = lam_dt.imag.reshape(n_sg, sw)[None] * steps
    pw_re, pw_im = mag * jnp.cos(ang), mag * jnp.sin(ang)

    same_group = (jnp.arange(LANES)[:, None] // cw) == (jnp.arange(sw)[None, :] // n_state)

    def by_channel(x):
        return jnp.where(same_group, jnp.tile(x.reshape(n_sg, LANES, n_state), (1, 1, gl)), 0.0)

    def times_pw(x_re, x_im, p_re, p_im):
        p_re, p_im = p_re[:, :, None], p_im[:, :, None]
        return x_re * p_re - x_im * p_im, x_re * p_im + x_im * p_re

    bt = b_bar.transpose(0, 2, 1)
    c_re_d, c_im_d = by_channel(c_re), by_channel(c_im)
    bp_re, bp_im = times_pw(by_channel(bt.real), by_channel(bt.imag), pw_re[:chunk], pw_im[:chunk])
    lag = (jnp.einsum("kSaq,Sbq->kSab", bp_re, c_re_d, precision=hp)
           - jnp.einsum("kSaq,Sbq->kSab", bp_im, c_im_d, precision=hp))
    t_idx = jnp.arange(chunk)
    diff = t_idx[None, :] - t_idx[:, None]
    kmat = jnp.where((diff >= 0)[:, :, None, None, None], lag[jnp.clip(diff, 0, chunk - 1)], 0.0)
    kmat = kmat.astype(BF16).transpose(2, 0, 3, 1, 4).reshape(n_sg, chunk * LANES, chunk * LANES)
    bend = jnp.concatenate([bp_re[::-1], bp_im[::-1]], axis=-1).astype(BF16)
    bend = bend.transpose(1, 0, 2, 3).reshape(n_sg, chunk * LANES, 2 * sw)
    cp_re, cp_im = times_pw(c_re_d, c_im_d, pw_re[1:], pw_im[1:])
    cpow = jnp.concatenate([cp_re, -cp_im], axis=-1).astype(BF16)
    cpow = cpow.transpose(1, 3, 0, 2).reshape(n_sg, 2 * sw, chunk * LANES)
    return {"kmat": kmat, "bend": bend, "cpow": cpow,
            "lre": pw_re[chunk].reshape(n_sg, 1, sw), "lim": pw_im[chunk].reshape(n_sg, 1, sw),
            "dsk": d_skip.reshape(n_sg, 1, LANES)}


def _pack_state(h_re, h_im):
    b, g, p = h_re.shape
    gl = LANES // SSM_GROUP_WIDTH
    f = lambda h: h.reshape(b, g // gl, gl * p).transpose(1, 0, 2)
    return jnp.concatenate([f(h_re), f(h_im)], axis=-1)


def _unpack_state(h, n_state):
    n_sg, b, w = h.shape
    f = lambda x: x.transpose(1, 0, 2).reshape(b, -1, n_state)
    return f(h[..., :w // 2]), f(h[..., w // 2:])


def _mixer_out_kernel(x_ref, gm_ref, z_ref, o_ref, sga_ref, sgb_ref, wglu_ref, wuv_ref, wout_ref, y_ref, mixed_scr,
                      *, n_heads, v_dim):
    d = x_ref.shape[-1]
    glu = jnp.dot(z_ref[...], wglu_ref[...], preferred_element_type=F32)
    y_ssm = glu[:, :d] * _sigmoid(glu[:, d:])
    mixed_scr[...] = (sgb_ref[...] * y_ssm).astype(mixed_scr.dtype)
    for hd in range(n_heads):
        cols = slice(hd * v_dim, (hd + 1) * v_dim)
        y_attn = lax.dot_general(o_ref[hd], wuv_ref[hd], (((0,), (0,)), ((), ())), preferred_element_type=F32)
        mixed_scr[:, cols] = mixed_scr[:, cols] + sga_ref[:, cols] * y_attn
    proj = jnp.dot(mixed_scr[...].astype(BF16), wout_ref[...], preferred_element_type=F32)
    y_ref[...] = x_ref[...] + gm_ref[...] * proj


def _mixer_out(x, mod, layer, z, o_lat, sga, sgb, w, n_batch, tm=512):
    m, d = x.shape
    rows = m // n_batch
    tm = min(tm, rows)
    tiles = rows // tm
    _, nh, kv_rank, _ = o_lat.shape
    v_dim = w["wuv"].shape[-1]
    row = lambda width: pl.BlockSpec((tm, width), lambda i: (i, 0))
    weights = [w["wglu"], w["wuv"], w["wout"]]
    return pl.pallas_call(
        functools.partial(_mixer_out_kernel, n_heads=nh, v_dim=v_dim),
        out_shape=jax.ShapeDtypeStruct((m, d), F32),
        grid=(m // tm,),
        in_specs=[row(d), mod.spec(5, tm, d), row(d),
                  pl.BlockSpec((None, nh, kv_rank, tm), lambda i: (i // tiles, 0, 0, i % tiles)),
                  row(d), row(d)] + [_of_layer(a, layer) for a in weights],
        out_specs=row(d),
        scratch_shapes=[pltpu.VMEM((tm, d), F32)],
        compiler_params=_cparams("arbitrary"),
        name="mixer_out",
    )(x, mod.arr, z, o_lat, sga, sgb, *weights)


def _layer(x, mod, layer, n_batch, cs_table, w, dims, attend, transposed, ssm_seqs, ssm_chunk, h0, final_w):
    m, d = x.shape
    x = _ffn(x, mod, 0, layer, w["norm_ffn1"], w["ffn1_up"], w["ffn1_down"])
    *qkv, ckv, kr, u, sga, sgb = _mixer_in(x, mod, layer, w["norm_mix"], cs_table, w["mix_in"], dims, n_batch, transposed)
    o_lat = attend(*qkv)
    z, h_n = _ssm(u.reshape(n_batch, m // n_batch, u.shape[-1]), w["ssm"][ssm_chunk], layer, h0, ssm_chunk, ssm_seqs)
    x = _mixer_out(x, mod, layer, z.reshape(m, z.shape[-1]), o_lat, sga, sgb, w["mix_out"], n_batch)
    x = _ffn(x, mod, 6, layer, w["norm_ffn2"], w["ffn2_up"], w["ffn2_down"], final_w=final_w)
    return x, ckv, kr, h_n


def kernel(x_prompt, x_sample, c_prompt, c_sample, cache_ckv, cache_kr, state_ssm_re, state_ssm_im, page_table, ada_w, ada_b, norm_ffn1, ffn1_up, ffn1_down, norm_mix, w_in, q_norm, w_uq, kv_norm, w_uk, w_uv, ssm_a_re, ssm_a_im, ssm_log_dt, ssm_b_re, ssm_b_im, ssm_c_re, ssm_c_im, ssm_d, w_glu, w_out, norm_ffn2, ffn2_up, ffn2_down, final_norm):
    bp, tp, d = x_prompt.shape
    bs, ts, _ = x_sample.shape
    depth = ada_w.shape[0]
    q_rank, kv_rank = q_norm.shape[-1], kv_norm.shape[-1]
    rope_dim = cache_kr.shape[-1]
    nh, nope_dim = w_uk.shape[2], w_uk.shape[3]
    n_groups, n_state = ssm_a_re.shape[1], ssm_a_re.shape[2]
    dims = {"n_heads": nh, "q_rank": q_rank, "kv_rank": kv_rank, "rope_dim": rope_dim, "nope_dim": nope_dim,
            "d_ssm": ssm_d.shape[-1]}
    past_len = page_table.shape[1] * cache_ckv.shape[2]
    sample_chunk = ts
    prompt_chunk = ts if tp % ts == 0 else tp
    cache_krt = jnp.swapaxes(cache_kr, 2, 3)

    ssm_args = (ssm_a_re, ssm_a_im, ssm_log_dt, ssm_b_re, ssm_b_im, ssm_c_re, ssm_c_im, ssm_d)
    w = {"norm_ffn1": norm_ffn1.reshape(depth, 1, d), "norm_mix": norm_mix.reshape(depth, 1, d),
         "norm_ffn2": norm_ffn2.reshape(depth, 1, d),
         "ffn1_up": ffn1_up.astype(BF16), "ffn1_down": ffn1_down.astype(BF16),
         "ffn2_up": ffn2_up.astype(BF16), "ffn2_down": ffn2_down.astype(BF16),
         "mix_in": _mixer_in_weights(w_in, q_norm, w_uq, kv_norm, w_uk, dims),
         "mix_out": {"wglu": w_glu.astype(BF16), "wuv": w_uv.transpose(0, 2, 1, 3).astype(BF16),
                     "wout": w_out.astype(BF16)},
         "ssm": {c: jax.vmap(functools.partial(_ssm_params, chunk=c))(*ssm_args) for c in {prompt_chunk, sample_chunk}}}
    final_w = final_norm.reshape(1, 1, d)

    mod_all = _adaln(jnp.concatenate([c_prompt, c_sample], axis=0), ada_w, ada_b)
    mod_p_arr = mod_all[:, :bp].reshape(depth, bp, N_MODULATIONS, 1, d)
    mod_s_arr = jnp.repeat(mod_all[:, bp:].reshape(depth, bs, N_MODULATIONS, d), ts, axis=1).transpose(0, 2, 1, 3)
    cs_p = _rope_tables(jnp.arange(tp), rope_dim)
    cs_s = jnp.tile(_rope_tables(past_len + jnp.arange(ts), rope_dim), (bs, 1))

    xp = x_prompt.reshape(bp * tp, d)
    xs = x_sample.reshape(bs * ts, d)
    zeros_state = jnp.zeros((bp, n_groups, n_state), F32)
    h0_p = _pack_state(zeros_state, zeros_state)
    outs = {k: [] for k in ("ckv_p", "kr_p", "hre_p", "him_p", "ckv_s", "kr_s", "hre_s", "him_s")}
    for l in range(depth):
        last = final_w if l == depth - 1 else None

        xp, ckv, kr, h_n = _layer(
            xp, _Mod(mod_p_arr, l, False, tp), l, bp, cs_p, w, dims,
            lambda qt, kc, vt: _attn_prompt(qt, kc.reshape(bp, tp, kc.shape[-1]), vt), True, bp, prompt_chunk,
            h0_p, last)
        h_re, h_im = _unpack_state(h_n, n_state)
        outs["ckv_p"].append(ckv.reshape(bp, tp, kv_rank)); outs["kr_p"].append(kr.reshape(bp, tp, rope_dim))
        outs["hre_p"].append(h_re); outs["him_p"].append(h_im)

        def attend_sample(q, kc, l=l):
            kw = q.shape[-1]
            qb = q.reshape(nh, bs, ts, kw).transpose(1, 0, 2, 3).reshape(bs, nh * ts, kw)
            k_new = jnp.pad(kc.reshape(bs, ts, kw), ((0, 0), (0, 16 - ts), (0, 0)))
            o = _attn_sample(qb, k_new, cache_ckv, cache_krt, page_table, l, kv_rank, rope_dim, ts)
            return o.reshape(bs, nh, ts, kv_rank).transpose(1, 3, 0, 2).reshape(1, nh, kv_rank, bs * ts)

        xs, ckv, kr, h_n = _layer(
            xs, _Mod(mod_s_arr, l, True, ts), l, 1, cs_s, w, dims,
            attend_sample, False, bs, sample_chunk, _pack_state(state_ssm_re[l], state_ssm_im[l]), last)
        h_re, h_im = _unpack_state(h_n, n_state)
        outs["ckv_s"].append(ckv.reshape(bs, ts, kv_rank)); outs["kr_s"].append(kr.reshape(bs, ts, rope_dim))
        outs["hre_s"].append(h_re); outs["him_s"].append(h_im)

    st = lambda k: jnp.stack(outs[k])
    return (xp.reshape(bp, tp, d), xs.reshape(bs, ts, d),
            st("ckv_p"), st("kr_p"), st("hre_p"), st("him_p"),
            st("ckv_s"), st("kr_s"), st("hre_s"), st("him_s"))
```

```python
import functools
import math

import jax
import jax.numpy as jnp
from jax import lax
from jax.experimental import pallas as pl
from jax.experimental.pallas import tpu as pltpu

F32 = jnp.float32
BF16 = jnp.bfloat16

NORM_EPS = 1e-6
ROPE_BASE = 10000.0
SSM_GROUP_WIDTH = 16
N_MODULATIONS = 9
LANES = 128
VMEM_LIMIT_BYTES = 56 * 1024 * 1024
FFN_CHUNK = 256


def _cparams(*sem):
    return pltpu.CompilerParams(dimension_semantics=sem, vmem_limit_bytes=VMEM_LIMIT_BYTES)


def _sigmoid(x):
    return 1.0 / (1.0 + jnp.exp(-x))


def _rms(x, w):
    return x * lax.rsqrt(jnp.mean(x * x, axis=-1, keepdims=True) + NORM_EPS) * w


def _of_layer(arr, layer):
    zeros = (0,) * (arr.ndim - 1)
    return pl.BlockSpec((None,) + arr.shape[1:], lambda *_: (layer,) + zeros, pipeline_mode=pl.Buffered(1))


def _adaln_kernel(c_ref, w_ref, b_ref, o_ref):
    c = c_ref[...]
    a = (c * _sigmoid(c)).astype(BF16)
    o_ref[...] = jnp.dot(a, w_ref[...].astype(BF16), preferred_element_type=F32) + b_ref[...]


def _adaln(c_all, ada_w, ada_b, tn=1536):
    depth, d, n = ada_w.shape
    rows = c_all.shape[0]
    return pl.pallas_call(
        _adaln_kernel,
        out_shape=jax.ShapeDtypeStruct((depth, rows, n), F32),
        grid=(depth, n // tn),
        in_specs=[pl.BlockSpec((rows, d), lambda l, j: (0, 0)),
                  pl.BlockSpec((None, d, tn), lambda l, j: (l, 0, j)),
                  pl.BlockSpec((None, 1, tn), lambda l, j: (l, 0, j))],
        out_specs=pl.BlockSpec((None, rows, tn), lambda l, j: (l, 0, j)),
        compiler_params=_cparams("arbitrary", "arbitrary"),
        name="adaln",
    )(c_all, ada_w, ada_b.reshape(depth, 1, n))


class _Mod:
    def __init__(self, arr, layer, per_token, rows_per_batch):
        self.arr = arr
        self.layer = layer
        self.per_token = per_token
        self.rows_per_batch = rows_per_batch

    def spec(self, k, tm, d):
        layer = self.layer
        if self.per_token:
            return pl.BlockSpec((None, tm, d), lambda i: (layer, i, k))
        tiles_per_batch = self.rows_per_batch // tm
        return pl.BlockSpec((None, None, None, 1, d), lambda i: (layer, i // tiles_per_batch, k, 0, 0))


def _ffn_kernel(x_ref, sh_ref, sc_ref, g_ref, nw_ref, wup_ref, wdn_ref, *rest, final):
    if final:
        fn_ref, o_ref, h_scr, acc_scr = rest
    else:
        o_ref, h_scr, acc_scr = rest
    d_ff = wdn_ref.shape[0]
    x = x_ref[...]
    h_scr[...] = (_rms(x, nw_ref[...]) * (1.0 + sc_ref[...]) + sh_ref[...]).astype(BF16)

    def down(c):
        cols = slice(c * FFN_CHUNK, (c + 1) * FFN_CHUNK)
        gate_cols = slice(d_ff + c * FFN_CHUNK, d_ff + (c + 1) * FFN_CHUNK)
        h = h_scr[...]
        a = jnp.dot(h, wup_ref[:, cols], preferred_element_type=F32)
        b = jnp.dot(h, wup_ref[:, gate_cols], preferred_element_type=F32)
        act = (a * _sigmoid(a) * b).astype(BF16)
        return jnp.dot(act, wdn_ref[cols, :], preferred_element_type=F32)

    n_chunks = d_ff // FFN_CHUNK
    acc_scr[...] = down(0)
    for c in range(1, n_chunks - 1):
        acc_scr[...] += down(c)
    y = x + (0.5 * g_ref[...]) * (acc_scr[...] + down(n_chunks - 1))
    if final:
        y = _rms(y, fn_ref[...])
    o_ref[...] = y


def _ffn(x, mod, k0, layer, norm_w, w_up, w_down, final_w=None, tm=1024):
    m, d = x.shape
    tm = min(tm, m if mod.per_token else mod.rows_per_batch)
    d_ff = w_down.shape[1]
    assert d_ff % FFN_CHUNK == 0
    final = final_w is not None
    in_specs = [pl.BlockSpec((tm, d), lambda i: (i, 0)),
                mod.spec(k0, tm, d), mod.spec(k0 + 1, tm, d), mod.spec(k0 + 2, tm, d),
                _of_layer(norm_w, layer), _of_layer(w_up, layer), _of_layer(w_down, layer)]
    args = [x, mod.arr, mod.arr, mod.arr, norm_w, w_up, w_down]
    if final:
        in_specs.append(_of_layer(final_w, 0))
        args.append(final_w)
    return pl.pallas_call(
        functools.partial(_ffn_kernel, final=final),
        out_shape=jax.ShapeDtypeStruct((m, d), F32),
        grid=(m // tm,),
        in_specs=in_specs,
        out_specs=pl.BlockSpec((tm, d), lambda i: (i, 0)),
        scratch_shapes=[pltpu.VMEM((tm, d), BF16), pltpu.VMEM((tm, d), F32)],
        compiler_params=_cparams("arbitrary"),
        name="ffn",
    )(*args)


def _mixer_in_kernel(x_ref, sh_ref, sc_ref, nw_ref, cs_ref, wa_ref, wu_ref, qn_ref, kvn_ref, wuq_ref, wuk_ref,
                     q_ref, kc_ref, *rest,
                     n_heads, q_rank, kv_rank, rope_dim, nope_dim, d_ssm, scale, transposed, sub):
    if transposed:
        vt_ref, ckv_ref, kr_ref, u_ref, sga_ref, sgb_ref = rest
    else:
        ckv_ref, kr_ref, u_ref, sga_ref, sgb_ref = rest
    d_model = sga_ref.shape[-1]
    base = n_heads * nope_dim

    def rows_of(ref, r):
        return ref[...] if ref.shape[0] == 1 else ref[r, :]

    for r0 in range(0, x_ref.shape[0], sub):
        r = slice(r0, r0 + sub)
        x = x_ref[r, :]
        h = (_rms(x, nw_ref[...]) * (1.0 + rows_of(sc_ref, r)) + rows_of(sh_ref, r)).astype(BF16)
        cos = cs_ref[r, :LANES]
        sin = cs_ref[r, LANES:]

        def rotate(pair):
            return pair * cos + pltpu.roll(pair, rope_dim, axis=1) * sin

        t = jnp.dot(h, wa_ref[...], preferred_element_type=F32)
        cq = t[:, :q_rank]
        kr = rotate(t[:, q_rank:q_rank + LANES])
        ckv = _rms(t[:, q_rank + LANES:], kvn_ref[...])
        kr_ref[r, :] = kr[:, :rope_dim]
        ckv_ref[r, :] = ckv
        kc_ref[r, :kv_rank] = ckv.astype(BF16)
        kc_ref[r, kv_rank:] = kr.astype(BF16)
        if transposed:
            vt_ref[:, r] = ckv.T.astype(BF16)

        ug = jnp.dot(h, wu_ref[...], preferred_element_type=F32)
        u_ref[r, :] = ug[:, :d_ssm]
        sga_ref[r, :] = _sigmoid(ug[:, d_ssm:d_ssm + d_model])
        sgb_ref[r, :] = _sigmoid(ug[:, d_ssm + d_model:])

        cqn = _rms(cq, qn_ref[...]).astype(BF16)
        q = jnp.dot(cqn, wuq_ref[...], preferred_element_type=F32)
        for hd in range(n_heads):
            nope = q[:, hd * nope_dim:(hd + 1) * nope_dim].astype(BF16)
            q_lat = jnp.dot(nope, wuk_ref[hd], preferred_element_type=F32) * scale
            q_rot = rotate(q[:, base + hd * LANES:base + (hd + 1) * LANES]) * scale
            if transposed:
                q_ref[hd, :kv_rank, r] = q_lat.T.astype(BF16)
                q_ref[hd, kv_rank:, r] = q_rot.T.astype(BF16)
            else:
                q_ref[hd, r, :kv_rank] = q_lat.astype(BF16)
                q_ref[hd, r, kv_rank:] = q_rot.astype(BF16)


def _mixer_in(x, mod, layer, norm_w, cs_table, w, dims, n_batch, transposed, tm=512, sub=256):
    m, d = x.shape
    tm = min(tm, m // n_batch)
    rows = m // n_batch
    tiles = rows // tm
    cs_tiles = cs_table.shape[0] // tm
    nh, q_rank, kv_rank, rope_dim, nope_dim, d_ssm = (dims[k] for k in
                                                       ("n_heads", "q_rank", "kv_rank", "rope_dim", "nope_dim", "d_ssm"))
    kw = kv_rank + LANES
    kern = functools.partial(_mixer_in_kernel, n_heads=nh, q_rank=q_rank, kv_rank=kv_rank, rope_dim=rope_dim,
                             nope_dim=nope_dim, d_ssm=d_ssm, scale=(nope_dim + rope_dim) ** -0.5, transposed=transposed,
                             sub=min(sub, tm))
    row = lambda width: pl.BlockSpec((tm, width), lambda i: (i, 0))
    out_shape = [jax.ShapeDtypeStruct((n_batch, nh, kw, rows) if transposed else (n_batch, nh, rows, kw), BF16),
                 jax.ShapeDtypeStruct((m, kw), BF16),
                 jax.ShapeDtypeStruct((m, kv_rank), F32),
                 jax.ShapeDtypeStruct((m, rope_dim), F32),
                 jax.ShapeDtypeStruct((m, d_ssm), F32),
                 jax.ShapeDtypeStruct((m, d), F32),
                 jax.ShapeDtypeStruct((m, d), F32)]
    out_specs = [pl.BlockSpec((None, nh, kw, tm), lambda i: (i // tiles, 0, 0, i % tiles)) if transposed else
                 pl.BlockSpec((None, nh, tm, kw), lambda i: (i // tiles, 0, i % tiles, 0)),
                 row(kw), row(kv_rank), row(rope_dim), row(d_ssm), row(d), row(d)]
    if transposed:
        out_shape.insert(2, jax.ShapeDtypeStruct((n_batch, kv_rank, rows), BF16))
        out_specs.insert(2, pl.BlockSpec((None, kv_rank, tm), lambda i: (i // tiles, 0, i % tiles)))
    weights = [norm_w, w["wa"], w["wu"], w["qn"], w["kvn"], w["wuq"], w["wuk"]]
    lspec = [_of_layer(a, layer) for a in weights]
    return pl.pallas_call(
        kern,
        out_shape=out_shape,
        grid=(m // tm,),
        in_specs=[row(d), mod.spec(3, tm, d), mod.spec(4, tm, d), lspec[0],
                  pl.BlockSpec((tm, 2 * LANES), lambda i: (i % cs_tiles, 0))] + lspec[1:],
        out_specs=out_specs,
        compiler_params=_cparams("arbitrary"),
        name="mixer_in",
    )(x, mod.arr, mod.arr, norm_w, cs_table, *weights[1:])


def _rope_tables(pos, rope_dim):
    half = rope_dim // 2
    inv = ROPE_BASE ** (-jnp.arange(half, dtype=F32) / half)
    ang = pos.astype(F32)[:, None] * inv[None, :]
    cos, sin = jnp.cos(ang), jnp.sin(ang)
    pad = jnp.zeros((pos.shape[0], LANES - rope_dim), F32)
    return jnp.concatenate([cos, cos, pad, -sin, sin, pad], axis=-1)


def _swap_halves(w, rope_dim):
    half = rope_dim // 2
    return jnp.concatenate([w[..., half:], w[..., :half]], axis=-1)


def _pad_lanes(w):
    return jnp.pad(w, [(0, 0)] * (w.ndim - 1) + [(0, LANES - w.shape[-1])])


def _mixer_in_weights(w_in, q_norm, w_uq, kv_norm, w_uk, dims):
    nh, q_rank, kv_rank, rope_dim, nope_dim, d_ssm = (dims[k] for k in
                                                       ("n_heads", "q_rank", "kv_rank", "rope_dim", "nope_dim", "d_ssm"))
    depth = w_in.shape[0]
    o1, o2 = q_rank + kv_rank, q_rank + kv_rank + rope_dim
    w_cq, w_ckv, w_kr, w_rest = w_in[..., :q_rank], w_in[..., q_rank:o1], w_in[..., o1:o2], w_in[..., o2:]
    assert 2 * rope_dim == LANES
    wa = jnp.concatenate([w_cq, w_kr, _swap_halves(w_kr, rope_dim), w_ckv], axis=-1)
    wq = w_uq.reshape(depth, q_rank, nh, nope_dim + rope_dim)
    wq_nope = wq[..., :nope_dim].reshape(depth, q_rank, nh * nope_dim)
    wq_rope = wq[..., nope_dim:]
    wq_pair = jnp.concatenate([wq_rope, _swap_halves(wq_rope, rope_dim)], axis=-1).reshape(depth, q_rank, nh * LANES)
    return {"wa": wa.astype(BF16), "wu": w_rest.astype(BF16),
            "qn": q_norm.reshape(depth, 1, q_rank), "kvn": kv_norm.reshape(depth, 1, kv_rank),
            "wuq": jnp.concatenate([wq_nope, wq_pair], axis=-1).astype(BF16),
            "wuk": w_uk.transpose(0, 2, 3, 1).astype(BF16)}


_NT = (((1,), (1,)), ((), ()))


def _attn_prompt_kernel(qt_ref, k_ref, vt_ref, o_ref, m_scr, l_scr, acc_scr, s_scr, *, tq, n_heads):
    qi = pl.program_id(1)
    half = tq // 2

    def block(key0, n_keys, q_lanes, first, triangular):
        keys = pl.ds(pl.multiple_of(key0, n_keys), n_keys)
        k = k_ref[keys, :]
        vt = vt_ref[:, keys]
        n_q = q_lanes.stop - q_lanes.start
        if triangular:
            keep = (lax.broadcasted_iota(jnp.int32, (n_keys, n_q), 0) <= lax.broadcasted_iota(jnp.int32, (n_keys, n_q), 1))
        for hd in range(n_heads):
            s_scr[hd, :n_keys, q_lanes] = jnp.dot(k, qt_ref[hd, :, q_lanes], preferred_element_type=F32)
        for hd in range(n_heads):
            s = s_scr[hd, :n_keys, q_lanes]
            if triangular:
                s = jnp.where(keep, s, -jnp.inf)
            if first:
                m_new = jnp.max(s, axis=0, keepdims=True)
                p = jnp.exp(s - m_new)
                l_scr[hd, :, q_lanes] = jnp.sum(p, axis=0, keepdims=True)
                acc_scr[hd, :, q_lanes] = jnp.dot(vt, p.astype(BF16), preferred_element_type=F32)
            else:
                m_prev = m_scr[hd, :, q_lanes]
                m_new = jnp.maximum(m_prev, jnp.max(s, axis=0, keepdims=True))
                alpha = jnp.exp(m_prev - m_new)
                p = jnp.exp(s - m_new)
                l_scr[hd, :, q_lanes] = alpha * l_scr[hd, :, q_lanes] + jnp.sum(p, axis=0, keepdims=True)
                acc_scr[hd, :, q_lanes] = (alpha * acc_scr[hd, :, q_lanes]
                                           + jnp.dot(vt, p.astype(BF16), preferred_element_type=F32))
            m_scr[hd, :, q_lanes] = m_new

    def body(j, carry):
        block(pl.multiple_of(j * tq, tq), tq, slice(0, tq), False, False)
        return carry

    own = pl.multiple_of(qi * tq, tq)
    block(own, half, slice(0, tq), True, True)
    block(own + half, half, slice(half, tq), False, True)
    lax.fori_loop(0, qi, body, 0)
    for hd in range(n_heads):
        o_ref[hd] = (acc_scr[hd] / l_scr[hd]).astype(BF16)


def _attn_prompt(qt, kc, vt, tq=512):
    b, nh, kw, t = qt.shape
    kv_rank = vt.shape[1]
    tq = min(tq, t)
    assert t % tq == 0 and tq % (2 * LANES) == 0 and kv_rank % LANES == 0
    return pl.pallas_call(
        functools.partial(_attn_prompt_kernel, tq=tq, n_heads=nh),
        out_shape=jax.ShapeDtypeStruct((b, nh, kv_rank, t), BF16),
        grid=(b, t // tq),
        in_specs=[pl.BlockSpec((None, nh, kw, tq), lambda bi, qi: (bi, 0, 0, qi)),
                  pl.BlockSpec((None, t, kw), lambda bi, qi: (bi, 0, 0)),
                  pl.BlockSpec((None, kv_rank, t), lambda bi, qi: (bi, 0, 0))],
        out_specs=pl.BlockSpec((None, nh, kv_rank, tq), lambda bi, qi: (bi, 0, 0, qi)),
        scratch_shapes=[pltpu.VMEM((nh, 1, tq), F32), pltpu.VMEM((nh, 1, tq), F32),
                        pltpu.VMEM((nh, kv_rank, tq), F32), pltpu.VMEM((nh, tq, tq), F32)],
        compiler_params=_cparams("arbitrary", "arbitrary"),
        name="attn_prompt",
    )(qt, kc, vt)


def _attn_sample_kernel(pt_ref, q_ref, knew_ref, ckv_hbm, krt_hbm, o_ref, kbuf, krbuf, sem,
                        *, layer, n_pages, page, kv_rank, rope_dim, new_len, n_req):
    b = pl.program_id(0)
    slot = lax.rem(b, 2)

    def page_copies(req, slot_, p):
        pid = pt_ref[req, p]
        off = pl.multiple_of(p * page, page)
        return (pltpu.make_async_copy(ckv_hbm.at[layer, pid], kbuf.at[slot_, pl.ds(off, page), :], sem.at[0, slot_]),
                pltpu.make_async_copy(krt_hbm.at[layer, pid], krbuf.at[slot_, :, pl.ds(off, page)], sem.at[1, slot_]))

    def start_all(req, slot_):
        def body(p, carry):
            for cp in page_copies(req, slot_, p):
                cp.start()
            return carry
        lax.fori_loop(0, n_pages, body, 0, unroll=True)

    def wait_all(slot_):
        pltpu.make_async_copy(kbuf.at[slot_], kbuf.at[slot_], sem.at[0, slot_]).wait()
        pltpu.make_async_copy(krbuf.at[slot_], krbuf.at[slot_], sem.at[1, slot_]).wait()

    @pl.when(b == 0)
    def _():
        start_all(0, 0)

    @pl.when(b + 1 < n_req)
    def _():
        start_all(b + 1, 1 - slot)

    wait_all(slot)

    q = q_ref[...]
    rows = q.shape[0]
    kb = kbuf[slot].astype(BF16)
    s = (lax.dot_general(q[:, :kv_rank], kb, _NT, preferred_element_type=F32)
         + jnp.dot(q[:, kv_rank:kv_rank + rope_dim], krbuf[slot].astype(BF16), preferred_element_type=F32))
    kn = knew_ref[...]
    npad = kn.shape[0]
    sn = lax.dot_general(q, kn, _NT, preferred_element_type=F32)
    q_pos = jnp.bitwise_and(lax.broadcasted_iota(jnp.int32, (rows, npad), 0), new_len - 1)
    k_pos = lax.broadcasted_iota(jnp.int32, (rows, npad), 1)
    sn = jnp.where(k_pos <= q_pos, sn, -jnp.inf)
    m = jnp.maximum(jnp.max(s, axis=1, keepdims=True), jnp.max(sn, axis=1, keepdims=True))
    p = jnp.exp(s - m)
    pn = jnp.exp(sn - m)
    denom = jnp.sum(p, axis=1, keepdims=True) + jnp.sum(pn, axis=1, keepdims=True)
    o = (jnp.dot(p.astype(BF16), kb, preferred_element_type=F32)
         + jnp.dot(pn.astype(BF16), kn[:, :kv_rank], preferred_element_type=F32))
    o_ref[...] = (o / denom).astype(BF16)


def _attn_sample(q, k_new, cache_ckv, cache_krt, page_table, layer, kv_rank, rope_dim, new_len):
    bs, rows, kw = q.shape
    n_pages = page_table.shape[1]
    page = cache_ckv.shape[2]
    past = n_pages * page
    new_pad = k_new.shape[1]
    assert new_len & (new_len - 1) == 0 and page % LANES == 0
    kern = functools.partial(_attn_sample_kernel, layer=layer, n_pages=n_pages, page=page, kv_rank=kv_rank,
                             rope_dim=rope_dim, new_len=new_len, n_req=bs)
    return pl.pallas_call(
        kern,
        out_shape=jax.ShapeDtypeStruct((bs, rows, kv_rank), BF16),
        grid_spec=pltpu.PrefetchScalarGridSpec(
            num_scalar_prefetch=1,
            grid=(bs,),
            in_specs=[pl.BlockSpec((None, rows, kw), lambda b, pt: (b, 0, 0)),
                      pl.BlockSpec((None, new_pad, kw), lambda b, pt: (b, 0, 0)),
                      pl.BlockSpec(memory_space=pl.ANY),
                      pl.BlockSpec(memory_space=pl.ANY)],
            out_specs=pl.BlockSpec((None, rows, kv_rank), lambda b, pt: (b, 0, 0)),
            scratch_shapes=[pltpu.VMEM((2, past, kv_rank), F32),
                            pltpu.VMEM((2, rope_dim, past), F32),
                            pltpu.SemaphoreType.DMA((2, 2))]),
        compiler_params=_cparams("arbitrary"),
        name="attn_sample",
    )(page_table, q, k_new, cache_ckv, cache_krt)


def _gelu_tanh(y):
    return 0.5 * y * (1.0 + jnp.tanh(math.sqrt(2.0 / math.pi) * (y + 0.044715 * (y * y * y))))


def _ssm_kernel(u_ref, kmat_ref, bend_ref, cpow_ref, lre_ref, lim_ref, dsk_ref, h0_ref, z_ref, hn_ref,
                h_scr, s_scr, y_scr, *, chunk, n_seq, n_chunks):
    nb, tt, lanes = u_ref.shape
    rpb = tt // chunk
    half = lre_ref.shape[-1]

    @pl.when(pl.program_id(1) == 0)
    def _():
        h_scr[...] = h0_ref[...]

    u_rows = jnp.concatenate(
        [jnp.concatenate([u_ref[b, pl.ds(s, rpb, stride=chunk), :] for s in range(chunk)], axis=1)
         for b in range(nb)], axis=0)
    ub = u_rows.astype(BF16)
    own = jnp.dot(ub, bend_ref[...], preferred_element_type=F32)
    nt = half // lanes
    tile = lambda x, j: x[:, j * lanes:(j + 1) * lanes]

    def seq_rows(q):
        return pl.ds(q, n_chunks, stride=n_seq)

    if n_chunks > 1:
        for q in range(n_seq):
            for j in range(2 * nt):
                s_scr[j, seq_rows(q), :] = tile(own[q * n_chunks:(q + 1) * n_chunks], j)
    else:
        for j in range(2 * nt):
            s_scr[j] = tile(own, j)
    lre = [tile(lre_ref[...], j) for j in range(nt)]
    lim = [tile(lim_ref[...], j) for j in range(nt)]
    h = h_scr[...]
    h_re = [tile(h, j) for j in range(nt)]
    h_im = [tile(h, nt + j) for j in range(nt)]
    for n in range(n_chunks):
        rows = pl.ds(n * n_seq, n_seq)
        for j in range(nt):
            s_re, s_im = s_scr[j, rows, :], s_scr[nt + j, rows, :]
            s_scr[j, rows, :] = h_re[j]
            s_scr[nt + j, rows, :] = h_im[j]
            h_re[j], h_im[j] = (lre[j] * h_re[j] - lim[j] * h_im[j] + s_re,
                                lre[j] * h_im[j] + lim[j] * h_re[j] + s_im)
    h_scr[...] = jnp.concatenate(h_re + h_im, axis=1)
    hn_ref[...] = h_scr[...]
    if n_chunks > 1:
        entering = jnp.concatenate(
            [jnp.concatenate([s_scr[j, seq_rows(q), :] for j in range(2 * nt)], axis=1) for q in range(n_seq)], axis=0)
    else:
        entering = jnp.concatenate([s_scr[j] for j in range(2 * nt)], axis=1)
    y = (jnp.dot(ub, kmat_ref[...], preferred_element_type=F32)
         + jnp.dot(entering.astype(BF16), cpow_ref[...], preferred_element_type=F32))
    for b in range(nb):
        for t in range(chunk):
            y_scr[b, pl.ds(t, rpb, stride=chunk), :] = y[b * rpb:(b + 1) * rpb, t * lanes:(t + 1) * lanes]
    z_ref[...] = _gelu_tanh(y_scr[...] + dsk_ref[...] * u_ref[...]).astype(BF16)


def _ssm(u3, p, layer, h0, chunk, n_seq, tt=512):
    nb, t, d = u3.shape
    n_sg = d // LANES
    tt = min(tt, t)
    rows = nb * tt // chunk
    n_chunks = rows // n_seq
    n_state = h0.shape[-1]
    per_sg = lambda arr: pl.BlockSpec((None, None) + arr.shape[2:], lambda sg, i: (layer, sg, 0, 0))
    tok = pl.BlockSpec((nb, tt, LANES), lambda sg, i: (0, i, sg))
    state = pl.BlockSpec((None, n_seq, n_state), lambda sg, i: (sg, 0, 0))
    ops = [p[k] for k in ("kmat", "bend", "cpow", "lre", "lim", "dsk")]
    return pl.pallas_call(
        functools.partial(_ssm_kernel, chunk=chunk, n_seq=n_seq, n_chunks=n_chunks),
        out_shape=[jax.ShapeDtypeStruct((nb, t, d), BF16), jax.ShapeDtypeStruct(h0.shape, F32)],
        grid=(n_sg, t // tt),
        in_specs=[tok] + [per_sg(a) for a in ops] + [state],
        out_specs=[tok, state],
        scratch_shapes=[pltpu.VMEM((n_seq, n_state), F32), pltpu.VMEM((n_state // LANES, rows, LANES), F32),
                        pltpu.VMEM((nb, tt, LANES), F32)],
        compiler_params=_cparams("arbitrary", "arbitrary"),
        name="ssm",
    )(u3, *ops, h0)


def _ssm_params(a_re, a_im, log_dt, b_re, b_im, c_re, c_im, d_skip, chunk):
    hp = lax.Precision.HIGHEST
    g, n_state = a_re.shape
    cw = SSM_GROUP_WIDTH
    gl = LANES // cw
    n_sg = g // gl
    sw = gl * n_state
    lam = lax.complex(a_re, a_im)
    lam_dt = lam * jnp.exp(log_dt)[:, None]
    b_bar = ((jnp.exp(lam_dt) - 1.0) / lam)[..., None] * lax.complex(b_re, b_im)
    steps = jnp.arange(chunk + 1, dtype=F32)[:, None, None]
    mag = jnp.exp(lam_dt.real.reshape(n_sg, sw)[None] * steps)
    ang = lam_dt.imag.reshape(n_sg, sw)[None] * steps
    pw_re, pw_im = mag * jnp.cos(ang), mag * jnp.sin(ang)

    same_group = (jnp.arange(LANES)[:, None] // cw) == (jnp.arange(sw)[None, :] // n_state)

    def by_channel(x):
        return jnp.where(same_group, jnp.tile(x.reshape(n_sg, LANES, n_state), (1, 1, gl)), 0.0)

    def times_pw(x_re, x_im, p_re, p_im):
        p_re, p_im = p_re[:, :, None], p_im[:, :, None]
        return x_re * p_re - x_im * p_im, x_re * p_im + x_im * p_re

    bt = b_bar.transpose(0, 2, 1)
    c_re_d, c_im_d = by_channel(c_re), by_channel(c_im)
    bp_re, bp_im = times_pw(by_channel(bt.real), by_channel(bt.imag), pw_re[:chunk], pw_im[:chunk])
    lag = (jnp.einsum("kSaq,Sbq->kSab", bp_re, c_re_d, precision=hp)
           - jnp.einsum("kSaq,Sbq->kSab", bp_im, c_im_d, precision=hp))
    t_idx = jnp.arange(chunk)
    diff = t_idx[None, :] - t_idx[:, None]
    kmat = jnp.where((diff >= 0)[:, :, None, None, None], lag[jnp.clip(diff, 0, chunk - 1)], 0.0)
    kmat = kmat.astype(BF16).transpose(2, 0, 3, 1, 4).reshape(n_sg, chunk * LANES, chunk * LANES)
    bend = jnp.concatenate([bp_re[::-1], bp_im[::-1]], axis=-1).astype(BF16)
    bend = bend.transpose(1, 0, 2, 3).reshape(n_sg, chunk * LANES, 2 * sw)
    cp_re, cp_im = times_pw(c_re_d, c_im_d, pw_re[1:], pw_im[1:])
    cpow = jnp.concatenate([cp_re, -cp_im], axis=-1).astype(BF16)
    cpow = cpow.transpose(1, 3, 0, 2).reshape(n_sg, 2 * sw, chunk * LANES)
    return {"kmat": kmat, "bend": bend, "cpow": cpow,
            "lre": pw_re[chunk].reshape(n_sg, 1, sw), "lim": pw_im[chunk].reshape(n_sg, 1, sw),
            "dsk": d_skip.reshape(n_sg, 1, LANES)}


def _pack_state(h_re, h_im):
    b, g, p = h_re.shape
    gl = LANES // SSM_GROUP_WIDTH
    f = lambda h: h.reshape(b, g // gl, gl * p).transpose(1, 0, 2)
    return jnp.concatenate([f(h_re), f(h_im)], axis=-1)


def _unpack_state(h, n_state):
    n_sg, b, w = h.shape
    f = lambda x: x.transpose(1, 0, 2).reshape(b, -1, n_state)
    return f(h[..., :w // 2]), f(h[..., w // 2:])


def _mixer_out_kernel(x_ref, gm_ref, z_ref, o_ref, sga_ref, sgb_ref, wglu_ref, wuv_ref, wout_ref, y_ref, mixed_scr,
                      *, n_heads, v_dim):
    d = x_ref.shape[-1]
    glu = jnp.dot(z_ref[...], wglu_ref[...], preferred_element_type=F32)
    y_ssm = glu[:, :d] * _sigmoid(glu[:, d:])
    mixed_scr[...] = (sgb_ref[...] * y_ssm).astype(mixed_scr.dtype)
    for hd in range(n_heads):
        cols = slice(hd * v_dim, (hd + 1) * v_dim)
        y_attn = lax.dot_general(o_ref[hd], wuv_ref[hd], (((0,), (0,)), ((), ())), preferred_element_type=F32)
        mixed_scr[:, cols] = mixed_scr[:, cols] + sga_ref[:, cols] * y_attn
    proj = jnp.dot(mixed_scr[...].astype(BF16), wout_ref[...], preferred_element_type=F32)
    y_ref[...] = x_ref[...] + gm_ref[...] * proj


def _mixer_out(x, mod, layer, z, o_lat, sga, sgb, w, n_batch, tm=512):
    m, d = x.shape
    rows = m // n_batch
    tm = min(tm, rows)
    tiles = rows // tm
    _, nh, kv_rank, _ = o_lat.shape
    v_dim = w["wuv"].shape[-1]
    row = lambda width: pl.BlockSpec((tm, width), lambda i: (i, 0))
    weights = [w["wglu"], w["wuv"], w["wout"]]
    return pl.pallas_call(
        functools.partial(_mixer_out_kernel, n_heads=nh, v_dim=v_dim),
        out_shape=jax.ShapeDtypeStruct((m, d), F32),
        grid=(m // tm,),
        in_specs=[row(d), mod.spec(5, tm, d), row(d),
                  pl.BlockSpec((None, nh, kv_rank, tm), lambda i: (i // tiles, 0, 0, i % tiles)),
                  row(d), row(d)] + [_of_layer(a, layer) for a in weights],
        out_specs=row(d),
        scratch_shapes=[pltpu.VMEM((tm, d), F32)],
        compiler_params=_cparams("arbitrary"),
        name="mixer_out",
    )(x, mod.arr, z, o_lat, sga, sgb, *weights)


def _layer(x, mod, layer, n_batch, cs_table, w, dims, attend, transposed, ssm_seqs, ssm_chunk, h0, final_w):
    m, d = x.shape
    x = _ffn(x, mod, 0, layer, w["norm_ffn1"], w["ffn1_up"], w["ffn1_down"])
    *qkv, ckv, kr, u, sga, sgb = _mixer_in(x, mod, layer, w["norm_mix"], cs_table, w["mix_in"], dims, n_batch, transposed)
    o_lat = attend(*qkv)
    z, h_n = _ssm(u.reshape(n_batch, m // n_batch, u.shape[-1]), w["ssm"][ssm_chunk], layer, h0, ssm_chunk, ssm_seqs)
    x = _mixer_out(x, mod, layer, z.reshape(m, z.shape[-1]), o_lat, sga, sgb, w["mix_out"], n_batch)
    x = _ffn(x, mod, 6, layer, w["norm_ffn2"], w["ffn2_up"], w["ffn2_down"], final_w=final_w)
    return x, ckv, kr, h_n


def kernel(x_prompt, x_sample, c_prompt, c_sample, cache_ckv, cache_kr, state_ssm_re, state_ssm_im, page_table, ada_w, ada_b, norm_ffn1, ffn1_up, ffn1_down, norm_mix, w_in, q_norm, w_uq, kv_norm, w_uk, w_uv, ssm_a_re, ssm_a_im, ssm_log_dt, ssm_b_re, ssm_b_im, ssm_c_re, ssm_c_im, ssm_d, w_glu, w_out, norm_ffn2, ffn2_up, ffn2_down, final_norm):
    bp, tp, d = x_prompt.shape
    bs, ts, _ = x_sample.shape
    depth = ada_w.shape[0]
    q_rank, kv_rank = q_norm.shape[-1], kv_norm.shape[-1]
    rope_dim = cache_kr.shape[-1]
    nh, nope_dim = w_uk.shape[2], w_uk.shape[3]
    n_groups, n_state = ssm_a_re.shape[1], ssm_a_re.shape[2]
    dims = {"n_heads": nh, "q_rank": q_rank, "kv_rank": kv_rank, "rope_dim": rope_dim, "nope_dim": nope_dim,
            "d_ssm": ssm_d.shape[-1]}
    past_len = page_table.shape[1] * cache_ckv.shape[2]
    sample_chunk = ts
    prompt_chunk = ts if tp % ts == 0 else tp
    cache_krt = jnp.swapaxes(cache_kr, 2, 3)

    ssm_args = (ssm_a_re, ssm_a_im, ssm_log_dt, ssm_b_re, ssm_b_im, ssm_c_re, ssm_c_im, ssm_d)
    w = {"norm_ffn1": norm_ffn1.reshape(depth, 1, d), "norm_mix": norm_mix.reshape(depth, 1, d),
         "norm_ffn2": norm_ffn2.reshape(depth, 1, d),
         "ffn1_up": ffn1_up.astype(BF16), "ffn1_down": ffn1_down.astype(BF16),
         "ffn2_up": ffn2_up.astype(BF16), "ffn2_down": ffn2_down.astype(BF16),
         "mix_in": _mixer_in_weights(w_in, q_norm, w_uq, kv_norm, w_uk, dims),
         "mix_out": {"wglu": w_glu.astype(BF16), "wuv": w_uv.transpose(0, 2, 1, 3).astype(BF16),
                     "wout": w_out.astype(BF16)},
         "ssm": {c: jax.vmap(functools.partial(_ssm_params, chunk=c))(*ssm_args) for c in {prompt_chunk, sample_chunk}}}
    final_w = final_norm.reshape(1, 1, d)

    mod_all = _adaln(jnp.concatenate([c_prompt, c_sample], axis=0), ada_w, ada_b)
    mod_p_arr = mod_all[:, :bp].reshape(depth, bp, N_MODULATIONS, 1, d)
    mod_s_arr = jnp.repeat(mod_all[:, bp:], ts, axis=1)
    cs_p = _rope_tables(jnp.arange(tp), rope_dim)
    cs_s = jnp.tile(_rope_tables(past_len + jnp.arange(ts), rope_dim), (bs, 1))

    xp = x_prompt.reshape(bp * tp, d)
    xs = x_sample.reshape(bs * ts, d)
    zeros_state = jnp.zeros((bp, n_groups, n_state), F32)
    h0_p = _pack_state(zeros_state, zeros_state)
    outs = {k: [] for k in ("ckv_p", "kr_p", "hre_p", "him_p", "ckv_s", "kr_s", "hre_s", "him_s")}
    for l in range(depth):
        last = final_w if l == depth - 1 else None

        xp, ckv, kr, h_n = _layer(
            xp, _Mod(mod_p_arr, l, False, tp), l, bp, cs_p, w, dims,
            lambda qt, kc, vt: _attn_prompt(qt, kc.reshape(bp, tp, kc.shape[-1]), vt), True, bp, prompt_chunk,
            h0_p, last)
        h_re, h_im = _unpack_state(h_n, n_state)
        outs["ckv_p"].append(ckv.reshape(bp, tp, kv_rank)); outs["kr_p"].append(kr.reshape(bp, tp, rope_dim))
        outs["hre_p"].append(h_re); outs["him_p"].append(h_im)

        def attend_sample(q, kc, l=l):
            kw = q.shape[-1]
            qb = q.reshape(nh, bs, ts, kw).transpose(1, 0, 2, 3).reshape(bs, nh * ts, kw)
            k_new = jnp.pad(kc.reshape(bs, ts, kw), ((0, 0), (0, 16 - ts), (0, 0)))
            o = _attn_sample(qb, k_new, cache_ckv, cache_krt, page_table, l, kv_rank, rope_dim, ts)
            return o.reshape(bs, nh, ts, kv_rank).transpose(1, 3, 0, 2).reshape(1, nh, kv_rank, bs * ts)

        xs, ckv, kr, h_n = _layer(
            xs, _Mod(mod_s_arr, l, True, ts), l, 1, cs_s, w, dims,
            attend_sample, False, bs, sample_chunk, _pack_state(state_ssm_re[l], state_ssm_im[l]), last)
        h_re, h_im = _unpack_state(h_n, n_state)
        outs["ckv_s"].append(ckv.reshape(bs, ts, kv_rank)); outs["kr_s"].append(kr.reshape(bs, ts, rope_dim))
        outs["hre_s"].append(h_re); outs["him_s"].append(h_im)

    st = lambda k: jnp.stack(outs[k])
    return (xp.reshape(bp, tp, d), xs.reshape(bs, ts, d),
            st("ckv_p"), st("kr_p"), st("hre_p"), st("him_p"),
            st("ckv_s"), st("kr_s"), st("hre_s"), st("him_s"))
```

```python
import functools
import math

import jax
import jax.numpy as jnp
from jax import lax
from jax.experimental import pallas as pl
from jax.experimental.pallas import tpu as pltpu

F32 = jnp.float32
BF16 = jnp.bfloat16

NORM_EPS = 1e-6
ROPE_BASE = 10000.0
SSM_GROUP_WIDTH = 16
N_MODULATIONS = 9
LANES = 128
VMEM_LIMIT_BYTES = 56 * 1024 * 1024
FFN_CHUNK = 256


def _cparams(*sem):
    return pltpu.CompilerParams(dimension_semantics=sem, vmem_limit_bytes=VMEM_LIMIT_BYTES)


def _sigmoid(x):
    return 1.0 / (1.0 + jnp.exp(-x))


def _rms(x, w):
    return x * lax.rsqrt(jnp.mean(x * x, axis=-1, keepdims=True) + NORM_EPS) * w


def _of_layer(arr, layer):
    zeros = (0,) * (arr.ndim - 1)
    return pl.BlockSpec((None,) + arr.shape[1:], lambda *_: (layer,) + zeros, pipeline_mode=pl.Buffered(1))


def _adaln_kernel(c_ref, w_ref, b_ref, o_ref):
    c = c_ref[...]
    a = (c * _sigmoid(c)).astype(BF16)
    o_ref[...] = jnp.dot(a, w_ref[...].astype(BF16), preferred_element_type=F32) + b_ref[...]


def _adaln(c_all, ada_w, ada_b, tn=1536):
    depth, d, n = ada_w.shape
    rows = c_all.shape[0]
    return pl.pallas_call(
        _adaln_kernel,
        out_shape=jax.ShapeDtypeStruct((depth, rows, n), F32),
        grid=(depth, n // tn),
        in_specs=[pl.BlockSpec((rows, d), lambda l, j: (0, 0)),
                  pl.BlockSpec((None, d, tn), lambda l, j: (l, 0, j)),
                  pl.BlockSpec((None, 1, tn), lambda l, j: (l, 0, j))],
        out_specs=pl.BlockSpec((None, rows, tn), lambda l, j: (l, 0, j)),
        compiler_params=_cparams("arbitrary", "arbitrary"),
        name="adaln",
    )(c_all, ada_w, ada_b.reshape(depth, 1, n))


class _Mod:
    def __init__(self, arr, layer, per_token, rows_per_batch):
        self.arr = arr
        self.layer = layer
        self.per_token = per_token
        self.rows_per_batch = rows_per_batch

    def spec(self, k, tm, d):
        layer = self.layer
        if self.per_token:
            return pl.BlockSpec((None, tm, d), lambda i: (layer, i, k))
        tiles_per_batch = self.rows_per_batch // tm
        return pl.BlockSpec((None, None, None, 1, d), lambda i: (layer, i // tiles_per_batch, k, 0, 0))


def _ffn_kernel(x_ref, sh_ref, sc_ref, g_ref, nw_ref, wup_ref, wdn_ref, *rest, final):
    if final:
        fn_ref, o_ref, h_scr, acc_scr = rest
    else:
        o_ref, h_scr, acc_scr = rest
    d_ff = wdn_ref.shape[0]
    x = x_ref[...]
    h_scr[...] = (_rms(x, nw_ref[...]) * (1.0 + sc_ref[...]) + sh_ref[...]).astype(BF16)

    def down(c):
        cols = slice(c * FFN_CHUNK, (c + 1) * FFN_CHUNK)
        gate_cols = slice(d_ff + c * FFN_CHUNK, d_ff + (c + 1) * FFN_CHUNK)
        h = h_scr[...]
        a = jnp.dot(h, wup_ref[:, cols], preferred_element_type=F32)
        b = jnp.dot(h, wup_ref[:, gate_cols], preferred_element_type=F32)
        act = (a * _sigmoid(a) * b).astype(BF16)
        return jnp.dot(act, wdn_ref[cols, :], preferred_element_type=F32)

    n_chunks = d_ff // FFN_CHUNK
    acc_scr[...] = down(0)
    for c in range(1, n_chunks - 1):
        acc_scr[...] += down(c)
    y = x + (0.5 * g_ref[...]) * (acc_scr[...] + down(n_chunks - 1))
    if final:
        y = _rms(y, fn_ref[...])
    o_ref[...] = y


def _ffn(x, mod, k0, layer, norm_w, w_up, w_down, final_w=None, tm=1024):
    m, d = x.shape
    tm = min(tm, m if mod.per_token else mod.rows_per_batch)
    d_ff = w_down.shape[1]
    assert d_ff % FFN_CHUNK == 0
    final = final_w is not None
    in_specs = [pl.BlockSpec((tm, d), lambda i: (i, 0)),
                mod.spec(k0, tm, d), mod.spec(k0 + 1, tm, d), mod.spec(k0 + 2, tm, d),
                _of_layer(norm_w, layer), _of_layer(w_up, layer), _of_layer(w_down, layer)]
    args = [x, mod.arr, mod.arr, mod.arr, norm_w, w_up, w_down]
    if final:
        in_specs.append(_of_layer(final_w, 0))
        args.append(final_w)
    return pl.pallas_call(
        functools.partial(_ffn_kernel, final=final),
        out_shape=jax.ShapeDtypeStruct((m, d), F32),
        grid=(m // tm,),
        in_specs=in_specs,
        out_specs=pl.BlockSpec((tm, d), lambda i: (i, 0)),
        scratch_shapes=[pltpu.VMEM((tm, d), BF16), pltpu.VMEM((tm, d), F32)],
        compiler_params=_cparams("arbitrary"),
        name="ffn",
    )(*args)


def _mixer_in_kernel(x_ref, sh_ref, sc_ref, nw_ref, cs_ref, wa_ref, wu_ref, qn_ref, kvn_ref, wuq_ref, wuk_ref,
                     q_ref, kc_ref, *rest,
                     n_heads, q_rank, kv_rank, rope_dim, nope_dim, d_ssm, scale, transposed, sub):
    if transposed:
        vt_ref, ckv_ref, kr_ref, u_ref, sga_ref, sgb_ref = rest
    else:
        ckv_ref, kr_ref, u_ref, sga_ref, sgb_ref = rest
    d_model = sga_ref.shape[-1]
    base = n_heads * nope_dim

    def rows_of(ref, r):
        return ref[...] if ref.shape[0] == 1 else ref[r, :]

    for r0 in range(0, x_ref.shape[0], sub):
        r = slice(r0, r0 + sub)
        x = x_ref[r, :]
        h = (_rms(x, nw_ref[...]) * (1.0 + rows_of(sc_ref, r)) + rows_of(sh_ref, r)).astype(BF16)
        cos = cs_ref[r, :LANES]
        sin = cs_ref[r, LANES:]

        def rotate(pair):
            return pair * cos + pltpu.roll(pair, rope_dim, axis=1) * sin

        t = jnp.dot(h, wa_ref[...], preferred_element_type=F32)
        cq = t[:, :q_rank]
        kr = rotate(t[:, q_rank:q_rank + LANES])
        ckv = _rms(t[:, q_rank + LANES:], kvn_ref[...])
        kr_ref[r, :] = kr[:, :rope_dim]
        ckv_ref[r, :] = ckv
        kc_ref[r, :kv_rank] = ckv.astype(BF16)
        kc_ref[r, kv_rank:] = kr.astype(BF16)
        if transposed:
            vt_ref[:, r] = ckv.T.astype(BF16)

        ug = jnp.dot(h, wu_ref[...], preferred_element_type=F32)
        u_ref[r, :] = ug[:, :d_ssm]
        sga_ref[r, :] = _sigmoid(ug[:, d_ssm:d_ssm + d_model])
        sgb_ref[r, :] = _sigmoid(ug[:, d_ssm + d_model:])

        cqn = _rms(cq, qn_ref[...]).astype(BF16)
        q = jnp.dot(cqn, wuq_ref[...], preferred_element_type=F32)
        for hd in range(n_heads):
            nope = q[:, hd * nope_dim:(hd + 1) * nope_dim].astype(BF16)
            q_lat = jnp.dot(nope, wuk_ref[hd], preferred_element_type=F32) * scale
            q_rot = rotate(q[:, base + hd * LANES:base + (hd + 1) * LANES]) * scale
            if transposed:
                q_ref[hd, :kv_rank, r] = q_lat.T.astype(BF16)
                q_ref[hd, kv_rank:, r] = q_rot.T.astype(BF16)
            else:
                q_ref[hd, r, :kv_rank] = q_lat.astype(BF16)
                q_ref[hd, r, kv_rank:] = q_rot.astype(BF16)


def _mixer_in(x, mod, layer, norm_w, cs_table, w, dims, n_batch, transposed, tm=512, sub=256):
    m, d = x.shape
    tm = min(tm, m // n_batch)
    rows = m // n_batch
    tiles = rows // tm
    cs_tiles = cs_table.shape[0] // tm
    nh, q_rank, kv_rank, rope_dim, nope_dim, d_ssm = (dims[k] for k in
                                                       ("n_heads", "q_rank", "kv_rank", "rope_dim", "nope_dim", "d_ssm"))
    kw = kv_rank + LANES
    kern = functools.partial(_mixer_in_kernel, n_heads=nh, q_rank=q_rank, kv_rank=kv_rank, rope_dim=rope_dim,
                             nope_dim=nope_dim, d_ssm=d_ssm, scale=(nope_dim + rope_dim) ** -0.5, transposed=transposed,
                             sub=min(sub, tm))
    row = lambda width: pl.BlockSpec((tm, width), lambda i: (i, 0))
    out_shape = [jax.ShapeDtypeStruct((n_batch, nh, kw, rows) if transposed else (n_batch, nh, rows, kw), BF16),
                 jax.ShapeDtypeStruct((m, kw), BF16),
                 jax.ShapeDtypeStruct((m, kv_rank), F32),
                 jax.ShapeDtypeStruct((m, rope_dim), F32),
                 jax.ShapeDtypeStruct((m, d_ssm), F32),
                 jax.ShapeDtypeStruct((m, d), F32),
                 jax.ShapeDtypeStruct((m, d), F32)]
    out_specs = [pl.BlockSpec((None, nh, kw, tm), lambda i: (i // tiles, 0, 0, i % tiles)) if transposed else
                 pl.BlockSpec((None, nh, tm, kw), lambda i: (i // tiles, 0, i % tiles, 0)),
                 row(kw), row(kv_rank), row(rope_dim), row(d_ssm), row(d), row(d)]
    if transposed:
        out_shape.insert(2, jax.ShapeDtypeStruct((n_batch, kv_rank, rows), BF16))
        out_specs.insert(2, pl.BlockSpec((None, kv_rank, tm), lambda i: (i // tiles, 0, i % tiles)))
    weights = [norm_w, w["wa"], w["wu"], w["qn"], w["kvn"], w["wuq"], w["wuk"]]
    lspec = [_of_layer(a, layer) for a in weights]
    return pl.pallas_call(
        kern,
        out_shape=out_shape,
        grid=(m // tm,),
        in_specs=[row(d), mod.spec(3, tm, d), mod.spec(4, tm, d), lspec[0],
                  pl.BlockSpec((tm, 2 * LANES), lambda i: (i % cs_tiles, 0))] + lspec[1:],
        out_specs=out_specs,
        compiler_params=_cparams("arbitrary"),
        name="mixer_in",
    )(x, mod.arr, mod.arr, norm_w, cs_table, *weights[1:])


def _rope_tables(pos, rope_dim):
    half = rope_dim // 2
    inv = ROPE_BASE ** (-jnp.arange(half, dtype=F32) / half)
    ang = pos.astype(F32)[:, None] * inv[None, :]
    cos, sin = jnp.cos(ang), jnp.sin(ang)
    pad = jnp.zeros((pos.shape[0], LANES - rope_dim), F32)
    return jnp.concatenate([cos, cos, pad, -sin, sin, pad], axis=-1)


def _swap_halves(w, rope_dim):
    half = rope_dim // 2
    return jnp.concatenate([w[..., half:], w[..., :half]], axis=-1)


def _pad_lanes(w):
    return jnp.pad(w, [(0, 0)] * (w.ndim - 1) + [(0, LANES - w.shape[-1])])


def _mixer_in_weights(w_in, q_norm, w_uq, kv_norm, w_uk, dims):
    nh, q_rank, kv_rank, rope_dim, nope_dim, d_ssm = (dims[k] for k in
                                                       ("n_heads", "q_rank", "kv_rank", "rope_dim", "nope_dim", "d_ssm"))
    depth = w_in.shape[0]
    o1, o2 = q_rank + kv_rank, q_rank + kv_rank + rope_dim
    w_cq, w_ckv, w_kr, w_rest = w_in[..., :q_rank], w_in[..., q_rank:o1], w_in[..., o1:o2], w_in[..., o2:]
    assert 2 * rope_dim == LANES
    wa = jnp.concatenate([w_cq, w_kr, _swap_halves(w_kr, rope_dim), w_ckv], axis=-1)
    wq = w_uq.reshape(depth, q_rank, nh, nope_dim + rope_dim)
    wq_nope = wq[..., :nope_dim].reshape(depth, q_rank, nh * nope_dim)
    wq_rope = wq[..., nope_dim:]
    wq_pair = jnp.concatenate([wq_rope, _swap_halves(wq_rope, rope_dim)], axis=-1).reshape(depth, q_rank, nh * LANES)
    return {"wa": wa.astype(BF16), "wu": w_rest.astype(BF16),
            "qn": q_norm.reshape(depth, 1, q_rank), "kvn": kv_norm.reshape(depth, 1, kv_rank),
            "wuq": jnp.concatenate([wq_nope, wq_pair], axis=-1).astype(BF16),
            "wuk": w_uk.transpose(0, 2, 3, 1).astype(BF16)}


_NT = (((1,), (1,)), ((), ()))


def _attn_prompt_kernel(qt_ref, k_ref, vt_ref, o_ref, m_scr, l_scr, acc_scr, s_scr, *, tq, n_heads):
    qi = pl.program_id(1)
    half = tq // 2

    def block(key0, n_keys, q_lanes, first, triangular):
        keys = pl.ds(pl.multiple_of(key0, n_keys), n_keys)
        k = k_ref[keys, :]
        vt = vt_ref[:, keys]
        n_q = q_lanes.stop - q_lanes.start
        if triangular:
            keep = (lax.broadcasted_iota(jnp.int32, (n_keys, n_q), 0) <= lax.broadcasted_iota(jnp.int32, (n_keys, n_q), 1))
        for hd in range(n_heads):
            s_scr[hd, :n_keys, q_lanes] = jnp.dot(k, qt_ref[hd, :, q_lanes], preferred_element_type=F32)
        for hd in range(n_heads):
            s = s_scr[hd, :n_keys, q_lanes]
            if triangular:
                s = jnp.where(keep, s, -jnp.inf)
            if first:
                m_new = jnp.max(s, axis=0, keepdims=True)
                p = jnp.exp(s - m_new)
                l_scr[hd, :, q_lanes] = jnp.sum(p, axis=0, keepdims=True)
                acc_scr[hd, :, q_lanes] = jnp.dot(vt, p.astype(BF16), preferred_element_type=F32)
            else:
                m_prev = m_scr[hd, :, q_lanes]
                m_new = jnp.maximum(m_prev, jnp.max(s, axis=0, keepdims=True))
                alpha = jnp.exp(m_prev - m_new)
                p = jnp.exp(s - m_new)
                l_scr[hd, :, q_lanes] = alpha * l_scr[hd, :, q_lanes] + jnp.sum(p, axis=0, keepdims=True)
                acc_scr[hd, :, q_lanes] = (alpha * acc_scr[hd, :, q_lanes]
                                           + jnp.dot(vt, p.astype(BF16), preferred_element_type=F32))
            m_scr[hd, :, q_lanes] = m_new

    def body(j, carry):
        block(pl.multiple_of(j * tq, tq), tq, slice(0, tq), False, False)
        return carry

    own = pl.multiple_of(qi * tq, tq)
    block(own, half, slice(0, tq), True, True)
    block(own + half, half, slice(half, tq), False, True)
    lax.fori_loop(0, qi, body, 0)
    for hd in range(n_heads):
        o_ref[hd] = (acc_scr[hd] / l_scr[hd]).astype(BF16)


def _attn_prompt(qt, kc, vt, tq=512):
    b, nh, kw, t = qt.shape
    kv_rank = vt.shape[1]
    tq = min(tq, t)
    assert t % tq == 0 and tq % (2 * LANES) == 0 and kv_rank % LANES == 0
    return pl.pallas_call(
        functools.partial(_attn_prompt_kernel, tq=tq, n_heads=nh),
        out_shape=jax.ShapeDtypeStruct((b, nh, kv_rank, t), BF16),
        grid=(b, t // tq),
        in_specs=[pl.BlockSpec((None, nh, kw, tq), lambda bi, qi: (bi, 0, 0, qi)),
                  pl.BlockSpec((None, t, kw), lambda bi, qi: (bi, 0, 0)),
                  pl.BlockSpec((None, kv_rank, t), lambda bi, qi: (bi, 0, 0))],
        out_specs=pl.BlockSpec((None, nh, kv_rank, tq), lambda bi, qi: (bi, 0, 0, qi)),
        scratch_shapes=[pltpu.VMEM((nh, 1, tq), F32), pltpu.VMEM((nh, 1, tq), F32),
                        pltpu.VMEM((nh, kv_rank, tq), F32), pltpu.VMEM((nh, tq, tq), F32)],
        compiler_params=_cparams("arbitrary", "arbitrary"),
        name="attn_prompt",
    )(qt, kc, vt)


def _attn_sample_kernel(pt_ref, q_ref, knew_ref, ckv_hbm, krt_hbm, o_ref, kbuf, krbuf, sem,
                        *, layer, n_pages, page, kv_rank, rope_dim, new_len, n_req):
    b = pl.program_id(0)
    slot = lax.rem(b, 2)

    def page_copies(req, slot_, p):
        pid = pt_ref[req, p]
        off = pl.multiple_of(p * page, page)
        return (pltpu.make_async_copy(ckv_hbm.at[layer, pid], kbuf.at[slot_, pl.ds(off, page), :], sem.at[0, slot_]),
                pltpu.make_async_copy(krt_hbm.at[layer, pid], krbuf.at[slot_, :, pl.ds(off, page)], sem.at[1, slot_]))

    def start_all(req, slot_):
        def body(p, carry):
            latent_cp, rotary_cp = page_copies(req, slot_, p)
            latent_cp.start(priority=0)
            rotary_cp.start(priority=1)
            return carry
        lax.fori_loop(0, n_pages, body, 0, unroll=True)

    def wait_all(slot_):
        pltpu.make_async_copy(kbuf.at[slot_], kbuf.at[slot_], sem.at[0, slot_]).wait()
        pltpu.make_async_copy(krbuf.at[slot_], krbuf.at[slot_], sem.at[1, slot_]).wait()

    @pl.when(b == 0)
    def _():
        start_all(0, 0)

    @pl.when(b + 1 < n_req)
    def _():
        start_all(b + 1, 1 - slot)

    wait_all(slot)

    q = q_ref[...]
    rows = q.shape[0]
    kb = kbuf[slot].astype(BF16)
    s = (lax.dot_general(q[:, :kv_rank], kb, _NT, preferred_element_type=F32)
         + jnp.dot(q[:, kv_rank:kv_rank + rope_dim], krbuf[slot].astype(BF16), preferred_element_type=F32))
    kn = knew_ref[...]
    npad = kn.shape[0]
    sn = lax.dot_general(q, kn, _NT, preferred_element_type=F32)
    q_pos = jnp.bitwise_and(lax.broadcasted_iota(jnp.int32, (rows, npad), 0), new_len - 1)
    k_pos = lax.broadcasted_iota(jnp.int32, (rows, npad), 1)
    sn = jnp.where(k_pos <= q_pos, sn, -jnp.inf)
    m = jnp.maximum(jnp.max(s, axis=1, keepdims=True), jnp.max(sn, axis=1, keepdims=True))
    p = jnp.exp(s - m)
    pn = jnp.exp(sn - m)
    denom = jnp.sum(p, axis=1, keepdims=True) + jnp.sum(pn, axis=1, keepdims=True)
    o = (jnp.dot(p.astype(BF16), kb, preferred_element_type=F32)
         + jnp.dot(pn.astype(BF16), kn[:, :kv_rank], preferred_element_type=F32))
    o_ref[...] = (o / denom).astype(BF16)


def _attn_sample(q, k_new, cache_ckv, cache_krt, page_table, layer, kv_rank, rope_dim, new_len):
    bs, rows, kw = q.shape
    n_pages = page_table.shape[1]
    page = cache_ckv.shape[2]
    past = n_pages * page
    new_pad = k_new.shape[1]
    assert new_len & (new_len - 1) == 0 and page % LANES == 0
    kern = functools.partial(_attn_sample_kernel, layer=layer, n_pages=n_pages, page=page, kv_rank=kv_rank,
                             rope_dim=rope_dim, new_len=new_len, n_req=bs)
    return pl.pallas_call(
        kern,
        out_shape=jax.ShapeDtypeStruct((bs, rows, kv_rank), BF16),
        grid_spec=pltpu.PrefetchScalarGridSpec(
            num_scalar_prefetch=1,
            grid=(bs,),
            in_specs=[pl.BlockSpec((None, rows, kw), lambda b, pt: (b, 0, 0)),
                      pl.BlockSpec((None, new_pad, kw), lambda b, pt: (b, 0, 0)),
                      pl.BlockSpec(memory_space=pl.ANY),
                      pl.BlockSpec(memory_space=pl.ANY)],
            out_specs=pl.BlockSpec((None, rows, kv_rank), lambda b, pt: (b, 0, 0)),
            scratch_shapes=[pltpu.VMEM((2, past, kv_rank), F32),
                            pltpu.VMEM((2, rope_dim, past), F32),
                            pltpu.SemaphoreType.DMA((2, 2))]),
        compiler_params=_cparams("arbitrary"),
        name="attn_sample",
    )(page_table, q, k_new, cache_ckv, cache_krt)


def _gelu_tanh(y):
    return 0.5 * y * (1.0 + jnp.tanh(math.sqrt(2.0 / math.pi) * (y + 0.044715 * (y * y * y))))


def _ssm_kernel(u_ref, kmat_ref, bend_ref, cpow_ref, lre_ref, lim_ref, dsk_ref, h0_ref, z_ref, hn_ref,
                h_scr, s_scr, y_scr, *, chunk, n_seq, n_chunks):
    nb, tt, lanes = u_ref.shape
    rpb = tt // chunk
    half = lre_ref.shape[-1]

    @pl.when(pl.program_id(1) == 0)
    def _():
        h_scr[...] = h0_ref[...]

    u_rows = jnp.concatenate(
        [jnp.concatenate([u_ref[b, pl.ds(s, rpb, stride=chunk), :] for s in range(chunk)], axis=1)
         for b in range(nb)], axis=0)
    ub = u_rows.astype(BF16)
    own = jnp.dot(ub, bend_ref[...], preferred_element_type=F32)
    nt = half // lanes
    tile = lambda x, j: x[:, j * lanes:(j + 1) * lanes]

    def seq_rows(q):
        return pl.ds(q, n_chunks, stride=n_seq)

    if n_chunks > 1:
        for q in range(n_seq):
            for j in range(2 * nt):
                s_scr[j, seq_rows(q), :] = tile(own[q * n_chunks:(q + 1) * n_chunks], j)
    else:
        for j in range(2 * nt):
            s_scr[j] = tile(own, j)
    lre = [tile(lre_ref[...], j) for j in range(nt)]
    lim = [tile(lim_ref[...], j) for j in range(nt)]
    h = h_scr[...]
    h_re = [tile(h, j) for j in range(nt)]
    h_im = [tile(h, nt + j) for j in range(nt)]
    for n in range(n_chunks):
        rows = pl.ds(n * n_seq, n_seq)
        for j in range(nt):
            s_re, s_im = s_scr[j, rows, :], s_scr[nt + j, rows, :]
            s_scr[j, rows, :] = h_re[j]
            s_scr[nt + j, rows, :] = h_im[j]
            h_re[j], h_im[j] = (lre[j] * h_re[j] - lim[j] * h_im[j] + s_re,
                                lre[j] * h_im[j] + lim[j] * h_re[j] + s_im)
    h_scr[...] = jnp.concatenate(h_re + h_im, axis=1)
    hn_ref[...] = h_scr[...]
    if n_chunks > 1:
        entering = jnp.concatenate(
            [jnp.concatenate([s_scr[j, seq_rows(q), :] for j in range(2 * nt)], axis=1) for q in range(n_seq)], axis=0)
    else:
        entering = jnp.concatenate([s_scr[j] for j in range(2 * nt)], axis=1)
    y = (jnp.dot(ub, kmat_ref[...], preferred_element_type=F32)
         + jnp.dot(entering.astype(BF16), cpow_ref[...], preferred_element_type=F32))
    for b in range(nb):
        for t in range(chunk):
            y_scr[b, pl.ds(t, rpb, stride=chunk), :] = y[b * rpb:(b + 1) * rpb, t * lanes:(t + 1) * lanes]
    z_ref[...] = _gelu_tanh(y_scr[...] + dsk_ref[...] * u_ref[...]).astype(BF16)


def _ssm(u3, p, layer, h0, chunk, n_seq, tt=512):
    nb, t, d = u3.shape
    n_sg = d // LANES
    tt = min(tt, t)
    rows = nb * tt // chunk
    n_chunks = rows // n_seq
    n_state = h0.shape[-1]
    per_sg = lambda arr: pl.BlockSpec((None, None) + arr.shape[2:], lambda sg, i: (layer, sg, 0, 0))
    tok = pl.BlockSpec((nb, tt, LANES), lambda sg, i: (0, i, sg))
    state = pl.BlockSpec((None, n_seq, n_state), lambda sg, i: (sg, 0, 0))
    ops = [p[k] for k in ("kmat", "bend", "cpow", "lre", "lim", "dsk")]
    return pl.pallas_call(
        functools.partial(_ssm_kernel, chunk=chunk, n_seq=n_seq, n_chunks=n_chunks),
        out_shape=[jax.ShapeDtypeStruct((nb, t, d), BF16), jax.ShapeDtypeStruct(h0.shape, F32)],
        grid=(n_sg, t // tt),
        in_specs=[tok] + [per_sg(a) for a in ops] + [state],
        out_specs=[tok, state],
        scratch_shapes=[pltpu.VMEM((n_seq, n_state), F32), pltpu.VMEM((n_state // LANES, rows, LANES), F32),
                        pltpu.VMEM((nb, tt, LANES), F32)],
        compiler_params=_cparams("arbitrary", "arbitrary"),
        name="ssm",
    )(u3, *ops, h0)


def _ssm_params(a_re, a_im, log_dt, b_re, b_im, c_re, c_im, d_skip, chunk):
    hp = lax.Precision.HIGHEST
    g, n_state = a_re.shape
    cw = SSM_GROUP_WIDTH
    gl = LANES // cw
    n_sg = g // gl
    sw = gl * n_state
    lam = lax.complex(a_re, a_im)
    lam_dt = lam * jnp.exp(log_dt)[:, None]
    b_bar = ((jnp.exp(lam_dt) - 1.0) / lam)[..., None] * lax.complex(b_re, b_im)
    steps = jnp.arange(chunk + 1, dtype=F32)[:, None, None]
    mag = jnp.exp(lam_dt.real.reshape(n_sg, sw)[None] * steps)
    ang = lam_dt.imag.reshape(n_sg, sw)[None] * steps
    pw_re, pw_im = mag * jnp.cos(ang), mag * jnp.sin(ang)

    same_group = (jnp.arange(LANES)[:, None] // cw) == (jnp.arange(sw)[None, :] // n_state)

    def by_channel(x):
        return jnp.where(same_group, jnp.tile(x.reshape(n_sg, LANES, n_state), (1, 1, gl)), 0.0)

    def times_pw(x_re, x_im, p_re, p_im):
        p_re, p_im = p_re[:, :, None], p_im[:, :, None]
        return x_re * p_re - x_im * p_im, x_re * p_im + x_im * p_re

    bt = b_bar.transpose(0, 2, 1)
    c_re_d, c_im_d = by_channel(c_re), by_channel(c_im)
    bp_re, bp_im = times_pw(by_channel(bt.real), by_channel(bt.imag), pw_re[:chunk], pw_im[:chunk])
    lag = (jnp.einsum("kSaq,Sbq->kSab", bp_re, c_re_d, precision=hp)
           - jnp.einsum("kSaq,Sbq->kSab", bp_im, c_im_d, precision=hp))
    t_idx = jnp.arange(chunk)
    diff = t_idx[None, :] - t_idx[:, None]
    kmat = jnp.where((diff >= 0)[:, :, None, None, None], lag[jnp.clip(diff, 0, chunk - 1)], 0.0)
    kmat = kmat.astype(BF16).transpose(2, 0, 3, 1, 4).reshape(n_sg, chunk * LANES, chunk * LANES)
    bend = jnp.concatenate([bp_re[::-1], bp_im[::-1]], axis=-1).astype(BF16)
    bend = bend.transpose(1, 0, 2, 3).reshape(n_sg, chunk * LANES, 2 * sw)
    cp_re, cp_im = times_pw(c_re_d, c_im_d, pw_re[1:], pw_im[1:])
    cpow = jnp.concatenate([cp_re, -cp_im], axis=-1).astype(BF16)
    cpow = cpow.transpose(1, 3, 0, 2).reshape(n_sg, 2 * sw, chunk * LANES)
    return {"kmat": kmat, "bend": bend, "cpow": cpow,
            "lre": pw_re[chunk].reshape(n_sg, 1, sw), "lim": pw_im[chunk].reshape(n_sg, 1, sw),
            "dsk": d_skip.reshape(n_sg, 1, LANES)}


def _pack_state(h_re, h_im):
    b, g, p = h_re.shape
    gl = LANES // SSM_GROUP_WIDTH
    f = lambda h: h.reshape(b, g // gl, gl * p).transpose(1, 0, 2)
    return jnp.concatenate([f(h_re), f(h_im)], axis=-1)


def _unpack_state(h, n_state):
    n_sg, b, w = h.shape
    f = lambda x: x.transpose(1, 0, 2).reshape(b, -1, n_state)
    return f(h[..., :w // 2]), f(h[..., w // 2:])


def _mixer_out_kernel(x_ref, gm_ref, z_ref, o_ref, sga_ref, sgb_ref, wglu_ref, wuv_ref, wout_ref, y_ref, mixed_scr,
                      *, n_heads, v_dim):
    d = x_ref.shape[-1]
    glu = jnp.dot(z_ref[...], wglu_ref[...], preferred_element_type=F32)
    y_ssm = glu[:, :d] * _sigmoid(glu[:, d:])
    mixed_scr[...] = (sgb_ref[...] * y_ssm).astype(mixed_scr.dtype)
    for hd in range(n_heads):
        cols = slice(hd * v_dim, (hd + 1) * v_dim)
        y_attn = lax.dot_general(o_ref[hd], wuv_ref[hd], (((0,), (0,)), ((), ())), preferred_element_type=F32)
        mixed_scr[:, cols] = mixed_scr[:, cols] + sga_ref[:, cols] * y_attn
    proj = jnp.dot(mixed_scr[...].astype(BF16), wout_ref[...], preferred_element_type=F32)
    y_ref[...] = x_ref[...] + gm_ref[...] * proj


def _mixer_out(x, mod, layer, z, o_lat, sga, sgb, w, n_batch, tm=512):
    m, d = x.shape
    rows = m // n_batch
    tm = min(tm, rows)
    tiles = rows // tm
    _, nh, kv_rank, _ = o_lat.shape
    v_dim = w["wuv"].shape[-1]
    row = lambda width: pl.BlockSpec((tm, width), lambda i: (i, 0))
    weights = [w["wglu"], w["wuv"], w["wout"]]
    return pl.pallas_call(
        functools.partial(_mixer_out_kernel, n_heads=nh, v_dim=v_dim),
        out_shape=jax.ShapeDtypeStruct((m, d), F32),
        grid=(m // tm,),
        in_specs=[row(d), mod.spec(5, tm, d), row(d),
                  pl.BlockSpec((None, nh, kv_rank, tm), lambda i: (i // tiles, 0, 0, i % tiles)),
                  row(d), row(d)] + [_of_layer(a, layer) for a in weights],
        out_specs=row(d),
        scratch_shapes=[pltpu.VMEM((tm, d), F32)],
        compiler_params=_cparams("arbitrary"),
        name="mixer_out",
    )(x, mod.arr, z, o_lat, sga, sgb, *weights)


def _layer(x, mod, layer, n_batch, cs_table, w, dims, attend, transposed, ssm_seqs, ssm_chunk, h0, final_w):
    m, d = x.shape
    x = _ffn(x, mod, 0, layer, w["norm_ffn1"], w["ffn1_up"], w["ffn1_down"])
    *qkv, ckv, kr, u, sga, sgb = _mixer_in(x, mod, layer, w["norm_mix"], cs_table, w["mix_in"], dims, n_batch, transposed)
    o_lat = attend(*qkv)
    z, h_n = _ssm(u.reshape(n_batch, m // n_batch, u.shape[-1]), w["ssm"][ssm_chunk], layer, h0, ssm_chunk, ssm_seqs)
    x = _mixer_out(x, mod, layer, z.reshape(m, z.shape[-1]), o_lat, sga, sgb, w["mix_out"], n_batch)
    x = _ffn(x, mod, 6, layer, w["norm_ffn2"], w["ffn2_up"], w["ffn2_down"], final_w=final_w)
    return x, ckv, kr, h_n


def kernel(x_prompt, x_sample, c_prompt, c_sample, cache_ckv, cache_kr, state_ssm_re, state_ssm_im, page_table, ada_w, ada_b, norm_ffn1, ffn1_up, ffn1_down, norm_mix, w_in, q_norm, w_uq, kv_norm, w_uk, w_uv, ssm_a_re, ssm_a_im, ssm_log_dt, ssm_b_re, ssm_b_im, ssm_c_re, ssm_c_im, ssm_d, w_glu, w_out, norm_ffn2, ffn2_up, ffn2_down, final_norm):
    bp, tp, d = x_prompt.shape
    bs, ts, _ = x_sample.shape
    depth = ada_w.shape[0]
    q_rank, kv_rank = q_norm.shape[-1], kv_norm.shape[-1]
    rope_dim = cache_kr.shape[-1]
    nh, nope_dim = w_uk.shape[2], w_uk.shape[3]
    n_groups, n_state = ssm_a_re.shape[1], ssm_a_re.shape[2]
    dims = {"n_heads": nh, "q_rank": q_rank, "kv_rank": kv_rank, "rope_dim": rope_dim, "nope_dim": nope_dim,
            "d_ssm": ssm_d.shape[-1]}
    past_len = page_table.shape[1] * cache_ckv.shape[2]
    sample_chunk = ts
    prompt_chunk = ts if tp % ts == 0 else tp
    cache_krt = jnp.swapaxes(cache_kr, 2, 3)

    ssm_args = (ssm_a_re, ssm_a_im, ssm_log_dt, ssm_b_re, ssm_b_im, ssm_c_re, ssm_c_im, ssm_d)
    w = {"norm_ffn1": norm_ffn1.reshape(depth, 1, d), "norm_mix": norm_mix.reshape(depth, 1, d),
         "norm_ffn2": norm_ffn2.reshape(depth, 1, d),
         "ffn1_up": ffn1_up.astype(BF16), "ffn1_down": ffn1_down.astype(BF16),
         "ffn2_up": ffn2_up.astype(BF16), "ffn2_down": ffn2_down.astype(BF16),
         "mix_in": _mixer_in_weights(w_in, q_norm, w_uq, kv_norm, w_uk, dims),
         "mix_out": {"wglu": w_glu.astype(BF16), "wuv": w_uv.transpose(0, 2, 1, 3).astype(BF16),
                     "wout": w_out.astype(BF16)},
         "ssm": {c: jax.vmap(functools.partial(_ssm_params, chunk=c))(*ssm_args) for c in {prompt_chunk, sample_chunk}}}
    final_w = final_norm.reshape(1, 1, d)

    mod_all = _adaln(jnp.concatenate([c_prompt, c_sample], axis=0), ada_w, ada_b)
    mod_p_arr = mod_all[:, :bp].reshape(depth, bp, N_MODULATIONS, 1, d)
    mod_s_arr = jnp.repeat(mod_all[:, bp:], ts, axis=1)
    cs_p = _rope_tables(jnp.arange(tp), rope_dim)
    cs_s = jnp.tile(_rope_tables(past_len + jnp.arange(ts), rope_dim), (bs, 1))

    xp = x_prompt.reshape(bp * tp, d)
    xs = x_sample.reshape(bs * ts, d)
    zeros_state = jnp.zeros((bp, n_groups, n_state), F32)
    h0_p = _pack_state(zeros_state, zeros_state)
    outs = {k: [] for k in ("ckv_p", "kr_p", "hre_p", "him_p", "ckv_s", "kr_s", "hre_s", "him_s")}
    for l in range(depth):
        last = final_w if l == depth - 1 else None

        xp, ckv, kr, h_n = _layer(
            xp, _Mod(mod_p_arr, l, False, tp), l, bp, cs_p, w, dims,
            lambda qt, kc, vt: _attn_prompt(qt, kc.reshape(bp, tp, kc.shape[-1]), vt), True, bp, prompt_chunk,
            h0_p, last)
        h_re, h_im = _unpack_state(h_n, n_state)
        outs["ckv_p"].append(ckv.reshape(bp, tp, kv_rank)); outs["kr_p"].append(kr.reshape(bp, tp, rope_dim))
        outs["hre_p"].append(h_re); outs["him_p"].append(h_im)

        def attend_sample(q, kc, l=l):
            kw = q.shape[-1]
            qb = q.reshape(nh, bs, ts, kw).transpose(1, 0, 2, 3).reshape(bs, nh * ts, kw)
            k_new = jnp.pad(kc.reshape(bs, ts, kw), ((0, 0), (0, 16 - ts), (0, 0)))
            o = _attn_sample(qb, k_new, cache_ckv, cache_krt, page_table, l, kv_rank, rope_dim, ts)
            return o.reshape(bs, nh, ts, kv_rank).transpose(1, 3, 0, 2).reshape(1, nh, kv_rank, bs * ts)

        xs, ckv, kr, h_n = _layer(
            xs, _Mod(mod_s_arr, l, True, ts), l, 1, cs_s, w, dims,
            attend_sample, False, bs, sample_chunk, _pack_state(state_ssm_re[l], state_ssm_im[l]), last)
        h_re, h_im = _unpack_state(h_n, n_state)
        outs["ckv_s"].append(ckv.reshape(bs, ts, kv_rank)); outs["kr_s"].append(kr.reshape(bs, ts, rope_dim))
        outs["hre_s"].append(h_re); outs["him_s"].append(h_im)

    st = lambda k: jnp.stack(outs[k])
    return (xp.reshape(bp, tp, d), xs.reshape(bs, ts, d),
            st("ckv_p"), st("kr_p"), st("hre_p"), st("him_p"),
            st("ckv_s"), st("kr_s"), st("hre_s"), st("him_s"))
```
